```python
import math
import jax, jax.numpy as jnp
from jax import lax
import numpy as np

D_MODEL = 2048
BATCH = 4
SEQ = 2048
DEPTH = 4

CHUNK = 64
N_MIXERS = 4
HEAD_DIM = 128
N_HEADS = D_MODEL // HEAD_DIM
D_FF = 5632
RG_WIDTH = D_MODEL
RG_BLOCKS = N_HEADS
RG_BLOCK = RG_WIDTH // RG_BLOCKS
RG_CONV = 4
RG_C = 8.0
SC_CONV = 3
Q_BLOCK = 128
EPS = 1e-6

kernel_name = "hybrid_interleaved_streaming_encoder"


def n_occ(m):
    return len(range(m, DEPTH, N_MIXERS))


def rms(x):
    xf = x.astype(jnp.float32)
    return (xf * lax.rsqrt(jnp.mean(xf * xf, axis=-1, keepdims=True) + EPS)).astype(x.dtype)


def modulate(x, g, m):
    return rms(x) * g * (1.0 + m[:, 1][:, None, :]) + m[:, 0][:, None, :]


def swiglu(h, w_gate, w_up, w_down):
    return (jax.nn.silu(h @ w_gate) * (h @ w_up)) @ w_down


def causal_conv(x, w, b=None):
    W = w.shape[0]
    S = x.shape[1]
    xp = jnp.pad(x, ((0, 0), (W - 1, 0), (0, 0)))
    y = w[0] * xp[:, 0:S]
    for k in range(1, W):
        y = y + w[k] * xp[:, k:k + S]
    return y if b is None else y + b


def rglru_mixer(h, w_in, conv_w, conv_b, w_r, b_r, w_i, b_i, lam, w_out):
    B, S, _ = h.shape
    gate_branch, xb = jnp.split(h @ w_in, 2, axis=-1)
    xb = causal_conv(xb, conv_w, conv_b)
    xh = xb.reshape(B, S, RG_BLOCKS, RG_BLOCK)
    r = jax.nn.sigmoid(jnp.einsum('bsgi,gij->bsgj', xh, w_r).reshape(B, S, RG_WIDTH) + b_r)
    ig = jax.nn.sigmoid(jnp.einsum('bsgi,gij->bsgj', xh, w_i).reshape(B, S, RG_WIDTH) + b_i)
    log_a = RG_C * r.astype(jnp.float32) * jax.nn.log_sigmoid(lam.astype(jnp.float32))
    a = jnp.exp(log_a)
    bx = jnp.sqrt(-jnp.expm1(2.0 * log_a)) * (ig * xb).astype(jnp.float32)

    def combine(left, right):
        a1, b1 = left
        a2, b2 = right
        return a1 * a2, a2 * b1 + b2

    _, hs = lax.associative_scan(combine, (a, bx), axis=1)
    y = hs.astype(h.dtype) * jax.nn.gelu(gate_branch)
    return y @ w_out


def shortconv_mixer(h, w_in, conv_w, w_out):
    bg, cg, xv = jnp.split(h @ w_in, 3, axis=-1)
    y = bg * causal_conv(cg * xv, conv_w)
    return y @ w_out


def fox_mixer(h, w_in, b_f, q_gain, k_gain, w_out):
    B, S, D = h.shape
    H, dh = N_HEADS, HEAD_DIM
    q, k, v, g, fl = jnp.split(h @ w_in, [D, 2 * D, 3 * D, 4 * D], axis=-1)
    q = rms(q.reshape(B, S, H, dh)) * q_gain
    k = rms(k.reshape(B, S, H, dh)) * k_gain
    v = v.reshape(B, S, H, dh)
    logf = jax.nn.log_sigmoid((fl + b_f).astype(jnp.float32))
    F = jnp.cumsum(logf, axis=1)
    Fk = F.transpose(0, 2, 1)
    nq = S // Q_BLOCK
    qb = q.reshape(B, nq, Q_BLOCK, H, dh).transpose(1, 0, 2, 3, 4)
    Fq = F.reshape(B, nq, Q_BLOCK, H).transpose(1, 0, 3, 2)
    pos_k = jnp.arange(S)
    scale = 1.0 / math.sqrt(dh)

    def block(args):
        qi, Fi, bi = args
        s = jnp.einsum('bqhd,bkhd->bhqk', qi, k).astype(jnp.float32) * scale
        s = s + Fi[..., None] - Fk[:, :, None, :]
        pos_q = bi * Q_BLOCK + jnp.arange(Q_BLOCK)
        s = jnp.where(pos_k[None, :] <= pos_q[:, None], s, -jnp.inf)
        p = jax.nn.softmax(s, axis=-1).astype(v.dtype)
        return jnp.einsum('bhqk,bkhd->bqhd', p, v)

    o = lax.map(block, (qb, Fq, jnp.arange(nq)))
    o = o.transpose(1, 0, 2, 3, 4).reshape(B, S, D)
    return (o * jax.nn.sigmoid(g)) @ w_out


def hgrn2_mixer(h, w_in, lb, norm_gain, w_out):
    B, S, D = h.shape
    H, dh, C = N_HEADS, HEAD_DIM, CHUNK
    N = S // C
    q, fz, iv, g = jnp.split(h @ w_in, 4, axis=-1)
    logf = jnp.log(lb + (1.0 - lb) * jax.nn.sigmoid(fz.astype(jnp.float32)))
    kk = -jnp.expm1(logf)

    def heads(t):
        return t.astype(jnp.float32).reshape(B, N, C, H, dh).transpose(0, 3, 1, 2, 4)

    qh, kh, vh, gh = heads(q), heads(kk), heads(iv), heads(logf)
    bcum = jnp.cumsum(gh, axis=3)
    b_last = bcum[:, :, :, -1:, :]
    q_in = qh * jnp.exp(bcum)
    k_in = kh * jnp.exp(-bcum)
    tril = jnp.tril(jnp.ones((C, C), dtype=bool))
    scores = jnp.where(tril, jnp.einsum('bhncd,bhnsd->bhncs', q_in, k_in), 0.0)
    o_intra = jnp.einsum('bhncs,bhnse->bhnce', scores, vh)
    U = jnp.einsum('bhncd,bhnce->bhnde', kh * jnp.exp(b_last - bcum), vh)
    decay = jnp.exp(b_last[:, :, :, 0, :])

    def step(S_prev, inp):
        dec, u = inp
        return dec[..., None] * S_prev + u, S_prev

    S0 = jnp.zeros((B, H, dh, dh), jnp.float32)
    _, S_start = lax.scan(step, S0, (decay.transpose(2, 0, 1, 3), U.transpose(2, 0, 1, 3, 4)))
    S_start = S_start.transpose(1, 2, 0, 3, 4)
    o = o_intra + jnp.einsum('bhncd,bhnde->bhnce', q_in, S_start)
    o = o.transpose(0, 2, 3, 1, 4).reshape(B, S, H, dh)
    o = (rms(o) * norm_gain).reshape(B, S, D).astype(h.dtype)
    return (o * jax.nn.silu(g)) @ w_out


def setup_inputs(seed: int = 0) -> dict:
    key = jax.random.key(seed)
    keys = jax.random.split(key, 32)
    cnt = [0]

    def nk():
        k = keys[cnt[0]]
        cnt[0] += 1
        return k

    def w(shape, fan_in, s=1.0):
        return s * jax.random.normal(nk(), shape, jnp.float32) * fan_in ** -0.5

    def gain(shape):
        return 1.0 + 0.1 * jax.random.normal(nk(), shape, jnp.float32)

    def bias(shape, s=0.02):
        return s * jax.random.normal(nk(), shape, jnp.float32)

    D, H, dh = D_MODEL, N_HEADS, HEAD_DIM
    na, nb, nc, nd = n_occ(0), n_occ(1), n_occ(2), n_occ(3)
    x = jax.random.normal(nk(), (BATCH, SEQ, D), jnp.float32)
    c = jax.random.normal(nk(), (BATCH, D), jnp.float32)
    ada_w = w((DEPTH, D, 9 * D), D, 0.5)
    ada_b = bias((DEPTH, 9 * D))
    norm_g = gain((DEPTH, 3, D))
    ffn_w_gate = w((DEPTH, 2, D, D_FF), D)
    ffn_w_up = w((DEPTH, 2, D, D_FF), D)
    ffn_w_down = w((DEPTH, 2, D_FF, D), D_FF)
    rg_w_in = w((na, D, 2 * RG_WIDTH), D)
    rg_conv_w = w((na, RG_CONV, RG_WIDTH), RG_CONV)
    rg_conv_b = bias((na, RG_WIDTH))
    rg_w_r = w((na, RG_BLOCKS, RG_BLOCK, RG_BLOCK), RG_BLOCK)
    rg_b_r = bias((na, RG_WIDTH))
    rg_w_i = w((na, RG_BLOCKS, RG_BLOCK, RG_BLOCK), RG_BLOCK)
    rg_b_i = bias((na, RG_WIDTH))
    u = jax.random.uniform(nk(), (na, RG_WIDTH), jnp.float32, 0.9, 0.999)
    a0 = u ** (1.0 / RG_C)
    rg_lam = jnp.log(a0) - jnp.log1p(-a0)
    rg_w_out = w((na, RG_WIDTH, D), RG_WIDTH)
    sc_w_in = w((nb, D, 3 * D), D)
    sc_conv_w = w((nb, SC_CONV, D), SC_CONV)
    sc_w_out = w((nb, D, D), D)
    fox_w_in = w((nc, D, 4 * D + H), D)
    fox_b_f = 2.0 + bias((nc, H), 0.5)
    fox_q_gain = gain((nc, dh))
    fox_k_gain = gain((nc, dh))
    fox_w_out = w((nc, D, D), D)
    hg_w_in = w((nd, D, 4 * D), D)
    hg_lb_logits = bias((DEPTH, D), 0.1)
    hg_norm_gain = gain((nd, dh))
    hg_w_out = w((nd, D, D), D)
    return {"x": x, "c": c, "ada_w": ada_w, "ada_b": ada_b, "norm_g": norm_g,
            "ffn_w_gate": ffn_w_gate, "ffn_w_up": ffn_w_up, "ffn_w_down": ffn_w_down,
            "rg_w_in": rg_w_in, "rg_conv_w": rg_conv_w, "rg_conv_b": rg_conv_b,
            "rg_w_r": rg_w_r, "rg_b_r": rg_b_r, "rg_w_i": rg_w_i, "rg_b_i": rg_b_i,
            "rg_lam": rg_lam, "rg_w_out": rg_w_out,
            "sc_w_in": sc_w_in, "sc_conv_w": sc_conv_w, "sc_w_out": sc_w_out,
            "fox_w_in": fox_w_in, "fox_b_f": fox_b_f, "fox_q_gain": fox_q_gain,
            "fox_k_gain": fox_k_gain, "fox_w_out": fox_w_out,
            "hg_w_in": hg_w_in, "hg_lb_logits": hg_lb_logits, "hg_norm_gain": hg_norm_gain,
            "hg_w_out": hg_w_out}


def reference(x, c, ada_w, ada_b, norm_g, ffn_w_gate, ffn_w_up, ffn_w_down,
              rg_w_in, rg_conv_w, rg_conv_b, rg_w_r, rg_b_r, rg_w_i, rg_b_i, rg_lam, rg_w_out,
              sc_w_in, sc_conv_w, sc_w_out,
              fox_w_in, fox_b_f, fox_q_gain, fox_k_gain, fox_w_out,
              hg_w_in, hg_lb_logits, hg_norm_gain, hg_w_out):
    B = x.shape[0]
    D = D_MODEL
    sc = jax.nn.silu(c)
    lb_p = jax.nn.softmax(hg_lb_logits.astype(jnp.float32), axis=0)
    lb_table = jnp.cumsum(lb_p, axis=0) - lb_p
    for i in range(DEPTH):
        mod = (sc @ ada_w[i] + ada_b[i]).reshape(B, 3, 3, D)
        h = modulate(x, norm_g[i, 0], mod[:, 0])
        x = x + 0.5 * mod[:, 0, 2][:, None, :] * swiglu(h, ffn_w_gate[i, 0], ffn_w_up[i, 0], ffn_w_down[i, 0])
        h = modulate(x, norm_g[i, 1], mod[:, 1])
        m, j = i % N_MIXERS, i // N_MIXERS
        if m == 0:
            y = rglru_mixer(h, rg_w_in[j], rg_conv_w[j], rg_conv_b[j], rg_w_r[j], rg_b_r[j],
                            rg_w_i[j], rg_b_i[j], rg_lam[j], rg_w_out[j])
        elif m == 1:
            y = shortconv_mixer(h, sc_w_in[j], sc_conv_w[j], sc_w_out[j])
        elif m == 2:
            y = fox_mixer(h, fox_w_in[j], fox_b_f[j], fox_q_gain[j], fox_k_gain[j], fox_w_out[j])
        else:
            y = hgrn2_mixer(h, hg_w_in[j], lb_table[i], hg_norm_gain[j], hg_w_out[j])
        x = x + mod[:, 1, 2][:, None, :] * y
        h = modulate(x, norm_g[i, 2], mod[:, 2])
        x = x + 0.5 * mod[:, 2, 2][:, None, :] * swiglu(h, ffn_w_gate[i, 1], ffn_w_up[i, 1], ffn_w_down[i, 1])
    return x
```

```python
import functools
import math

import jax
import jax.numpy as jnp
from jax import lax
from jax.experimental import pallas as pl
from jax.experimental.pallas import tpu as pltpu

F32 = jnp.float32
BF16 = jnp.bfloat16

EPS = 1e-6
HEAD_DIM = 128
CHUNK = 64
CHUNK_SHIFT = 6
RG_C = 8.0
SUBLANES = 8
VMEM_LIMIT = 56 * 1024 * 1024


def _cparams(n_axes):
    return pltpu.CompilerParams(dimension_semantics=("arbitrary",) * n_axes,
                                vmem_limit_bytes=VMEM_LIMIT)


def _tile(n, preferred):
    if n <= preferred:
        return n
    t = preferred - preferred % HEAD_DIM
    while n % t:
        t -= HEAD_DIM
    return t


def _modulate(x, g, shift, scale):
    ms = jnp.mean(x * x, axis=-1, keepdims=True)
    return (x * lax.rsqrt(ms + EPS)) * (g * (1.0 + scale)) + shift


def _log_sigmoid(x):
    return jnp.minimum(x, 0.0) - jnp.log1p(jnp.exp(-jnp.abs(x)))


def _split3(x):
    hi = x.astype(BF16)
    r1 = x - hi.astype(F32)
    mid = r1.astype(BF16)
    lo = (r1 - mid.astype(F32)).astype(BF16)
    return hi, mid, lo


def _tril_dot(tril, x):
    hi, mid, lo = _split3(x)
    acc = jnp.dot(tril, hi, preferred_element_type=F32)
    acc = acc + jnp.dot(tril, mid, preferred_element_type=F32)
    return acc + jnp.dot(tril, lo, preferred_element_type=F32)


def _ada_kernel(c_ref, w_ref, b_ref, o_ref):
    c = c_ref[...]
    sc = (c * jax.nn.sigmoid(c)).astype(BF16)
    o_ref[...] = jnp.dot(sc, w_ref[...].astype(BF16), preferred_element_type=F32) + b_ref[...]


def _ada_call(c, ada_w, ada_b, tn=1024):
    depth, d, n = ada_w.shape
    b = c.shape[0]
    rows = -(-b // SUBLANES) * SUBLANES
    c_pad = jnp.pad(c, ((0, rows - b), (0, 0)))
    tn = _tile(n, tn)
    out = pl.pallas_call(
        _ada_kernel,
        out_shape=jax.ShapeDtypeStruct((depth, rows, n), F32),
        grid=(depth, n // tn),
        in_specs=[
            pl.BlockSpec((rows, d), lambda l, j: (0, 0)),
            pl.BlockSpec((None, d, tn), lambda l, j: (l, 0, j)),
            pl.BlockSpec((None, 1, tn), lambda l, j: (l, 0, j)),
        ],
        out_specs=pl.BlockSpec((None, rows, tn), lambda l, j: (l, 0, j)),
        compiler_params=_cparams(2),
        name="ada",
    )(c_pad, ada_w, ada_b.reshape(depth, 1, n))
    return out[:, :b]


def _ffn_kernel(x_ref, shift_ref, scale_ref, gate_ref, g_ref, wg_ref, wu_ref, wd_ref, o_ref, h_scr):
    f = pl.program_id(1)

    @pl.when(f == 0)
    def _():
        h = _modulate(x_ref[...], g_ref[...], shift_ref[...], scale_ref[...])
        h_scr[...] = h.astype(BF16)
        o_ref[...] = jnp.zeros_like(o_ref)

    h = h_scr[...]
    gg = jnp.dot(h, wg_ref[...], preferred_element_type=F32)
    uu = jnp.dot(h, wu_ref[...], preferred_element_type=F32)
    a = ((gg * jax.nn.sigmoid(gg)) * uu).astype(BF16)
    o_ref[...] += jnp.dot(a, wd_ref[...], preferred_element_type=F32)

    @pl.when(f == pl.num_programs(1) - 1)
    def _():
        o_ref[...] = x_ref[...] + (0.5 * gate_ref[...]) * o_ref[...]


def _mod_spec(d, base, tiles_per_batch, n_grid_axes, col_axis=None, tn=None):
    width = d if tn is None else tn

    def index(*ids):
        col = 0 if col_axis is None else ids[col_axis]
        return (base + 9 * (ids[0] // tiles_per_batch), 0, col)

    del n_grid_axes
    return pl.BlockSpec((None, 1, width), index)


def _ffn_call(x, mod_r, ng_r, wg, wu, wd, layer, which, sub, batch, tm=512, tf=512):
    m, d = x.shape
    f_dim = wg.shape[-1]
    tm = min(tm, m // batch)
    tf = _tile(f_dim, tf)
    tpb = (m // batch) // tm
    base = layer * batch * 9 + sub * 3
    return pl.pallas_call(
        _ffn_kernel,
        out_shape=jax.ShapeDtypeStruct((m, d), F32),
        grid=(m // tm, f_dim // tf),
        in_specs=[
            pl.BlockSpec((tm, d), lambda i, f: (i, 0)),
            _mod_spec(d, base + 0, tpb, 2),
            _mod_spec(d, base + 1, tpb, 2),
            _mod_spec(d, base + 2, tpb, 2),
            pl.BlockSpec((None, 1, d), lambda i, f: (layer * 3 + sub, 0, 0)),
            pl.BlockSpec((None, None, d, tf), lambda i, f: (layer, which, 0, f)),
            pl.BlockSpec((None, None, d, tf), lambda i, f: (layer, which, 0, f)),
            pl.BlockSpec((None, None, tf, d), lambda i, f: (layer, which, f, 0)),
        ],
        out_specs=pl.BlockSpec((tm, d), lambda i, f: (i, 0)),
        scratch_shapes=[pltpu.VMEM((tm, d), BF16)],
        compiler_params=_cparams(2),
        name="ffn",
    )(x, mod_r, mod_r, mod_r, ng_r, wg, wu, wd)


def _mm_mod_kernel(x_ref, shift_ref, scale_ref, g_ref, w_ref, o_ref, h_scr):
    @pl.when(pl.program_id(1) == 0)
    def _():
        h = _modulate(x_ref[...], g_ref[...], shift_ref[...], scale_ref[...])
        h_scr[...] = h.astype(BF16)

    o_ref[...] = jnp.dot(h_scr[...], w_ref[...], preferred_element_type=F32).astype(o_ref.dtype)


def _mm_mod_call(x, mod_r, ng_r, w, widx, n_out, layer, batch, out_dtype, tm=512, tn=1024):
    m, d = x.shape
    tm = min(tm, m // batch)
    tn = _tile(n_out, tn)
    tpb = (m // batch) // tm
    base = layer * batch * 9 + 3
    return pl.pallas_call(
        _mm_mod_kernel,
        out_shape=jax.ShapeDtypeStruct((m, n_out), out_dtype),
        grid=(m // tm, n_out // tn),
        in_specs=[
            pl.BlockSpec((tm, d), lambda i, j: (i, 0)),
            _mod_spec(d, base + 0, tpb, 2),
            _mod_spec(d, base + 1, tpb, 2),
            pl.BlockSpec((None, 1, d), lambda i, j: (layer * 3 + 1, 0, 0)),
            pl.BlockSpec((None, d, tn), lambda i, j: (widx, 0, j)),
        ],
        out_specs=pl.BlockSpec((tm, tn), lambda i, j: (i, j)),
        scratch_shapes=[pltpu.VMEM((tm, d), BF16)],
        compiler_params=_cparams(2),
        name="in_proj",
    )(x, mod_r, mod_r, ng_r, w)


def _mm_res_kernel(y_ref, w_ref, x_ref, gate_ref, o_ref):
    acc = jnp.dot(y_ref[...], w_ref[...], preferred_element_type=F32)
    o_ref[...] = x_ref[...] + gate_ref[...] * acc


def _mm_res_call(y, w, widx, x, mod_r, layer, batch, tm=512, tn=1024):
    m, k = y.shape
    d = x.shape[1]
    tm = min(tm, m // batch)
    tn = _tile(d, tn)
    tpb = (m // batch) // tm
    base = layer * batch * 9 + 3 + 2
    return pl.pallas_call(
        _mm_res_kernel,
        out_shape=jax.ShapeDtypeStruct((m, d), F32),
        grid=(m // tm, d // tn),
        in_specs=[
            pl.BlockSpec((tm, k), lambda i, j: (i, 0)),
            pl.BlockSpec((None, k, tn), lambda i, j: (widx, 0, j)),
            pl.BlockSpec((tm, tn), lambda i, j: (i, j)),
            _mod_spec(d, base, tpb, 2, col_axis=1, tn=tn),
        ],
        out_specs=pl.BlockSpec((tm, tn), lambda i, j: (i, j)),
        compiler_params=_cparams(2),
        name="out_proj",
    )(y, w, x, mod_r)


def _shift_in_tile(ext_scr, cur, ts):
    s = pl.program_id(1)

    @pl.when(s == 0)
    def _():
        ext_scr[0:SUBLANES, :] = jnp.zeros((SUBLANES, ext_scr.shape[1]), F32)

    @pl.when(s > 0)
    def _():
        ext_scr[0:SUBLANES, :] = ext_scr[ts:ts + SUBLANES, :]

    ext_scr[SUBLANES:ts + SUBLANES, :] = cur


def _causal_conv_from_ext(ext_scr, cw_ref, ts):
    kw = cw_ref.shape[0]
    acc = None
    for k in range(kw):
        term = cw_ref[k:k + 1, :] * ext_scr[pl.ds(SUBLANES - (kw - 1) + k, ts), :]
        acc = term if acc is None else acc + term
    return acc


def _rg_mid_kernel(gate_ref, xb_ref, cw_ref, cb_ref, wri_ref, br_ref, bi_ref, lam_ref, y_ref,
                   ext_scr, a_scr, b_scr, hs_scr, h_scr):
    ts, width = xb_ref.shape
    n_blocks, blk, _ = wri_ref.shape

    @pl.when(pl.program_id(1) == 0)
    def _():
        h_scr[...] = jnp.zeros_like(h_scr)

    _shift_in_tile(ext_scr, xb_ref[...], ts)
    xc = _causal_conv_from_ext(ext_scr, cw_ref, ts) + cb_ref[...]

    for g in range(n_blocks):
        cols = slice(g * blk, (g + 1) * blk)
        xg = xc[:, cols]
        ri = jnp.dot(xg.astype(BF16), wri_ref[g], preferred_element_type=F32)
        r = jax.nn.sigmoid(ri[:, :blk] + br_ref[:, cols])
        ig = jax.nn.sigmoid(ri[:, blk:] + bi_ref[:, cols])
        log_a = (RG_C * r) * _log_sigmoid(lam_ref[:, cols])
        a_scr[:, cols] = jnp.exp(log_a)
        b_scr[:, cols] = jnp.sqrt(1.0 - jnp.exp(2.0 * log_a)) * (ig * xg)

    row = lax.broadcasted_iota(jnp.int32, (SUBLANES, width), 0)

    def body(i, h):
        r0 = pl.multiple_of(i * SUBLANES, SUBLANES)
        a = a_scr[pl.ds(r0, SUBLANES), :]
        b = b_scr[pl.ds(r0, SUBLANES), :]
        for sh in (1, 2, 4):
            keep = row >= sh
            a_prev = jnp.where(keep, pltpu.roll(a, sh, 0), 1.0)
            b_prev = jnp.where(keep, pltpu.roll(b, sh, 0), 0.0)
            b = a * b_prev + b
            a = a * a_prev
        hs = a * h + b
        hs_scr[pl.ds(r0, SUBLANES), :] = hs
        return jnp.broadcast_to(hs[SUBLANES - 1:SUBLANES, :], (SUBLANES, width))

    h_scr[...] = lax.fori_loop(0, ts // SUBLANES, body, h_scr[...])
    y_ref[...] = (hs_scr[...] * jax.nn.gelu(gate_ref[...].astype(F32))).astype(BF16)


def _rg_mid_call(p, conv_w, conv_b, w_ri, b_r, b_i, lam, batch, ts=256):
    m, two_w = p.shape
    width = two_w // 2
    seq = m // batch
    ts = min(ts, seq)
    nst = seq // ts
    n_blocks, blk, _ = w_ri.shape
    full = lambda shape: pl.BlockSpec(shape, lambda b, s: (0,) * len(shape))
    return pl.pallas_call(
        _rg_mid_kernel,
        out_shape=jax.ShapeDtypeStruct((m, width), BF16),
        grid=(batch, nst),
        in_specs=[
            pl.BlockSpec((ts, width), lambda b, s: (b * nst + s, 0)),
            pl.BlockSpec((ts, width), lambda b, s: (b * nst + s, 1)),
            full(conv_w.shape),
            full((1, width)),
            full((n_blocks, blk, 2 * blk)),
            full((1, width)),
            full((1, width)),
            full((1, width)),
        ],
        out_specs=pl.BlockSpec((ts, width), lambda b, s: (b * nst + s, 0)),
        scratch_shapes=[
            pltpu.VMEM((ts + SUBLANES, width), F32),
            pltpu.VMEM((ts, width), F32),
            pltpu.VMEM((ts, width), F32),
            pltpu.VMEM((ts, width), F32),
            pltpu.VMEM((SUBLANES, width), F32),
        ],
        compiler_params=_cparams(2),
        name="rglru_core",
    )(p, p, conv_w, conv_b.reshape(1, width), w_ri, b_r.reshape(1, width), b_i.reshape(1, width),
      lam.reshape(1, width))


def _sc_mid_kernel(bg_ref, cg_ref, xv_ref, cw_ref, y_ref, ext_scr):
    ts = bg_ref.shape[0]
    _shift_in_tile(ext_scr, cg_ref[...].astype(F32) * xv_ref[...].astype(F32), ts)
    conv = _causal_conv_from_ext(ext_scr, cw_ref, ts)
    y_ref[...] = (bg_ref[...].astype(F32) * conv).astype(BF16)


def _sc_mid_call(p, conv_w, batch, ts=256):
    m, three_d = p.shape
    d = three_d // 3
    seq = m // batch
    ts = min(ts, seq)
    nst = seq // ts
    return pl.pallas_call(
        _sc_mid_kernel,
        out_shape=jax.ShapeDtypeStruct((m, d), BF16),
        grid=(batch, nst),
        in_specs=[
            pl.BlockSpec((ts, d), lambda b, s: (b * nst + s, 0)),
            pl.BlockSpec((ts, d), lambda b, s: (b * nst + s, 1)),
            pl.BlockSpec((ts, d), lambda b, s: (b * nst + s, 2)),
            pl.BlockSpec(conv_w.shape, lambda b, s: (0, 0)),
        ],
        out_specs=pl.BlockSpec((ts, d), lambda b, s: (b * nst + s, 0)),
        scratch_shapes=[pltpu.VMEM((ts + SUBLANES, d), F32)],
        compiler_params=_cparams(2),
        name="shortconv_core",
    )(p, p, p, conv_w)


def _fox_prep_kernel(q_ref, k_ref, fl_ref, bf_ref, qg_ref, kg_ref, qn_ref, kn_ref, fc_ref, ft_ref, carry_scr):
    ts, d = q_ref.shape

    @pl.when(pl.program_id(1) == 0)
    def _():
        carry_scr[...] = jnp.zeros_like(carry_scr)

    for h in range(d // HEAD_DIM):
        cols = slice(h * HEAD_DIM, (h + 1) * HEAD_DIM)
        for src, gain, dst in ((q_ref, qg_ref, qn_ref), (k_ref, kg_ref, kn_ref)):
            t = src[:, cols].astype(F32)
            ms = jnp.mean(t * t, axis=-1, keepdims=True)
            dst[:, cols] = ((t * lax.rsqrt(ms + EPS)) * gain[...]).astype(BF16)

    logf = _log_sigmoid(fl_ref[...] + bf_ref[...])
    r = lax.broadcasted_iota(jnp.int32, (ts, ts), 0)
    c = lax.broadcasted_iota(jnp.int32, (ts, ts), 1)
    tril = jnp.where(c <= r, 1.0, 0.0).astype(BF16)
    cum = _tril_dot(tril, logf) + carry_scr[0:1, :]
    carry_scr[...] = jnp.broadcast_to(cum[ts - 1:ts, :], carry_scr.shape)
    fc_ref[...] = cum
    ft_ref[...] = cum.T


def _fox_prep_call(p, fl, b_f_pad, q_gain, k_gain, batch, ts=256):
    m = p.shape[0]
    d = p.shape[1] // 4
    seq = m // batch
    ts = min(ts, seq)
    nst = seq // ts
    lanes = fl.shape[1]
    return pl.pallas_call(
        _fox_prep_kernel,
        out_shape=(
            jax.ShapeDtypeStruct((m, d), BF16),
            jax.ShapeDtypeStruct((m, d), BF16),
            jax.ShapeDtypeStruct((m, lanes), F32),
            jax.ShapeDtypeStruct((batch, lanes, seq), F32),
        ),
        grid=(batch, nst),
        in_specs=[
            pl.BlockSpec((ts, d), lambda b, s: (b * nst + s, 0)),
            pl.BlockSpec((ts, d), lambda b, s: (b * nst + s, 1)),
            pl.BlockSpec((ts, lanes), lambda b, s: (b * nst + s, 0)),
            pl.BlockSpec((1, lanes), lambda b, s: (0, 0)),
            pl.BlockSpec((1, HEAD_DIM), lambda b, s: (0, 0)),
            pl.BlockSpec((1, HEAD_DIM), lambda b, s: (0, 0)),
        ],
        out_specs=(
            pl.BlockSpec((ts, d), lambda b, s: (b * nst + s, 0)),
            pl.BlockSpec((ts, d), lambda b, s: (b * nst + s, 0)),
            pl.BlockSpec((ts, lanes), lambda b, s: (b * nst + s, 0)),
            pl.BlockSpec((None, lanes, ts), lambda b, s: (b, 0, s)),
        ),
        scratch_shapes=[pltpu.VMEM((SUBLANES, lanes), F32)],
        compiler_params=_cparams(2),
        name="fox_prep",
    )(p, p, fl, b_f_pad, q_gain.reshape(1, HEAD_DIM), k_gain.reshape(1, HEAD_DIM))


def _fox_flash_kernel(q_ref, k_ref, v_ref, g_ref, fc_ref, ft_ref, o_ref):
    tq, dh = q_ref.shape
    tk = ft_ref.shape[2]
    h = pl.program_id(1)
    qi = pl.program_id(2)
    scale = 1.0 / math.sqrt(dh)

    q = q_ref[...]
    lane = lax.broadcasted_iota(jnp.int32, fc_ref.shape, 1)
    fq = jnp.sum(jnp.where(lane == h, fc_ref[...], 0.0), axis=-1, keepdims=True)
    row = lax.broadcasted_iota(jnp.int32, (tq, tk), 0) + qi * tq
    col = lax.broadcasted_iota(jnp.int32, (tq, tk), 1)

    def body(j, carry):
        m_prev, l_prev, acc = carry
        k0 = pl.multiple_of(j * tk, tk)
        kj = k_ref[pl.ds(k0, tk), :]
        vj = v_ref[pl.ds(k0, tk), :].astype(BF16)
        s = lax.dot_general(q, kj, (((1,), (1,)), ((), ())), preferred_element_type=F32) * scale
        s = s + fq - ft_ref[j]
        s = jnp.where(col + j * tk <= row, s, -jnp.inf)
        m_new = jnp.maximum(m_prev, jnp.max(s, axis=-1, keepdims=True))
        alpha = jnp.exp(m_prev - m_new)
        p = jnp.exp(s - m_new)
        l_new = alpha * l_prev + jnp.sum(p, axis=-1, keepdims=True)
        acc = alpha * acc + jnp.dot(p.astype(BF16), vj, preferred_element_type=F32)
        return m_new, l_new, acc

    n_kv = (qi * tq + tq + tk - 1) // tk
    init = (jnp.full((tq, 1), -jnp.inf, F32), jnp.zeros((tq, 1), F32), jnp.zeros((tq, dh), F32))
    _, l_fin, acc = lax.fori_loop(0, n_kv, body, init)
    o = acc / l_fin
    o_ref[...] = (o * jax.nn.sigmoid(g_ref[...].astype(F32))).astype(BF16)


def _fox_flash_call(qn, kn, p, fc, ft4, batch, tq=512):
    m, d = qn.shape
    n_heads = d // HEAD_DIM
    seq = m // batch
    tq = min(tq, seq)
    nq = seq // tq
    nk, tk = ft4.shape[2], ft4.shape[4]
    lanes = fc.shape[1]
    return pl.pallas_call(
        _fox_flash_kernel,
        out_shape=jax.ShapeDtypeStruct((m, d), BF16),
        grid=(batch, n_heads, nq),
        in_specs=[
            pl.BlockSpec((tq, HEAD_DIM), lambda b, h, i: (b * nq + i, h)),
            pl.BlockSpec((seq, HEAD_DIM), lambda b, h, i: (b, h)),
            pl.BlockSpec((seq, HEAD_DIM), lambda b, h, i: (b, 2 * n_heads + h)),
            pl.BlockSpec((tq, HEAD_DIM), lambda b, h, i: (b * nq + i, 3 * n_heads + h)),
            pl.BlockSpec((tq, lanes), lambda b, h, i: (b * nq + i, 0)),
            pl.BlockSpec((None, None, nk, 1, tk), lambda b, h, i: (b, h, 0, 0, 0)),
        ],
        out_specs=pl.BlockSpec((tq, HEAD_DIM), lambda b, h, i: (b * nq + i, h)),
        compiler_params=_cparams(3),
        name="fox_flash",
    )(qn, kn, p, p, fc, ft4)


def _hg_mid_kernel(q_ref, fz_ref, iv_ref, g_ref, lbl_ref, ng_ref, y_ref,
                   qin_scr, kin_scr, kdec_scr, dec_scr, st_scr, *, layer):
    ts, d = q_ref.shape
    n_heads = d // HEAD_DIM
    n_chunks = ts // CHUNK

    @pl.when(pl.program_id(1) == 0)
    def _():
        st_scr[...] = jnp.zeros_like(st_scr)

    lg = lbl_ref[...]
    depth = lg.shape[0]
    mx = lg[0:1, :]
    for i in range(1, depth):
        mx = jnp.maximum(mx, lg[i:i + 1, :])
    es = [jnp.exp(lg[i:i + 1, :] - mx) for i in range(depth)]
    tot = es[0]
    for i in range(1, depth):
        tot = tot + es[i]
    lb = jnp.zeros_like(tot)
    for i in range(layer):
        lb = lb + es[i] / tot

    sig = jax.nn.sigmoid(fz_ref[...].astype(F32))
    logf = jnp.log(lb + (1.0 - lb) * sig)
    kk = (1.0 - lb) * (1.0 - sig)

    r = lax.broadcasted_iota(jnp.int32, (ts, ts), 0)
    c = lax.broadcasted_iota(jnp.int32, (ts, ts), 1)
    tril = jnp.where((c <= r) & (jnp.right_shift(c, CHUNK_SHIFT) == jnp.right_shift(r, CHUNK_SHIFT)), 1.0, 0.0).astype(BF16)
    bcum = _tril_dot(tril, logf)

    qin_scr[...] = (q_ref[...].astype(F32) * jnp.exp(bcum)).astype(BF16)
    kin_scr[...] = (kk * jnp.exp(-bcum)).astype(BF16)
    for ci in range(n_chunks):
        rows = slice(ci * CHUNK, (ci + 1) * CHUNK)
        b_last = bcum[(ci + 1) * CHUNK - 1:(ci + 1) * CHUNK, :]
        kdec_scr[rows, :] = (kk[rows, :] * jnp.exp(b_last - bcum[rows, :])).astype(BF16)
        dec_scr[ci] = jnp.broadcast_to(jnp.exp(b_last), (SUBLANES, d))

    rr = lax.broadcasted_iota(jnp.int32, (CHUNK, CHUNK), 0)
    cc = lax.broadcasted_iota(jnp.int32, (CHUNK, CHUNK), 1)
    causal = cc <= rr
    nt = (((1,), (1,)), ((), ()))
    tn = (((0,), (0,)), ((), ()))

    def chunk_body(ci, carry):
        r0 = pl.multiple_of(ci * CHUNK, CHUNK)
        rows = pl.ds(r0, CHUNK)
        for h in range(n_heads):
            cols = slice(h * HEAD_DIM, (h + 1) * HEAD_DIM)
            qh = qin_scr[rows, cols]
            kh = kin_scr[rows, cols]
            kd = kdec_scr[rows, cols]
            vh = iv_ref[rows, cols].astype(BF16)
            st = st_scr[h]
            sc = lax.dot_general(qh, kh, nt, preferred_element_type=F32)
            sc = jnp.where(causal, sc, 0.0).astype(BF16)
            o = jnp.dot(sc, vh, preferred_element_type=F32)
            o = o + lax.dot_general(qh, st.astype(BF16), nt, preferred_element_type=F32)
            u_t = lax.dot_general(vh, kd, tn, preferred_element_type=F32)
            st_scr[h] = st * dec_scr[ci][0:1, cols] + u_t
            ms = jnp.mean(o * o, axis=-1, keepdims=True)
            on = (o * lax.rsqrt(ms + EPS)) * ng_ref[...]
            gh = g_ref[rows, cols].astype(F32)
            y_ref[rows, cols] = (on * (gh * jax.nn.sigmoid(gh))).astype(BF16)
        return carry

    lax.fori_loop(0, n_chunks, chunk_body, 0)


def _hg_mid_call(p, lb_logits, norm_gain, layer, batch, ts=256):
    m, four_d = p.shape
    d = four_d // 4
    n_heads = d // HEAD_DIM
    seq = m // batch
    ts = min(ts, seq)
    nst = seq // ts
    return pl.pallas_call(
        functools.partial(_hg_mid_kernel, layer=layer),
        out_shape=jax.ShapeDtypeStruct((m, d), BF16),
        grid=(batch, nst),
        in_specs=[
            pl.BlockSpec((ts, d), lambda b, s: (b * nst + s, 0)),
            pl.BlockSpec((ts, d), lambda b, s: (b * nst + s, 1)),
            pl.BlockSpec((ts, d), lambda b, s: (b * nst + s, 2)),
            pl.BlockSpec((ts, d), lambda b, s: (b * nst + s, 3)),
            pl.BlockSpec(lb_logits.shape, lambda b, s: (0, 0)),
            pl.BlockSpec((1, HEAD_DIM), lambda b, s: (0, 0)),
        ],
        out_specs=pl.BlockSpec((ts, d), lambda b, s: (b * nst + s, 0)),
        scratch_shapes=[
            pltpu.VMEM((ts, d), BF16),
            pltpu.VMEM((ts, d), BF16),
            pltpu.VMEM((ts, d), BF16),
            pltpu.VMEM((ts // CHUNK, SUBLANES, d), F32),
            pltpu.VMEM((n_heads, HEAD_DIM, HEAD_DIM), F32),
        ],
        compiler_params=_cparams(2),
        name="hgrn2_core",
    )(p, p, p, p, lb_logits, norm_gain.reshape(1, HEAD_DIM))


def kernel(x, c, ada_w, ada_b, norm_g, ffn_w_gate, ffn_w_up, ffn_w_down, rg_w_in, rg_conv_w, rg_conv_b, rg_w_r, rg_b_r, rg_w_i, rg_b_i, rg_lam, rg_w_out, sc_w_in, sc_conv_w, sc_w_out, fox_w_in, fox_b_f, fox_q_gain, fox_k_gain, fox_w_out, hg_w_in, hg_lb_logits, hg_norm_gain, hg_w_out):
    batch, seq, d = x.shape
    depth = ada_w.shape[0]
    n_heads = d // HEAD_DIM
    n_mixers = 4
    mid_dtype = F32

    mod = _ada_call(c, ada_w, ada_b)
    mod_r = mod.reshape(depth * batch * 9, 1, d)
    ng_r = norm_g.reshape(depth * 3, 1, d)

    wg = ffn_w_gate.astype(BF16)
    wu = ffn_w_up.astype(BF16)
    wd = ffn_w_down.astype(BF16)

    xf = x.reshape(batch * seq, d)
    for i in range(depth):
        xf = _ffn_call(xf, mod_r, ng_r, wg, wu, wd, i, 0, 0, batch)
        m, j = i % n_mixers, i // n_mixers
        if m == 0:
            p = _mm_mod_call(xf, mod_r, ng_r, rg_w_in.astype(BF16), j, rg_w_in.shape[2], i, batch, mid_dtype)
            w_ri = jnp.concatenate([rg_w_r[j], rg_w_i[j]], axis=-1).astype(BF16)
            y = _rg_mid_call(p, rg_conv_w[j], rg_conv_b[j], w_ri, rg_b_r[j], rg_b_i[j], rg_lam[j], batch)
            xf = _mm_res_call(y, rg_w_out.astype(BF16), j, xf, mod_r, i, batch)
        elif m == 1:
            p = _mm_mod_call(xf, mod_r, ng_r, sc_w_in.astype(BF16), j, sc_w_in.shape[2], i, batch, mid_dtype)
            y = _sc_mid_call(p, sc_conv_w[j], batch)
            xf = _mm_res_call(y, sc_w_out.astype(BF16), j, xf, mod_r, i, batch)
        elif m == 2:
            w_in = fox_w_in.astype(BF16)
            p = _mm_mod_call(xf, mod_r, ng_r, w_in, j, 4 * d, i, batch, mid_dtype)
            w_fl = jnp.pad(w_in[j][:, 4 * d:], ((0, 0), (0, HEAD_DIM - n_heads)))[None]
            fl = _mm_mod_call(xf, mod_r, ng_r, w_fl, 0, HEAD_DIM, i, batch, F32)
            b_f_pad = jnp.pad(fox_b_f[j], (0, HEAD_DIM - n_heads)).reshape(1, HEAD_DIM)
            qn, kn, fc, ft = _fox_prep_call(p, fl, b_f_pad, fox_q_gain[j], fox_k_gain[j], batch)
            tk = min(512, seq)
            ft4 = ft[:, :n_heads, :].reshape(batch, n_heads, seq // tk, 1, tk)
            y = _fox_flash_call(qn, kn, p, fc, ft4, batch)
            xf = _mm_res_call(y, fox_w_out.astype(BF16), j, xf, mod_r, i, batch)
        else:
            p = _mm_mod_call(xf, mod_r, ng_r, hg_w_in.astype(BF16), j, hg_w_in.shape[2], i, batch, mid_dtype)
            y = _hg_mid_call(p, hg_lb_logits, hg_norm_gain[j], i, batch)
            xf = _mm_res_call(y, hg_w_out.astype(BF16), j, xf, mod_r, i, batch)
        xf = _ffn_call(xf, mod_r, ng_r, wg, wu, wd, i, 1, 2, batch)
    return xf.reshape(batch, seq, d)
```

```python
import functools
import math

import jax
import jax.numpy as jnp
from jax import lax
from jax.experimental import pallas as pl
from jax.experimental.pallas import tpu as pltpu

F32 = jnp.float32
BF16 = jnp.bfloat16

EPS = 1e-6
HEAD_DIM = 128
CHUNK = 64
CHUNK_SHIFT = 6
RG_C = 8.0
SUBLANES = 8
VMEM_LIMIT = 56 * 1024 * 1024


def _cparams(n_axes):
    return pltpu.CompilerParams(dimension_semantics=("arbitrary",) * n_axes,
                                vmem_limit_bytes=VMEM_LIMIT)


def _tile(n, preferred):
    if n <= preferred:
        return n
    t = preferred - preferred % HEAD_DIM
    while n % t:
        t -= HEAD_DIM
    return t


def _modulate(x, g, shift, scale):
    ms = jnp.mean(x * x, axis=-1, keepdims=True)
    return (x * lax.rsqrt(ms + EPS)) * (g * (1.0 + scale)) + shift


def _log_sigmoid(x):
    return jnp.minimum(x, 0.0) - jnp.log1p(jnp.exp(-jnp.abs(x)))


def _split3(x):
    hi = x.astype(BF16)
    r1 = x - hi.astype(F32)
    mid = r1.astype(BF16)
    lo = (r1 - mid.astype(F32)).astype(BF16)
    return hi, mid, lo


def _tril_dot(tril, x):
    hi, mid, lo = _split3(x)
    acc = jnp.dot(tril, hi, preferred_element_type=F32)
    acc = acc + jnp.dot(tril, mid, preferred_element_type=F32)
    return acc + jnp.dot(tril, lo, preferred_element_type=F32)


def _ada_kernel(c_ref, w_ref, b_ref, o_ref):
    c = c_ref[...]
    sc = (c * jax.nn.sigmoid(c)).astype(BF16)
    o_ref[...] = jnp.dot(sc, w_ref[...].astype(BF16), preferred_element_type=F32) + b_ref[...]


def _ada_call(c, ada_w, ada_b, tn=1024):
    depth, d, n = ada_w.shape
    b = c.shape[0]
    rows = -(-b // SUBLANES) * SUBLANES
    c_pad = jnp.pad(c, ((0, rows - b), (0, 0)))
    tn = _tile(n, tn)
    out = pl.pallas_call(
        _ada_kernel,
        out_shape=jax.ShapeDtypeStruct((depth, rows, n), F32),
        grid=(depth, n // tn),
        in_specs=[
            pl.BlockSpec((rows, d), lambda l, j: (0, 0)),
            pl.BlockSpec((None, d, tn), lambda l, j: (l, 0, j)),
            pl.BlockSpec((None, 1, tn), lambda l, j: (l, 0, j)),
        ],
        out_specs=pl.BlockSpec((None, rows, tn), lambda l, j: (l, 0, j)),
        compiler_params=_cparams(2),
        name="ada",
    )(c_pad, ada_w, ada_b.reshape(depth, 1, n))
    return out[:, :b]


def _row_chunks(tm, rc):
    return [slice(r, r + rc) for r in range(0, tm, rc)]


def _ffn_kernel(x_ref, shift_ref, scale_ref, gate_ref, g_ref, wg_ref, wu_ref, wd_ref, o_ref, h_scr, *, rc):
    f = pl.program_id(1)
    chunks = _row_chunks(x_ref.shape[0], rc)

    def weights():
        return wg_ref[...].astype(BF16), wu_ref[...].astype(BF16), wd_ref[...].astype(BF16)

    def swiglu(h, wg, wu, wd):
        gg = jnp.dot(h, wg, preferred_element_type=F32)
        uu = jnp.dot(h, wu, preferred_element_type=F32)
        a = ((gg * jax.nn.sigmoid(gg)) * uu).astype(BF16)
        return jnp.dot(a, wd, preferred_element_type=F32)

    @pl.when(f == 0)
    def _():
        w = weights()
        for rows in chunks:
            h = _modulate(x_ref[rows, :], g_ref[...], shift_ref[...], scale_ref[...]).astype(BF16)
            h_scr[rows, :] = h
            o_ref[rows, :] = swiglu(h, *w)

    @pl.when(f > 0)
    def _():
        w = weights()
        for rows in chunks:
            o_ref[rows, :] += swiglu(h_scr[rows, :], *w)

    @pl.when(f == pl.num_programs(1) - 1)
    def _():
        o_ref[...] = x_ref[...] + (0.5 * gate_ref[...]) * o_ref[...]


def _mod_spec(d, base, tiles_per_batch, n_grid_axes, col_axis=None, tn=None):
    width = d if tn is None else tn

    def index(*ids):
        col = 0 if col_axis is None else ids[col_axis]
        return (base + 9 * (ids[0] // tiles_per_batch), 0, col)

    del n_grid_axes
    return pl.BlockSpec((None, 1, width), index)


def _ffn_call(x, mod_r, ng_r, wg, wu, wd, layer, which, sub, batch, tm=1024, tf=256, rc=512):
    m, d = x.shape
    f_dim = wg.shape[-1]
    tm = min(tm, m // batch)
    rc = min(rc, tm)
    tf = _tile(f_dim, tf)
    tpb = (m // batch) // tm
    base = layer * batch * 9 + sub * 3
    return pl.pallas_call(
        functools.partial(_ffn_kernel, rc=rc),
        out_shape=jax.ShapeDtypeStruct((m, d), F32),
        grid=(m // tm, f_dim // tf),
        in_specs=[
            pl.BlockSpec((tm, d), lambda i, f: (i, 0)),
            _mod_spec(d, base + 0, tpb, 2),
            _mod_spec(d, base + 1, tpb, 2),
            _mod_spec(d, base + 2, tpb, 2),
            pl.BlockSpec((None, 1, d), lambda i, f: (layer * 3 + sub, 0, 0)),
            pl.BlockSpec((None, None, d, tf), lambda i, f: (layer, which, 0, f)),
            pl.BlockSpec((None, None, d, tf), lambda i, f: (layer, which, 0, f)),
            pl.BlockSpec((None, None, tf, d), lambda i, f: (layer, which, f, 0)),
        ],
        out_specs=pl.BlockSpec((tm, d), lambda i, f: (i, 0)),
        scratch_shapes=[pltpu.VMEM((tm, d), BF16)],
        compiler_params=_cparams(2),
        name="ffn",
    )(x, mod_r, mod_r, mod_r, ng_r, wg, wu, wd)


def _mm_mod_kernel(x_ref, shift_ref, scale_ref, g_ref, w_ref, o_ref, h_scr, *, rc):
    j = pl.program_id(1)
    chunks = _row_chunks(x_ref.shape[0], rc)

    @pl.when(j == 0)
    def _():
        w = w_ref[...].astype(BF16)
        for rows in chunks:
            h = _modulate(x_ref[rows, :], g_ref[...], shift_ref[...], scale_ref[...]).astype(BF16)
            h_scr[rows, :] = h
            o_ref[rows, :] = jnp.dot(h, w, preferred_element_type=F32).astype(o_ref.dtype)

    @pl.when(j > 0)
    def _():
        w = w_ref[...].astype(BF16)
        for rows in chunks:
            o_ref[rows, :] = jnp.dot(h_scr[rows, :], w, preferred_element_type=F32).astype(o_ref.dtype)


def _mm_mod_call(x, mod_r, ng_r, w, widx, n_out, layer, batch, out_dtype, tm=1024, tn=512, rc=512):
    m, d = x.shape
    tm = min(tm, m // batch)
    rc = min(rc, tm)
    tn = _tile(n_out, tn)
    tpb = (m // batch) // tm
    base = layer * batch * 9 + 3
    return pl.pallas_call(
        functools.partial(_mm_mod_kernel, rc=rc),
        out_shape=jax.ShapeDtypeStruct((m, n_out), out_dtype),
        grid=(m // tm, n_out // tn),
        in_specs=[
            pl.BlockSpec((tm, d), lambda i, j: (i, 0)),
            _mod_spec(d, base + 0, tpb, 2),
            _mod_spec(d, base + 1, tpb, 2),
            pl.BlockSpec((None, 1, d), lambda i, j: (layer * 3 + 1, 0, 0)),
            pl.BlockSpec((None, d, tn), lambda i, j: (widx, 0, j)),
        ],
        out_specs=pl.BlockSpec((tm, tn), lambda i, j: (i, j)),
        scratch_shapes=[pltpu.VMEM((tm, d), BF16)],
        compiler_params=_cparams(2),
        name="in_proj",
    )(x, mod_r, mod_r, ng_r, w)


def _mm_res_kernel(y_ref, w_ref, x_ref, gate_ref, o_ref):
    acc = jnp.dot(y_ref[...], w_ref[...], preferred_element_type=F32)
    o_ref[...] = x_ref[...] + gate_ref[...] * acc


def _mm_res_call(y, w, widx, x, mod_r, layer, batch, tm=512, tn=1024):
    m, k = y.shape
    d = x.shape[1]
    tm = min(tm, m // batch)
    tn = _tile(d, tn)
    tpb = (m // batch) // tm
    base = layer * batch * 9 + 3 + 2
    return pl.pallas_call(
        _mm_res_kernel,
        out_shape=jax.ShapeDtypeStruct((m, d), F32),
        grid=(m // tm, d // tn),
        in_specs=[
            pl.BlockSpec((tm, k), lambda i, j: (i, 0)),
            pl.BlockSpec((None, k, tn), lambda i, j: (widx, 0, j)),
            pl.BlockSpec((tm, tn), lambda i, j: (i, j)),
            _mod_spec(d, base, tpb, 2, col_axis=1, tn=tn),
        ],
        out_specs=pl.BlockSpec((tm, tn), lambda i, j: (i, j)),
        compiler_params=_cparams(2),
        name="out_proj",
    )(y, w, x, mod_r)


def _shift_in_tile(ext_scr, cur, ts):
    s = pl.program_id(1)

    @pl.when(s == 0)
    def _():
        ext_scr[0:SUBLANES, :] = jnp.zeros((SUBLANES, ext_scr.shape[1]), F32)

    @pl.when(s > 0)
    def _():
        ext_scr[0:SUBLANES, :] = ext_scr[ts:ts + SUBLANES, :]

    ext_scr[SUBLANES:ts + SUBLANES, :] = cur


def _causal_conv_from_ext(ext_scr, cw_ref, ts):
    kw = cw_ref.shape[0]
    acc = None
    for k in range(kw):
        term = cw_ref[k:k + 1, :] * ext_scr[pl.ds(SUBLANES - (kw - 1) + k, ts), :]
        acc = term if acc is None else acc + term
    return acc


def _rg_mid_kernel(gate_ref, xb_ref, cw_ref, cb_ref, wri_ref, br_ref, bi_ref, lam_ref, y_ref,
                   ext_scr, a_scr, b_scr, hs_scr, h_scr):
    ts, width = xb_ref.shape
    n_blocks, blk, _ = wri_ref.shape

    @pl.when(pl.program_id(1) == 0)
    def _():
        h_scr[...] = jnp.zeros_like(h_scr)

    _shift_in_tile(ext_scr, xb_ref[...].astype(F32), ts)
    xc = _causal_conv_from_ext(ext_scr, cw_ref, ts) + cb_ref[...]

    for g in range(n_blocks):
        cols = slice(g * blk, (g + 1) * blk)
        xg = xc[:, cols]
        ri = jnp.dot(xg.astype(BF16), wri_ref[g], preferred_element_type=F32)
        r = jax.nn.sigmoid(ri[:, :blk] + br_ref[:, cols])
        ig = jax.nn.sigmoid(ri[:, blk:] + bi_ref[:, cols])
        log_a = (RG_C * r) * _log_sigmoid(lam_ref[:, cols])
        a_scr[:, cols] = jnp.exp(log_a)
        b_scr[:, cols] = jnp.sqrt(1.0 - jnp.exp(2.0 * log_a)) * (ig * xg)

    row = lax.broadcasted_iota(jnp.int32, (SUBLANES, width), 0)

    def body(i, h):
        r0 = pl.multiple_of(i * SUBLANES, SUBLANES)
        a = a_scr[pl.ds(r0, SUBLANES), :]
        b = b_scr[pl.ds(r0, SUBLANES), :]
        for sh in (1, 2, 4):
            keep = row >= sh
            a_prev = jnp.where(keep, pltpu.roll(a, sh, 0), 1.0)
            b_prev = jnp.where(keep, pltpu.roll(b, sh, 0), 0.0)
            b = a * b_prev + b
            a = a * a_prev
        hs = a * h + b
        hs_scr[pl.ds(r0, SUBLANES), :] = hs
        return jnp.broadcast_to(hs[SUBLANES - 1:SUBLANES, :], (SUBLANES, width))

    h_scr[...] = lax.fori_loop(0, ts // SUBLANES, body, h_scr[...])
    y_ref[...] = (hs_scr[...] * jax.nn.gelu(gate_ref[...].astype(F32))).astype(BF16)


def _rg_mid_call(p, conv_w, conv_b, w_ri, b_r, b_i, lam, batch, ts=256):
    m, two_w = p.shape
    width = two_w // 2
    seq = m // batch
    ts = min(ts, seq)
    nst = seq // ts
    n_blocks, blk, _ = w_ri.shape
    full = lambda shape: pl.BlockSpec(shape, lambda b, s: (0,) * len(shape))
    return pl.pallas_call(
        _rg_mid_kernel,
        out_shape=jax.ShapeDtypeStruct((m, width), BF16),
        grid=(batch, nst),
        in_specs=[
            pl.BlockSpec((ts, width), lambda b, s: (b * nst + s, 0)),
            pl.BlockSpec((ts, width), lambda b, s: (b * nst + s, 1)),
            full(conv_w.shape),
            full((1, width)),
            full((n_blocks, blk, 2 * blk)),
            full((1, width)),
            full((1, width)),
            full((1, width)),
        ],
        out_specs=pl.BlockSpec((ts, width), lambda b, s: (b * nst + s, 0)),
        scratch_shapes=[
            pltpu.VMEM((ts + SUBLANES, width), F32),
            pltpu.VMEM((ts, width), F32),
            pltpu.VMEM((ts, width), F32),
            pltpu.VMEM((ts, width), F32),
            pltpu.VMEM((SUBLANES, width), F32),
        ],
        compiler_params=_cparams(2),
        name="rglru_core",
    )(p, p, conv_w, conv_b.reshape(1, width), w_ri, b_r.reshape(1, width), b_i.reshape(1, width),
      lam.reshape(1, width))


def _sc_mid_kernel(bg_ref, cg_ref, xv_ref, cw_ref, y_ref, ext_scr):
    ts = bg_ref.shape[0]
    _shift_in_tile(ext_scr, cg_ref[...].astype(F32) * xv_ref[...].astype(F32), ts)
    conv = _causal_conv_from_ext(ext_scr, cw_ref, ts)
    y_ref[...] = (bg_ref[...].astype(F32) * conv).astype(BF16)


def _sc_mid_call(p, conv_w, batch, ts=256):
    m, three_d = p.shape
    d = three_d // 3
    seq = m // batch
    ts = min(ts, seq)
    nst = seq // ts
    return pl.pallas_call(
        _sc_mid_kernel,
        out_shape=jax.ShapeDtypeStruct((m, d), BF16),
        grid=(batch, nst),
        in_specs=[
            pl.BlockSpec((ts, d), lambda b, s: (b * nst + s, 0)),
            pl.BlockSpec((ts, d), lambda b, s: (b * nst + s, 1)),
            pl.BlockSpec((ts, d), lambda b, s: (b * nst + s, 2)),
            pl.BlockSpec(conv_w.shape, lambda b, s: (0, 0)),
        ],
        out_specs=pl.BlockSpec((ts, d), lambda b, s: (b * nst + s, 0)),
        scratch_shapes=[pltpu.VMEM((ts + SUBLANES, d), F32)],
        compiler_params=_cparams(2),
        name="shortconv_core",
    )(p, p, p, conv_w)


def _fox_prep_kernel(q_ref, k_ref, fl_ref, bf_ref, qg_ref, kg_ref, qn_ref, kn_ref, fc_ref, ft_ref, carry_scr):
    ts, d = q_ref.shape

    @pl.when(pl.program_id(1) == 0)
    def _():
        carry_scr[...] = jnp.zeros_like(carry_scr)

    for h in range(d // HEAD_DIM):
        cols = slice(h * HEAD_DIM, (h + 1) * HEAD_DIM)
        for src, gain, dst in ((q_ref, qg_ref, qn_ref), (k_ref, kg_ref, kn_ref)):
            t = src[:, cols].astype(F32)
            ms = jnp.mean(t * t, axis=-1, keepdims=True)
            dst[:, cols] = ((t * lax.rsqrt(ms + EPS)) * gain[...]).astype(BF16)

    logf = _log_sigmoid(fl_ref[...] + bf_ref[...])
    r = lax.broadcasted_iota(jnp.int32, (ts, ts), 0)
    c = lax.broadcasted_iota(jnp.int32, (ts, ts), 1)
    tril = jnp.where(c <= r, 1.0, 0.0).astype(BF16)
    cum = _tril_dot(tril, logf) + carry_scr[0:1, :]
    carry_scr[...] = jnp.broadcast_to(cum[ts - 1:ts, :], carry_scr.shape)
    fc_ref[...] = cum
    ft_ref[...] = cum.T


def _fox_prep_call(p, fl, b_f_pad, q_gain, k_gain, batch, ts=256):
    m = p.shape[0]
    d = p.shape[1] // 4
    seq = m // batch
    ts = min(ts, seq)
    nst = seq // ts
    lanes = fl.shape[1]
    return pl.pallas_call(
        _fox_prep_kernel,
        out_shape=(
            jax.ShapeDtypeStruct((m, d), BF16),
            jax.ShapeDtypeStruct((m, d), BF16),
            jax.ShapeDtypeStruct((m, lanes), F32),
            jax.ShapeDtypeStruct((batch, lanes, seq), F32),
        ),
        grid=(batch, nst),
        in_specs=[
            pl.BlockSpec((ts, d), lambda b, s: (b * nst + s, 0)),
            pl.BlockSpec((ts, d), lambda b, s: (b * nst + s, 1)),
            pl.BlockSpec((ts, lanes), lambda b, s: (b * nst + s, 0)),
            pl.BlockSpec((1, lanes), lambda b, s: (0, 0)),
            pl.BlockSpec((1, HEAD_DIM), lambda b, s: (0, 0)),
            pl.BlockSpec((1, HEAD_DIM), lambda b, s: (0, 0)),
        ],
        out_specs=(
            pl.BlockSpec((ts, d), lambda b, s: (b * nst + s, 0)),
            pl.BlockSpec((ts, d), lambda b, s: (b * nst + s, 0)),
            pl.BlockSpec((ts, lanes), lambda b, s: (b * nst + s, 0)),
            pl.BlockSpec((None, lanes, ts), lambda b, s: (b, 0, s)),
        ),
        scratch_shapes=[pltpu.VMEM((SUBLANES, lanes), F32)],
        compiler_params=_cparams(2),
        name="fox_prep",
    )(p, p, fl, b_f_pad, q_gain.reshape(1, HEAD_DIM), k_gain.reshape(1, HEAD_DIM))


def _fox_flash_kernel(q_ref, k_ref, v_ref, g_ref, fc_ref, ft_ref, o_ref):
    tq, dh = q_ref.shape
    tk = ft_ref.shape[2]
    h = pl.program_id(1)
    qi = pl.program_id(2)
    scale = 1.0 / math.sqrt(dh)

    q = q_ref[...]
    lane = lax.broadcasted_iota(jnp.int32, fc_ref.shape, 1)
    fq = jnp.sum(jnp.where(lane == h, fc_ref[...], 0.0), axis=-1, keepdims=True)
    row = lax.broadcasted_iota(jnp.int32, (tq, tk), 0) + qi * tq
    col = lax.broadcasted_iota(jnp.int32, (tq, tk), 1)

    def body(j, carry):
        m_prev, l_prev, acc = carry
        k0 = pl.multiple_of(j * tk, tk)
        kj = k_ref[pl.ds(k0, tk), :]
        vj = v_ref[pl.ds(k0, tk), :].astype(BF16)
        s = lax.dot_general(q, kj, (((1,), (1,)), ((), ())), preferred_element_type=F32) * scale
        s = s + fq - ft_ref[j]
        s = jnp.where(col + j * tk <= row, s, -jnp.inf)
        m_new = jnp.maximum(m_prev, jnp.max(s, axis=-1, keepdims=True))
        alpha = jnp.exp(m_prev - m_new)
        p = jnp.exp(s - m_new)
        l_new = alpha * l_prev + jnp.sum(p, axis=-1, keepdims=True)
        acc = alpha * acc + jnp.dot(p.astype(BF16), vj, preferred_element_type=F32)
        return m_new, l_new, acc

    n_kv = (qi * tq + tq + tk - 1) // tk
    init = (jnp.full((tq, 1), -jnp.inf, F32), jnp.zeros((tq, 1), F32), jnp.zeros((tq, dh), F32))
    _, l_fin, acc = lax.fori_loop(0, n_kv, body, init)
    o = acc / l_fin
    o_ref[...] = (o * jax.nn.sigmoid(g_ref[...].astype(F32))).astype(BF16)


def _fox_flash_call(qn, kn, p, fc, ft4, batch, tq=512):
    m, d = qn.shape
    n_heads = d // HEAD_DIM
    seq = m // batch
    tq = min(tq, seq)
    nq = seq // tq
    nk, tk = ft4.shape[2], ft4.shape[4]
    lanes = fc.shape[1]
    return pl.pallas_call(
        _fox_flash_kernel,
        out_shape=jax.ShapeDtypeStruct((m, d), BF16),
        grid=(batch, n_heads, nq),
        in_specs=[
            pl.BlockSpec((tq, HEAD_DIM), lambda b, h, i: (b * nq + i, h)),
            pl.BlockSpec((seq, HEAD_DIM), lambda b, h, i: (b, h)),
            pl.BlockSpec((seq, HEAD_DIM), lambda b, h, i: (b, 2 * n_heads + h)),
            pl.BlockSpec((tq, HEAD_DIM), lambda b, h, i: (b * nq + i, 3 * n_heads + h)),
            pl.BlockSpec((tq, lanes), lambda b, h, i: (b * nq + i, 0)),
            pl.BlockSpec((None, None, nk, 1, tk), lambda b, h, i: (b, h, 0, 0, 0)),
        ],
        out_specs=pl.BlockSpec((tq, HEAD_DIM), lambda b, h, i: (b * nq + i, h)),
        compiler_params=_cparams(3),
        name="fox_flash",
    )(qn, kn, p, p, fc, ft4)


def _hg_mid_kernel(q_ref, fz_ref, iv_ref, g_ref, lbl_ref, ng_ref, y_ref,
                   qin_scr, kin_scr, kdec_scr, dec_scr, st_scr, *, layer):
    ts, d = q_ref.shape
    n_heads = d // HEAD_DIM
    n_chunks = ts // CHUNK

    @pl.when(pl.program_id(1) == 0)
    def _():
        st_scr[...] = jnp.zeros_like(st_scr)

    lg = lbl_ref[...]
    depth = lg.shape[0]
    mx = lg[0:1, :]
    for i in range(1, depth):
        mx = jnp.maximum(mx, lg[i:i + 1, :])
    es = [jnp.exp(lg[i:i + 1, :] - mx) for i in range(depth)]
    tot = es[0]
    for i in range(1, depth):
        tot = tot + es[i]
    lb = jnp.zeros_like(tot)
    for i in range(layer):
        lb = lb + es[i] / tot

    sig = jax.nn.sigmoid(fz_ref[...].astype(F32))
    logf = jnp.log(lb + (1.0 - lb) * sig)
    kk = (1.0 - lb) * (1.0 - sig)

    r = lax.broadcasted_iota(jnp.int32, (ts, ts), 0)
    c = lax.broadcasted_iota(jnp.int32, (ts, ts), 1)
    tril = jnp.where((c <= r) & (jnp.right_shift(c, CHUNK_SHIFT) == jnp.right_shift(r, CHUNK_SHIFT)), 1.0, 0.0).astype(BF16)
    bcum = _tril_dot(tril, logf)

    qin_scr[...] = (q_ref[...].astype(F32) * jnp.exp(bcum)).astype(BF16)
    kin_scr[...] = (kk * jnp.exp(-bcum)).astype(BF16)
    for ci in range(n_chunks):
        rows = slice(ci * CHUNK, (ci + 1) * CHUNK)
        b_last = bcum[(ci + 1) * CHUNK - 1:(ci + 1) * CHUNK, :]
        kdec_scr[rows, :] = (kk[rows, :] * jnp.exp(b_last - bcum[rows, :])).astype(BF16)
        dec_scr[ci] = jnp.broadcast_to(jnp.exp(b_last), (SUBLANES, d))

    rr = lax.broadcasted_iota(jnp.int32, (CHUNK, CHUNK), 0)
    cc = lax.broadcasted_iota(jnp.int32, (CHUNK, CHUNK), 1)
    causal = cc <= rr
    nt = (((1,), (1,)), ((), ()))
    tn = (((0,), (0,)), ((), ()))

    def chunk_body(ci, carry):
        r0 = pl.multiple_of(ci * CHUNK, CHUNK)
        rows = pl.ds(r0, CHUNK)
        for h in range(n_heads):
            cols = slice(h * HEAD_DIM, (h + 1) * HEAD_DIM)
            qh = qin_scr[rows, cols]
            kh = kin_scr[rows, cols]
            kd = kdec_scr[rows, cols]
            vh = iv_ref[rows, cols].astype(BF16)
            st = st_scr[h]
            sc = lax.dot_general(qh, kh, nt, preferred_element_type=F32)
            sc = jnp.where(causal, sc, 0.0).astype(BF16)
            o = jnp.dot(sc, vh, preferred_element_type=F32)
            o = o + lax.dot_general(qh, st.astype(BF16), nt, preferred_element_type=F32)
            u_t = lax.dot_general(vh, kd, tn, preferred_element_type=F32)
            st_scr[h] = st * dec_scr[ci][0:1, cols] + u_t
            ms = jnp.mean(o * o, axis=-1, keepdims=True)
            on = (o * lax.rsqrt(ms + EPS)) * ng_ref[...]
            gh = g_ref[rows, cols].astype(F32)
            y_ref[rows, cols] = (on * (gh * jax.nn.sigmoid(gh))).astype(BF16)
        return carry

    lax.fori_loop(0, n_chunks, chunk_body, 0)


def _hg_mid_call(p, lb_logits, norm_gain, layer, batch, ts=256):
    m, four_d = p.shape
    d = four_d // 4
    n_heads = d // HEAD_DIM
    seq = m // batch
    ts = min(ts, seq)
    nst = seq // ts
    return pl.pallas_call(
        functools.partial(_hg_mid_kernel, layer=layer),
        out_shape=jax.ShapeDtypeStruct((m, d), BF16),
        grid=(batch, nst),
        in_specs=[
            pl.BlockSpec((ts, d), lambda b, s: (b * nst + s, 0)),
            pl.BlockSpec((ts, d), lambda b, s: (b * nst + s, 1)),
            pl.BlockSpec((ts, d), lambda b, s: (b * nst + s, 2)),
            pl.BlockSpec((ts, d), lambda b, s: (b * nst + s, 3)),
            pl.BlockSpec(lb_logits.shape, lambda b, s: (0, 0)),
            pl.BlockSpec((1, HEAD_DIM), lambda b, s: (0, 0)),
        ],
        out_specs=pl.BlockSpec((ts, d), lambda b, s: (b * nst + s, 0)),
        scratch_shapes=[
            pltpu.VMEM((ts, d), BF16),
            pltpu.VMEM((ts, d), BF16),
            pltpu.VMEM((ts, d), BF16),
            pltpu.VMEM((ts // CHUNK, SUBLANES, d), F32),
            pltpu.VMEM((n_heads, HEAD_DIM, HEAD_DIM), F32),
        ],
        compiler_params=_cparams(2),
        name="hgrn2_core",
    )(p, p, p, p, lb_logits, norm_gain.reshape(1, HEAD_DIM))


def kernel(x, c, ada_w, ada_b, norm_g, ffn_w_gate, ffn_w_up, ffn_w_down, rg_w_in, rg_conv_w, rg_conv_b, rg_w_r, rg_b_r, rg_w_i, rg_b_i, rg_lam, rg_w_out, sc_w_in, sc_conv_w, sc_w_out, fox_w_in, fox_b_f, fox_q_gain, fox_k_gain, fox_w_out, hg_w_in, hg_lb_logits, hg_norm_gain, hg_w_out):
    batch, seq, d = x.shape
    depth = ada_w.shape[0]
    n_heads = d // HEAD_DIM
    n_mixers = 4
    mid_dtype = BF16

    mod = _ada_call(c, ada_w, ada_b)
    mod_r = mod.reshape(depth * batch * 9, 1, d)
    ng_r = norm_g.reshape(depth * 3, 1, d)

    wg, wu, wd = ffn_w_gate, ffn_w_up, ffn_w_down

    xf = x.reshape(batch * seq, d)
    for i in range(depth):
        xf = _ffn_call(xf, mod_r, ng_r, wg, wu, wd, i, 0, 0, batch)
        m, j = i % n_mixers, i // n_mixers
        if m == 0:
            p = _mm_mod_call(xf, mod_r, ng_r, rg_w_in, j, rg_w_in.shape[2], i, batch, mid_dtype)
            w_ri = jnp.concatenate([rg_w_r[j], rg_w_i[j]], axis=-1).astype(BF16)
            y = _rg_mid_call(p, rg_conv_w[j], rg_conv_b[j], w_ri, rg_b_r[j], rg_b_i[j], rg_lam[j], batch)
            xf = _mm_res_call(y, rg_w_out.astype(BF16), j, xf, mod_r, i, batch)
        elif m == 1:
            p = _mm_mod_call(xf, mod_r, ng_r, sc_w_in, j, sc_w_in.shape[2], i, batch, mid_dtype)
            y = _sc_mid_call(p, sc_conv_w[j], batch)
            xf = _mm_res_call(y, sc_w_out.astype(BF16), j, xf, mod_r, i, batch)
        elif m == 2:
            p = _mm_mod_call(xf, mod_r, ng_r, fox_w_in, j, 4 * d, i, batch, mid_dtype)
            w_fl = jnp.pad(fox_w_in[j][:, 4 * d:], ((0, 0), (0, HEAD_DIM - n_heads)))[None]
            fl = _mm_mod_call(xf, mod_r, ng_r, w_fl, 0, HEAD_DIM, i, batch, F32)
            b_f_pad = jnp.pad(fox_b_f[j], (0, HEAD_DIM - n_heads)).reshape(1, HEAD_DIM)
            qn, kn, fc, ft = _fox_prep_call(p, fl, b_f_pad, fox_q_gain[j], fox_k_gain[j], batch)
            tk = min(512, seq)
            ft4 = ft[:, :n_heads, :].reshape(batch, n_heads, seq // tk, 1, tk)
            y = _fox_flash_call(qn, kn, p, fc, ft4, batch)
            xf = _mm_res_call(y, fox_w_out.astype(BF16), j, xf, mod_r, i, batch)
        else:
            p = _mm_mod_call(xf, mod_r, ng_r, hg_w_in, j, hg_w_in.shape[2], i, batch, mid_dtype)
            y = _hg_mid_call(p, hg_lb_logits, hg_norm_gain[j], i, batch)
            xf = _mm_res_call(y, hg_w_out.astype(BF16), j, xf, mod_r, i, batch)
        xf = _ffn_call(xf, mod_r, ng_r, wg, wu, wd, i, 1, 2, batch)
    return xf.reshape(batch, seq, d)
```

```python
import functools
import math

import jax
import jax.numpy as jnp
from jax import lax
from jax.experimental import pallas as pl
from jax.experimental.pallas import tpu as pltpu

F32 = jnp.float32
BF16 = jnp.bfloat16

EPS = 1e-6
HEAD_DIM = 128
CHUNK = 64
CHUNK_SHIFT = 6
RG_C = 8.0
SUBLANES = 8
VMEM_LIMIT = 56 * 1024 * 1024


def _cparams(n_axes):
    return pltpu.CompilerParams(dimension_semantics=("arbitrary",) * n_axes,
                                vmem_limit_bytes=VMEM_LIMIT)


def _tile(n, preferred):
    if n <= preferred:
        return n
    t = preferred - preferred % HEAD_DIM
    while n % t:
        t -= HEAD_DIM
    return t


def _modulate(x, g, shift, scale):
    ms = jnp.mean(x * x, axis=-1, keepdims=True)
    return (x * lax.rsqrt(ms + EPS)) * (g * (1.0 + scale)) + shift


def _log_sigmoid(x):
    return jnp.minimum(x, 0.0) - jnp.log1p(jnp.exp(-jnp.abs(x)))


def _split3(x):
    hi = x.astype(BF16)
    r1 = x - hi.astype(F32)
    mid = r1.astype(BF16)
    lo = (r1 - mid.astype(F32)).astype(BF16)
    return hi, mid, lo


def _tril_dot(tril, x):
    hi, mid, lo = _split3(x)
    acc = jnp.dot(tril, hi, preferred_element_type=F32)
    acc = acc + jnp.dot(tril, mid, preferred_element_type=F32)
    return acc + jnp.dot(tril, lo, preferred_element_type=F32)


def _ada_kernel(c_ref, w_ref, b_ref, o_ref):
    c = c_ref[...]
    sc = (c * jax.nn.sigmoid(c)).astype(BF16)
    o_ref[...] = jnp.dot(sc, w_ref[...].astype(BF16), preferred_element_type=F32) + b_ref[...]


def _ada_call(c, ada_w, ada_b, tn=1024):
    depth, d, n = ada_w.shape
    b = c.shape[0]
    rows = -(-b // SUBLANES) * SUBLANES
    c_pad = jnp.pad(c, ((0, rows - b), (0, 0)))
    tn = _tile(n, tn)
    out = pl.pallas_call(
        _ada_kernel,
        out_shape=jax.ShapeDtypeStruct((depth, rows, n), F32),
        grid=(depth, n // tn),
        in_specs=[
            pl.BlockSpec((rows, d), lambda l, j: (0, 0)),
            pl.BlockSpec((None, d, tn), lambda l, j: (l, 0, j)),
            pl.BlockSpec((None, 1, tn), lambda l, j: (l, 0, j)),
        ],
        out_specs=pl.BlockSpec((None, rows, tn), lambda l, j: (l, 0, j)),
        compiler_params=_cparams(2),
        name="ada",
    )(c_pad, ada_w, ada_b.reshape(depth, 1, n))
    return out[:, :b]


def _row_chunks(tm, rc):
    return [slice(r, r + rc) for r in range(0, tm, rc)]


def _ffn_kernel(x_ref, shift_ref, scale_ref, gate_ref, g_ref, wg_ref, wu_ref, wd_ref, o_ref, h_scr, *, rc):
    f = pl.program_id(1)
    chunks = _row_chunks(x_ref.shape[0], rc)

    def weights():
        return wg_ref[...].astype(BF16), wu_ref[...].astype(BF16), wd_ref[...].astype(BF16)

    def swiglu(h, wg, wu, wd):
        gg = jnp.dot(h, wg, preferred_element_type=F32)
        uu = jnp.dot(h, wu, preferred_element_type=F32)
        a = ((gg * jax.nn.sigmoid(gg)) * uu).astype(BF16)
        return jnp.dot(a, wd, preferred_element_type=F32)

    @pl.when(f == 0)
    def _():
        w = weights()
        for rows in chunks:
            h = _modulate(x_ref[rows, :], g_ref[...], shift_ref[...], scale_ref[...]).astype(BF16)
            h_scr[rows, :] = h
            o_ref[rows, :] = swiglu(h, *w)

    @pl.when(f > 0)
    def _():
        w = weights()
        for rows in chunks:
            o_ref[rows, :] += swiglu(h_scr[rows, :], *w)

    @pl.when(f == pl.num_programs(1) - 1)
    def _():
        o_ref[...] = x_ref[...] + (0.5 * gate_ref[...]) * o_ref[...]


def _mod_spec(d, base, tiles_per_batch, n_grid_axes, col_axis=None, tn=None):
    width = d if tn is None else tn

    def index(*ids):
        col = 0 if col_axis is None else ids[col_axis]
        return (base + 9 * (ids[0] // tiles_per_batch), 0, col)

    del n_grid_axes
    return pl.BlockSpec((None, 1, width), index)


def _ffn_call(x, mod_r, ng_r, wg, wu, wd, layer, which, sub, batch, tm=1024, tf=256, rc=512):
    m, d = x.shape
    f_dim = wg.shape[-1]
    tm = min(tm, m // batch)
    rc = min(rc, tm)
    tf = _tile(f_dim, tf)
    tpb = (m // batch) // tm
    base = layer * batch * 9 + sub * 3
    return pl.pallas_call(
        functools.partial(_ffn_kernel, rc=rc),
        out_shape=jax.ShapeDtypeStruct((m, d), F32),
        grid=(m // tm, f_dim // tf),
        in_specs=[
            pl.BlockSpec((tm, d), lambda i, f: (i, 0)),
            _mod_spec(d, base + 0, tpb, 2),
            _mod_spec(d, base + 1, tpb, 2),
            _mod_spec(d, base + 2, tpb, 2),
            pl.BlockSpec((None, 1, d), lambda i, f: (layer * 3 + sub, 0, 0)),
            pl.BlockSpec((None, None, d, tf), lambda i, f: (layer, which, 0, f)),
            pl.BlockSpec((None, None, d, tf), lambda i, f: (layer, which, 0, f)),
            pl.BlockSpec((None, None, tf, d), lambda i, f: (layer, which, f, 0)),
        ],
        out_specs=pl.BlockSpec((tm, d), lambda i, f: (i, 0)),
        scratch_shapes=[pltpu.VMEM((tm, d), BF16)],
        compiler_params=_cparams(2),
        name="ffn",
    )(x, mod_r, mod_r, mod_r, ng_r, wg, wu, wd)


def _mm_mod_kernel(x_ref, shift_ref, scale_ref, g_ref, w_ref, o_ref, h_scr, *, rc):
    j = pl.program_id(1)
    chunks = _row_chunks(x_ref.shape[0], rc)

    @pl.when(j == 0)
    def _():
        w = w_ref[...].astype(BF16)
        for rows in chunks:
            h = _modulate(x_ref[rows, :], g_ref[...], shift_ref[...], scale_ref[...]).astype(BF16)
            h_scr[rows, :] = h
            o_ref[rows, :] = jnp.dot(h, w, preferred_element_type=F32).astype(o_ref.dtype)

    @pl.when(j > 0)
    def _():
        w = w_ref[...].astype(BF16)
        for rows in chunks:
            o_ref[rows, :] = jnp.dot(h_scr[rows, :], w, preferred_element_type=F32).astype(o_ref.dtype)


def _mm_mod_call(x, mod_r, ng_r, w, widx, n_out, layer, batch, out_dtype, tm=1024, tn=1024, rc=512):
    m, d = x.shape
    tm = min(tm, m // batch)
    rc = min(rc, tm)
    tn = _tile(n_out, tn)
    tpb = (m // batch) // tm
    base = layer * batch * 9 + 3
    return pl.pallas_call(
        functools.partial(_mm_mod_kernel, rc=rc),
        out_shape=jax.ShapeDtypeStruct((m, n_out), out_dtype),
        grid=(m // tm, n_out // tn),
        in_specs=[
            pl.BlockSpec((tm, d), lambda i, j: (i, 0)),
            _mod_spec(d, base + 0, tpb, 2),
            _mod_spec(d, base + 1, tpb, 2),
            pl.BlockSpec((None, 1, d), lambda i, j: (layer * 3 + 1, 0, 0)),
            pl.BlockSpec((None, d, tn), lambda i, j: (widx, 0, j)),
        ],
        out_specs=pl.BlockSpec((tm, tn), lambda i, j: (i, j)),
        scratch_shapes=[pltpu.VMEM((tm, d), BF16)],
        compiler_params=_cparams(2),
        name="in_proj",
    )(x, mod_r, mod_r, ng_r, w)


def _mm_res_kernel(y_ref, w_ref, x_ref, gate_ref, o_ref):
    acc = jnp.dot(y_ref[...], w_ref[...], preferred_element_type=F32)
    o_ref[...] = x_ref[...] + gate_ref[...] * acc


def _mm_res_call(y, w, widx, x, mod_r, layer, batch, tm=512, tn=2048):
    m, k = y.shape
    d = x.shape[1]
    tm = min(tm, m // batch)
    tn = _tile(d, tn)
    tpb = (m // batch) // tm
    base = layer * batch * 9 + 3 + 2
    return pl.pallas_call(
        _mm_res_kernel,
        out_shape=jax.ShapeDtypeStruct((m, d), F32),
        grid=(m // tm, d // tn),
        in_specs=[
            pl.BlockSpec((tm, k), lambda i, j: (i, 0)),
            pl.BlockSpec((None, k, tn), lambda i, j: (widx, 0, j)),
            pl.BlockSpec((tm, tn), lambda i, j: (i, j)),
            _mod_spec(d, base, tpb, 2, col_axis=1, tn=tn),
        ],
        out_specs=pl.BlockSpec((tm, tn), lambda i, j: (i, j)),
        compiler_params=_cparams(2),
        name="out_proj",
    )(y, w, x, mod_r)


def _res_specs(w_out, widx, ts, nst, d, gate_base):
    k = w_out.shape[1]
    return [
        pl.BlockSpec((None, k, d), lambda b, s: (widx, 0, 0)),
        pl.BlockSpec((ts, d), lambda b, s: (b * nst + s, 0)),
        pl.BlockSpec((None, 1, d), lambda b, s: (gate_base + 9 * b, 0, 0)),
    ]


def _project_residual(y, wout_ref, x_ref, mgate_ref, o_ref):
    o_ref[...] = x_ref[...] + mgate_ref[...] * jnp.dot(y, wout_ref[...], preferred_element_type=F32)


def _shift_in_tile(ext_scr, cur, ts):
    s = pl.program_id(1)

    @pl.when(s == 0)
    def _():
        ext_scr[0:SUBLANES, :] = jnp.zeros((SUBLANES, ext_scr.shape[1]), F32)

    @pl.when(s > 0)
    def _():
        ext_scr[0:SUBLANES, :] = ext_scr[ts:ts + SUBLANES, :]

    ext_scr[SUBLANES:ts + SUBLANES, :] = cur


def _causal_conv_from_ext(ext_scr, cw_ref, ts):
    kw = cw_ref.shape[0]
    acc = None
    for k in range(kw):
        term = cw_ref[k:k + 1, :] * ext_scr[pl.ds(SUBLANES - (kw - 1) + k, ts), :]
        acc = term if acc is None else acc + term
    return acc


def _rg_mid_kernel(gate_ref, xb_ref, cw_ref, cb_ref, wri_ref, br_ref, bi_ref, lam_ref,
                   wout_ref, x_ref, mgate_ref, o_ref, ext_scr, a_scr, b_scr, hs_scr, h_scr):
    ts, width = xb_ref.shape
    n_blocks, blk, _ = wri_ref.shape

    @pl.when(pl.program_id(1) == 0)
    def _():
        h_scr[...] = jnp.zeros_like(h_scr)

    _shift_in_tile(ext_scr, xb_ref[...].astype(F32), ts)
    xc = _causal_conv_from_ext(ext_scr, cw_ref, ts) + cb_ref[...]

    for g in range(n_blocks):
        cols = slice(g * blk, (g + 1) * blk)
        xg = xc[:, cols]
        ri = jnp.dot(xg.astype(BF16), wri_ref[g], preferred_element_type=F32)
        r = jax.nn.sigmoid(ri[:, :blk] + br_ref[:, cols])
        ig = jax.nn.sigmoid(ri[:, blk:] + bi_ref[:, cols])
        log_a = (RG_C * r) * _log_sigmoid(lam_ref[:, cols])
        a_scr[:, cols] = jnp.exp(log_a)
        b_scr[:, cols] = jnp.sqrt(1.0 - jnp.exp(2.0 * log_a)) * (ig * xg)

    row = lax.broadcasted_iota(jnp.int32, (SUBLANES, width), 0)

    def body(i, h):
        r0 = pl.multiple_of(i * SUBLANES, SUBLANES)
        a = a_scr[pl.ds(r0, SUBLANES), :]
        b = b_scr[pl.ds(r0, SUBLANES), :]
        for sh in (1, 2, 4):
            keep = row >= sh
            a_prev = jnp.where(keep, pltpu.roll(a, sh, 0), 1.0)
            b_prev = jnp.where(keep, pltpu.roll(b, sh, 0), 0.0)
            b = a * b_prev + b
            a = a * a_prev
        hs = a * h + b
        hs_scr[pl.ds(r0, SUBLANES), :] = hs
        return jnp.broadcast_to(hs[SUBLANES - 1:SUBLANES, :], (SUBLANES, width))

    h_scr[...] = lax.fori_loop(0, ts // SUBLANES, body, h_scr[...])
    y = (hs_scr[...] * jax.nn.gelu(gate_ref[...].astype(F32))).astype(BF16)
    _project_residual(y, wout_ref, x_ref, mgate_ref, o_ref)


def _rg_mid_call(p, conv_w, conv_b, w_ri, b_r, b_i, lam, w_out, widx, x, mod_r, layer, batch, ts=256):
    m, two_w = p.shape
    width = two_w // 2
    d = x.shape[1]
    seq = m // batch
    ts = min(ts, seq)
    nst = seq // ts
    n_blocks, blk, _ = w_ri.shape
    full = lambda shape: pl.BlockSpec(shape, lambda b, s: (0,) * len(shape))
    return pl.pallas_call(
        _rg_mid_kernel,
        out_shape=jax.ShapeDtypeStruct((m, d), F32),
        grid=(batch, nst),
        in_specs=[
            pl.BlockSpec((ts, width), lambda b, s: (b * nst + s, 0)),
            pl.BlockSpec((ts, width), lambda b, s: (b * nst + s, 1)),
            full(conv_w.shape),
            full((1, width)),
            full((n_blocks, blk, 2 * blk)),
            full((1, width)),
            full((1, width)),
            full((1, width)),
        ] + _res_specs(w_out, widx, ts, nst, d, layer * batch * 9 + 5),
        out_specs=pl.BlockSpec((ts, d), lambda b, s: (b * nst + s, 0)),
        scratch_shapes=[
            pltpu.VMEM((ts + SUBLANES, width), F32),
            pltpu.VMEM((ts, width), F32),
            pltpu.VMEM((ts, width), F32),
            pltpu.VMEM((ts, width), F32),
            pltpu.VMEM((SUBLANES, width), F32),
        ],
        compiler_params=_cparams(2),
        name="rglru_core",
    )(p, p, conv_w, conv_b.reshape(1, width), w_ri, b_r.reshape(1, width), b_i.reshape(1, width),
      lam.reshape(1, width), w_out, x, mod_r)


def _sc_mid_kernel(bg_ref, cg_ref, xv_ref, cw_ref, wout_ref, x_ref, mgate_ref, o_ref, ext_scr):
    ts = bg_ref.shape[0]
    _shift_in_tile(ext_scr, cg_ref[...].astype(F32) * xv_ref[...].astype(F32), ts)
    conv = _causal_conv_from_ext(ext_scr, cw_ref, ts)
    y = (bg_ref[...].astype(F32) * conv).astype(BF16)
    _project_residual(y, wout_ref, x_ref, mgate_ref, o_ref)


def _sc_mid_call(p, conv_w, w_out, widx, x, mod_r, layer, batch, ts=256):
    m, three_d = p.shape
    d = three_d // 3
    seq = m // batch
    ts = min(ts, seq)
    nst = seq // ts
    return pl.pallas_call(
        _sc_mid_kernel,
        out_shape=jax.ShapeDtypeStruct((m, d), F32),
        grid=(batch, nst),
        in_specs=[
            pl.BlockSpec((ts, d), lambda b, s: (b * nst + s, 0)),
            pl.BlockSpec((ts, d), lambda b, s: (b * nst + s, 1)),
            pl.BlockSpec((ts, d), lambda b, s: (b * nst + s, 2)),
            pl.BlockSpec(conv_w.shape, lambda b, s: (0, 0)),
        ] + _res_specs(w_out, widx, ts, nst, d, layer * batch * 9 + 5),
        out_specs=pl.BlockSpec((ts, d), lambda b, s: (b * nst + s, 0)),
        scratch_shapes=[pltpu.VMEM((ts + SUBLANES, d), F32)],
        compiler_params=_cparams(2),
        name="shortconv_core",
    )(p, p, p, conv_w, w_out, x, mod_r)


def _fox_prep_kernel(q_ref, k_ref, v_ref, fl_ref, bf_ref, qg_ref, kg_ref, qa_ref, ka_ref, vt_ref, carry_scr):
    ts, d = q_ref.shape

    @pl.when(pl.program_id(1) == 0)
    def _():
        carry_scr[...] = jnp.zeros_like(carry_scr)

    logf = _log_sigmoid(fl_ref[...] + bf_ref[...])
    r = lax.broadcasted_iota(jnp.int32, (ts, ts), 0)
    c = lax.broadcasted_iota(jnp.int32, (ts, ts), 1)
    tril = jnp.where(c <= r, 1.0, 0.0).astype(BF16)
    cum = _tril_dot(tril, logf) + carry_scr[0:1, :]
    carry_scr[...] = jnp.broadcast_to(cum[ts - 1:ts, :], carry_scr.shape)
    f_over_scale = cum * math.sqrt(HEAD_DIM)

    lane = lax.broadcasted_iota(jnp.int32, (ts, HEAD_DIM), 1)
    for h in range(d // HEAD_DIM):
        cols = slice(h * HEAD_DIM, (h + 1) * HEAD_DIM)
        hi, mid, lo = (part.astype(F32) for part in _split3(f_over_scale[:, h:h + 1]))
        parts = jnp.where((lane == 0) | (lane == 3), hi, jnp.where((lane == 1) | (lane == 4), mid, lo))
        q_aug = jnp.where(lane < 3, parts, jnp.where(lane < 6, 1.0, 0.0))
        k_aug = jnp.where(lane < 3, 1.0, jnp.where(lane < 6, -parts, 0.0))
        for src, gain, aug, dst in ((q_ref, qg_ref, q_aug, qa_ref), (k_ref, kg_ref, k_aug, ka_ref)):
            t = src[:, cols].astype(F32)
            ms = jnp.mean(t * t, axis=-1, keepdims=True)
            dst[:, 2 * h * HEAD_DIM:(2 * h + 1) * HEAD_DIM] = ((t * lax.rsqrt(ms + EPS)) * gain[...]).astype(BF16)
            dst[:, (2 * h + 1) * HEAD_DIM:(2 * h + 2) * HEAD_DIM] = aug.astype(BF16)
        vt_ref[h] = v_ref[:, cols].astype(F32).T.astype(BF16)


def _fox_prep_call(p, fl, b_f_pad, q_gain, k_gain, batch, ts):
    m = p.shape[0]
    d = p.shape[1] // 4
    n_heads = d // HEAD_DIM
    seq = m // batch
    nst = seq // ts
    lanes = fl.shape[1]
    return pl.pallas_call(
        _fox_prep_kernel,
        out_shape=(
            jax.ShapeDtypeStruct((m, 2 * d), BF16),
            jax.ShapeDtypeStruct((m, 2 * d), BF16),
            jax.ShapeDtypeStruct((batch, nst, n_heads, HEAD_DIM, ts), BF16),
        ),
        grid=(batch, nst),
        in_specs=[
            pl.BlockSpec((ts, d), lambda b, s: (b * nst + s, 0)),
            pl.BlockSpec((ts, d), lambda b, s: (b * nst + s, 1)),
            pl.BlockSpec((ts, d), lambda b, s: (b * nst + s, 2)),
            pl.BlockSpec((ts, lanes), lambda b, s: (b * nst + s, 0)),
            pl.BlockSpec((1, lanes), lambda b, s: (0, 0)),
            pl.BlockSpec((1, HEAD_DIM), lambda b, s: (0, 0)),
            pl.BlockSpec((1, HEAD_DIM), lambda b, s: (0, 0)),
        ],
        out_specs=(
            pl.BlockSpec((ts, 2 * d), lambda b, s: (b * nst + s, 0)),
            pl.BlockSpec((ts, 2 * d), lambda b, s: (b * nst + s, 0)),
            pl.BlockSpec((None, None, n_heads, HEAD_DIM, ts), lambda b, s: (b, s, 0, 0, 0)),
        ),
        scratch_shapes=[pltpu.VMEM((SUBLANES, lanes), F32)],
        compiler_params=_cparams(2),
        name="fox_prep",
    )(p, p, p, fl, b_f_pad, q_gain.reshape(1, HEAD_DIM), k_gain.reshape(1, HEAD_DIM))


def _fox_flash_kernel(q_ref, k_ref, vt_ref, g_ref, o_ref):
    t = q_ref.shape[0]
    _, hp, dh, _ = vt_ref.shape
    qi = pl.program_id(2)
    to_log2 = (1.0 / math.sqrt(dh)) * math.log2(math.e)
    nt = (((1,), (1,)), ((), ()))

    def step(j, carry, masked):
        out = []
        for e in range(hp):
            m_prev, l_prev, acc = carry[e]
            cols = slice(2 * e * dh, 2 * (e + 1) * dh)
            kj = k_ref[pl.ds(pl.multiple_of(j * t, t), t), cols]
            s = lax.dot_general(kj, q_ref[:, cols], nt, preferred_element_type=F32) * to_log2
            if masked:
                key = lax.broadcasted_iota(jnp.int32, (t, t), 0)
                qry = lax.broadcasted_iota(jnp.int32, (t, t), 1)
                s = jnp.where(key <= qry, s, -jnp.inf)
            m_new = jnp.maximum(m_prev, jnp.max(s, axis=0, keepdims=True))
            alpha = jnp.exp2(m_prev - m_new)
            p = jnp.exp2(s - m_new)
            l_new = alpha * l_prev + jnp.sum(p, axis=0, keepdims=True)
            acc = alpha * acc + jnp.dot(vt_ref[j, e], p.astype(BF16), preferred_element_type=F32)
            out.append((m_new, l_new, acc))
        return tuple(out)

    init = tuple((jnp.full((1, t), -jnp.inf, F32), jnp.zeros((1, t), F32), jnp.zeros((dh, t), F32))
                 for _ in range(hp))
    carry = lax.fori_loop(0, qi, functools.partial(step, masked=False), init)
    fin = step(qi, carry, True)
    for e in range(hp):
        _, l_fin, acc = fin[e]
        cols = slice(e * dh, (e + 1) * dh)
        o = (acc / l_fin).T
        o_ref[:, cols] = (o * jax.nn.sigmoid(g_ref[:, cols].astype(F32))).astype(BF16)


def _fox_flash_call(qa, ka, vt, p, batch, t, hp=2):
    m = qa.shape[0]
    n_heads = vt.shape[2]
    d = n_heads * HEAD_DIM
    seq = m // batch
    nq = seq // t
    hp = math.gcd(hp, n_heads)
    g_blk = 3 * n_heads // hp
    return pl.pallas_call(
        _fox_flash_kernel,
        out_shape=jax.ShapeDtypeStruct((m, d), BF16),
        grid=(batch, n_heads // hp, nq),
        in_specs=[
            pl.BlockSpec((t, 2 * hp * HEAD_DIM), lambda b, h, i: (b * nq + i, h)),
            pl.BlockSpec((seq, 2 * hp * HEAD_DIM), lambda b, h, i: (b, h)),
            pl.BlockSpec((None, nq, hp, HEAD_DIM, t), lambda b, h, i: (b, 0, h, 0, 0)),
            pl.BlockSpec((t, hp * HEAD_DIM), lambda b, h, i: (b * nq + i, g_blk + h)),
        ],
        out_specs=pl.BlockSpec((t, hp * HEAD_DIM), lambda b, h, i: (b * nq + i, h)),
        compiler_params=_cparams(3),
        name="fox_flash",
    )(qa, ka, vt, p)


def _hg_mid_kernel(q_ref, fz_ref, iv_ref, g_ref, lbl_ref, ng_ref, wout_ref, x_ref, mgate_ref, o_ref,
                   qin_scr, kin_scr, kdec_scr, dec_scr, st_scr, y_scr, *, layer):
    ts, d = q_ref.shape
    n_heads = d // HEAD_DIM
    n_chunks = ts // CHUNK

    @pl.when(pl.program_id(1) == 0)
    def _():
        st_scr[...] = jnp.zeros_like(st_scr)

    lg = lbl_ref[...]
    depth = lg.shape[0]
    mx = lg[0:1, :]
    for i in range(1, depth):
        mx = jnp.maximum(mx, lg[i:i + 1, :])
    es = [jnp.exp(lg[i:i + 1, :] - mx) for i in range(depth)]
    tot = es[0]
    for i in range(1, depth):
        tot = tot + es[i]
    lb = jnp.zeros_like(tot)
    for i in range(layer):
        lb = lb + es[i] / tot

    sig = jax.nn.sigmoid(fz_ref[...].astype(F32))
    logf = jnp.log(lb + (1.0 - lb) * sig)
    kk = (1.0 - lb) * (1.0 - sig)

    r = lax.broadcasted_iota(jnp.int32, (ts, ts), 0)
    c = lax.broadcasted_iota(jnp.int32, (ts, ts), 1)
    tril = jnp.where((c <= r) & (jnp.right_shift(c, CHUNK_SHIFT) == jnp.right_shift(r, CHUNK_SHIFT)), 1.0, 0.0).astype(BF16)
    bcum = _tril_dot(tril, logf)

    qin_scr[...] = (q_ref[...].astype(F32) * jnp.exp(bcum)).astype(BF16)
    kin_scr[...] = (kk * jnp.exp(-bcum)).astype(BF16)
    for ci in range(n_chunks):
        rows = slice(ci * CHUNK, (ci + 1) * CHUNK)
        b_last = bcum[(ci + 1) * CHUNK - 1:(ci + 1) * CHUNK, :]
        kdec_scr[rows, :] = (kk[rows, :] * jnp.exp(b_last - bcum[rows, :])).astype(BF16)
        dec_scr[ci] = jnp.broadcast_to(jnp.exp(b_last), (SUBLANES, d))

    rr = lax.broadcasted_iota(jnp.int32, (CHUNK, CHUNK), 0)
    cc = lax.broadcasted_iota(jnp.int32, (CHUNK, CHUNK), 1)
    causal = cc <= rr
    nt = (((1,), (1,)), ((), ()))
    tn = (((0,), (0,)), ((), ()))

    def chunk_body(ci, carry):
        r0 = pl.multiple_of(ci * CHUNK, CHUNK)
        rows = pl.ds(r0, CHUNK)
        for h in range(n_heads):
            cols = slice(h * HEAD_DIM, (h + 1) * HEAD_DIM)
            qh = qin_scr[rows, cols]
            kh = kin_scr[rows, cols]
            kd = kdec_scr[rows, cols]
            vh = iv_ref[rows, cols].astype(BF16)
            st = st_scr[h]
            sc = lax.dot_general(qh, kh, nt, preferred_element_type=F32)
            sc = jnp.where(causal, sc, 0.0).astype(BF16)
            o = jnp.dot(sc, vh, preferred_element_type=F32)
            o = o + lax.dot_general(qh, st.astype(BF16), nt, preferred_element_type=F32)
            u_t = lax.dot_general(vh, kd, tn, preferred_element_type=F32)
            st_scr[h] = st * dec_scr[ci][0:1, cols] + u_t
            ms = jnp.mean(o * o, axis=-1, keepdims=True)
            on = (o * lax.rsqrt(ms + EPS)) * ng_ref[...]
            gh = g_ref[rows, cols].astype(F32)
            y_scr[rows, cols] = (on * (gh * jax.nn.sigmoid(gh))).astype(BF16)
        return carry

    lax.fori_loop(0, n_chunks, chunk_body, 0)
    _project_residual(y_scr[...], wout_ref, x_ref, mgate_ref, o_ref)


def _hg_mid_call(p, lb_logits, norm_gain, w_out, widx, x, mod_r, layer, batch, ts=256):
    m, four_d = p.shape
    d = four_d // 4
    n_heads = d // HEAD_DIM
    seq = m // batch
    ts = min(ts, seq)
    nst = seq // ts
    return pl.pallas_call(
        functools.partial(_hg_mid_kernel, layer=layer),
        out_shape=jax.ShapeDtypeStruct((m, d), F32),
        grid=(batch, nst),
        in_specs=[
            pl.BlockSpec((ts, d), lambda b, s: (b * nst + s, 0)),
            pl.BlockSpec((ts, d), lambda b, s: (b * nst + s, 1)),
            pl.BlockSpec((ts, d), lambda b, s: (b * nst + s, 2)),
            pl.BlockSpec((ts, d), lambda b, s: (b * nst + s, 3)),
            pl.BlockSpec(lb_logits.shape, lambda b, s: (0, 0)),
            pl.BlockSpec((1, HEAD_DIM), lambda b, s: (0, 0)),
        ] + _res_specs(w_out, widx, ts, nst, d, layer * batch * 9 + 5),
        out_specs=pl.BlockSpec((ts, d), lambda b, s: (b * nst + s, 0)),
        scratch_shapes=[
            pltpu.VMEM((ts, d), BF16),
            pltpu.VMEM((ts, d), BF16),
            pltpu.VMEM((ts, d), BF16),
            pltpu.VMEM((ts // CHUNK, SUBLANES, d), F32),
            pltpu.VMEM((n_heads, HEAD_DIM, HEAD_DIM), F32),
            pltpu.VMEM((ts, d), BF16),
        ],
        compiler_params=_cparams(2),
        name="hgrn2_core",
    )(p, p, p, p, lb_logits, norm_gain.reshape(1, HEAD_DIM), w_out, x, mod_r)


def kernel(x, c, ada_w, ada_b, norm_g, ffn_w_gate, ffn_w_up, ffn_w_down, rg_w_in, rg_conv_w, rg_conv_b, rg_w_r, rg_b_r, rg_w_i, rg_b_i, rg_lam, rg_w_out, sc_w_in, sc_conv_w, sc_w_out, fox_w_in, fox_b_f, fox_q_gain, fox_k_gain, fox_w_out, hg_w_in, hg_lb_logits, hg_norm_gain, hg_w_out):
    batch, seq, d = x.shape
    depth = ada_w.shape[0]
    n_heads = d // HEAD_DIM
    n_mixers = 4
    mid_dtype = BF16

    mod = _ada_call(c, ada_w, ada_b)
    mod_r = mod.reshape(depth * batch * 9, 1, d)
    ng_r = norm_g.reshape(depth * 3, 1, d)

    wg, wu, wd = ffn_w_gate, ffn_w_up, ffn_w_down

    xf = x.reshape(batch * seq, d)
    for i in range(depth):
        xf = _ffn_call(xf, mod_r, ng_r, wg, wu, wd, i, 0, 0, batch)
        m, j = i % n_mixers, i // n_mixers
        if m == 0:
            p = _mm_mod_call(xf, mod_r, ng_r, rg_w_in, j, rg_w_in.shape[2], i, batch, mid_dtype)
            w_ri = jnp.concatenate([rg_w_r[j], rg_w_i[j]], axis=-1).astype(BF16)
            xf = _rg_mid_call(p, rg_conv_w[j], rg_conv_b[j], w_ri, rg_b_r[j], rg_b_i[j], rg_lam[j],
                              rg_w_out.astype(BF16), j, xf, mod_r, i, batch)
        elif m == 1:
            p = _mm_mod_call(xf, mod_r, ng_r, sc_w_in, j, sc_w_in.shape[2], i, batch, mid_dtype)
            xf = _sc_mid_call(p, sc_conv_w[j], sc_w_out.astype(BF16), j, xf, mod_r, i, batch)
        elif m == 2:
            p = _mm_mod_call(xf, mod_r, ng_r, fox_w_in.astype(BF16), j, 4 * d, i, batch, mid_dtype)
            w_fl = jnp.pad(fox_w_in[j][:, 4 * d:], ((0, 0), (0, HEAD_DIM - n_heads)))[None]
            fl = _mm_mod_call(xf, mod_r, ng_r, w_fl, 0, HEAD_DIM, i, batch, F32)
            b_f_pad = jnp.pad(fox_b_f[j], (0, HEAD_DIM - n_heads)).reshape(1, HEAD_DIM)
            t_attn = min(512, seq)
            qa, ka, vt = _fox_prep_call(p, fl, b_f_pad, fox_q_gain[j], fox_k_gain[j], batch, t_attn)
            y = _fox_flash_call(qa, ka, vt, p, batch, t_attn)
            xf = _mm_res_call(y, fox_w_out.astype(BF16), j, xf, mod_r, i, batch)
        else:
            p = _mm_mod_call(xf, mod_r, ng_r, hg_w_in, j, hg_w_in.shape[2], i, batch, mid_dtype)
            xf = _hg_mid_call(p, hg_lb_logits, hg_norm_gain[j], hg_w_out.astype(BF16), j, xf, mod_r, i, batch)
        xf = _ffn_call(xf, mod_r, ng_r, wg, wu, wd, i, 1, 2, batch)
    return xf.reshape(batch, seq, d)
```

```python
import functools
import math

import jax
import jax.numpy as jnp
from jax import lax
from jax.experimental import pallas as pl
from jax.experimental.pallas import tpu as pltpu

F32 = jnp.float32
BF16 = jnp.bfloat16

EPS = 1e-6
HEAD_DIM = 128
CHUNK = 64
CHUNK_SHIFT = 6
RG_C = 8.0
SUBLANES = 8
VMEM_LIMIT = 56 * 1024 * 1024


def _cparams(n_axes):
    return pltpu.CompilerParams(dimension_semantics=("arbitrary",) * n_axes,
                                vmem_limit_bytes=VMEM_LIMIT)


def _tile(n, preferred):
    if n <= preferred:
        return n
    t = preferred - preferred % HEAD_DIM
    while n % t:
        t -= HEAD_DIM
    return t


def _modulate(x, g, shift, scale):
    ms = jnp.mean(x * x, axis=-1, keepdims=True)
    return (x * lax.rsqrt(ms + EPS)) * (g * (1.0 + scale)) + shift


def _log_sigmoid(x):
    return jnp.minimum(x, 0.0) - jnp.log1p(jnp.exp(-jnp.abs(x)))


def _split3(x):
    hi = x.astype(BF16)
    r1 = x - hi.astype(F32)
    mid = r1.astype(BF16)
    lo = (r1 - mid.astype(F32)).astype(BF16)
    return hi, mid, lo


def _tril_dot(tril, x):
    hi, mid, lo = _split3(x)
    acc = jnp.dot(tril, hi, preferred_element_type=F32)
    acc = acc + jnp.dot(tril, mid, preferred_element_type=F32)
    return acc + jnp.dot(tril, lo, preferred_element_type=F32)


def _ada_kernel(c_ref, w_ref, b_ref, o_ref):
    c = c_ref[...]
    sc = (c * jax.nn.sigmoid(c)).astype(BF16)
    o_ref[...] = jnp.dot(sc, w_ref[...].astype(BF16), preferred_element_type=F32) + b_ref[...]


def _ada_call(c, ada_w, ada_b, tn=1024):
    depth, d, n = ada_w.shape
    b = c.shape[0]
    rows = -(-b // SUBLANES) * SUBLANES
    c_pad = jnp.pad(c, ((0, rows - b), (0, 0)))
    tn = _tile(n, tn)
    out = pl.pallas_call(
        _ada_kernel,
        out_shape=jax.ShapeDtypeStruct((depth, rows, n), F32),
        grid=(depth, n // tn),
        in_specs=[
            pl.BlockSpec((rows, d), lambda l, j: (0, 0)),
            pl.BlockSpec((None, d, tn), lambda l, j: (l, 0, j)),
            pl.BlockSpec((None, 1, tn), lambda l, j: (l, 0, j)),
        ],
        out_specs=pl.BlockSpec((None, rows, tn), lambda l, j: (l, 0, j)),
        compiler_params=_cparams(2),
        name="ada",
    )(c_pad, ada_w, ada_b.reshape(depth, 1, n))
    return out[:, :b]


def _row_chunks(tm, rc):
    return [slice(r, r + rc) for r in range(0, tm, rc)]


def _ffn_kernel(x_ref, shift_ref, scale_ref, gate_ref, g_ref, wg_ref, wu_ref, wd_ref, o_ref, h_scr, *, rc, nf):
    f = pl.program_id(1)
    chunks = _row_chunks(x_ref.shape[0], rc)

    def weights():
        return wg_ref[...].astype(BF16), wu_ref[...].astype(BF16), wd_ref[...].astype(BF16)

    def swiglu(h, wg, wu, wd):
        gg = jnp.dot(h, wg, preferred_element_type=F32)
        uu = jnp.dot(h, wu, preferred_element_type=F32)
        a = ((gg * jax.nn.sigmoid(gg)) * uu).astype(BF16)
        return jnp.dot(a, wd, preferred_element_type=F32)

    def run(first, last):
        w = weights()
        half_gate = 0.5 * gate_ref[...] if last else None
        for rows in chunks:
            if first:
                h = _modulate(x_ref[rows, :], g_ref[...], shift_ref[...], scale_ref[...]).astype(BF16)
                h_scr[rows, :] = h
                acc = swiglu(h, *w)
            else:
                acc = o_ref[rows, :] + swiglu(h_scr[rows, :], *w)
            if last:
                acc = x_ref[rows, :] + half_gate * acc
            o_ref[rows, :] = acc

    if nf == 1:
        run(True, True)
    else:
        pl.when(f == 0)(functools.partial(run, True, False))
        pl.when((f > 0) & (f < nf - 1))(functools.partial(run, False, False))
        pl.when(f == nf - 1)(functools.partial(run, False, True))


def _mod_spec(d, base, tiles_per_batch, n_grid_axes, col_axis=None, tn=None):
    width = d if tn is None else tn

    def index(*ids):
        col = 0 if col_axis is None else ids[col_axis]
        return (base + 9 * (ids[0] // tiles_per_batch), 0, col)

    del n_grid_axes
    return pl.BlockSpec((None, 1, width), index)


def _ffn_call(x, mod_r, ng_r, wg, wu, wd, wl, ww, layer, sub, batch, tm=1024, tf=256, rc=512):
    m, d = x.shape
    f_dim = wg.shape[-1]
    tm = min(tm, m // batch)
    rc = min(rc, tm)
    tf = _tile(f_dim, tf)
    tpb = (m // batch) // tm
    base = layer * batch * 9 + sub * 3
    return pl.pallas_call(
        functools.partial(_ffn_kernel, rc=rc, nf=f_dim // tf),
        out_shape=jax.ShapeDtypeStruct((m, d), F32),
        grid=(m // tm, f_dim // tf),
        in_specs=[
            pl.BlockSpec((tm, d), lambda i, f: (i, 0)),
            _mod_spec(d, base + 0, tpb, 2),
            _mod_spec(d, base + 1, tpb, 2),
            _mod_spec(d, base + 2, tpb, 2),
            pl.BlockSpec((None, 1, d), lambda i, f: (layer * 3 + sub, 0, 0)),
            pl.BlockSpec((None, None, d, tf), lambda i, f: (wl, ww, 0, f)),
            pl.BlockSpec((None, None, d, tf), lambda i, f: (wl, ww, 0, f)),
            pl.BlockSpec((None, None, tf, d), lambda i, f: (wl, ww, f, 0)),
        ],
        out_specs=pl.BlockSpec((tm, d), lambda i, f: (i, 0)),
        scratch_shapes=[pltpu.VMEM((tm, d), BF16)],
        compiler_params=_cparams(2),
        name="ffn",
    )(x, mod_r, mod_r, mod_r, ng_r, wg, wu, wd)


def _mm_mod_kernel(x_ref, shift_ref, scale_ref, g_ref, w_ref, o_ref, h_scr, *, rc, w_t):
    j = pl.program_id(1)
    chunks = _row_chunks(x_ref.shape[0], rc)
    dims = (((1,), (1,)), ((), ())) if w_t else (((1,), (0,)), ((), ()))

    def project(h, w):
        return lax.dot_general(h, w, dims, preferred_element_type=F32).astype(o_ref.dtype)

    @pl.when(j == 0)
    def _():
        w = w_ref[...].astype(BF16)
        for rows in chunks:
            h = _modulate(x_ref[rows, :], g_ref[...], shift_ref[...], scale_ref[...]).astype(BF16)
            h_scr[rows, :] = h
            o_ref[rows, :] = project(h, w)

    @pl.when(j > 0)
    def _():
        w = w_ref[...].astype(BF16)
        for rows in chunks:
            o_ref[rows, :] = project(h_scr[rows, :], w)


def _mm_mod_call(x, mod_r, ng_r, w, widx, n_out, layer, batch, out_dtype, tm=1024, tn=1024, rc=512, w_t=False):
    m, d = x.shape
    tm = min(tm, m // batch)
    rc = min(rc, tm)
    tn = _tile(n_out, tn)
    tpb = (m // batch) // tm
    base = layer * batch * 9 + 3
    if w_t:
        w_spec = pl.BlockSpec((None, tn, d), lambda i, j: (widx, j, 0))
    else:
        w_spec = pl.BlockSpec((None, d, tn), lambda i, j: (widx, 0, j))
    return pl.pallas_call(
        functools.partial(_mm_mod_kernel, rc=rc, w_t=w_t),
        out_shape=jax.ShapeDtypeStruct((m, n_out), out_dtype),
        grid=(m // tm, n_out // tn),
        in_specs=[
            pl.BlockSpec((tm, d), lambda i, j: (i, 0)),
            _mod_spec(d, base + 0, tpb, 2),
            _mod_spec(d, base + 1, tpb, 2),
            pl.BlockSpec((None, 1, d), lambda i, j: (layer * 3 + 1, 0, 0)),
            w_spec,
        ],
        out_specs=pl.BlockSpec((tm, tn), lambda i, j: (i, j)),
        scratch_shapes=[pltpu.VMEM((tm, d), BF16)],
        compiler_params=_cparams(2),
        name="in_proj",
    )(x, mod_r, mod_r, ng_r, w)


def _mm_res_kernel(y_ref, w_ref, x_ref, gate_ref, o_ref):
    acc = jnp.dot(y_ref[...], w_ref[...], preferred_element_type=F32)
    o_ref[...] = x_ref[...] + gate_ref[...] * acc


def _mm_res_call(y, w, widx, x, mod_r, layer, batch, tm=512, tn=2048):
    m, k = y.shape
    d = x.shape[1]
    tm = min(tm, m // batch)
    tn = _tile(d, tn)
    tpb = (m // batch) // tm
    base = layer * batch * 9 + 3 + 2
    return pl.pallas_call(
        _mm_res_kernel,
        out_shape=jax.ShapeDtypeStruct((m, d), F32),
        grid=(m // tm, d // tn),
        in_specs=[
            pl.BlockSpec((tm, k), lambda i, j: (i, 0)),
            pl.BlockSpec((None, k, tn), lambda i, j: (widx, 0, j)),
            pl.BlockSpec((tm, tn), lambda i, j: (i, j)),
            _mod_spec(d, base, tpb, 2, col_axis=1, tn=tn),
        ],
        out_specs=pl.BlockSpec((tm, tn), lambda i, j: (i, j)),
        compiler_params=_cparams(2),
        name="out_proj",
    )(y, w, x, mod_r)


def _res_specs(w_out, widx, ts, nst, d, gate_base):
    k = w_out.shape[1]
    return [
        pl.BlockSpec((None, k, d), lambda b, s: (widx, 0, 0)),
        pl.BlockSpec((ts, d), lambda b, s: (b * nst + s, 0)),
        pl.BlockSpec((None, 1, d), lambda b, s: (gate_base + 9 * b, 0, 0)),
    ]


def _project_residual(y, wout_ref, x_ref, mgate_ref, o_ref):
    o_ref[...] = x_ref[...] + mgate_ref[...] * jnp.dot(y, wout_ref[...], preferred_element_type=F32)


def _shift_in_tile(ext_scr, cur, ts):
    s = pl.program_id(1)

    @pl.when(s == 0)
    def _():
        ext_scr[0:SUBLANES, :] = jnp.zeros((SUBLANES, ext_scr.shape[1]), F32)

    @pl.when(s > 0)
    def _():
        ext_scr[0:SUBLANES, :] = ext_scr[ts:ts + SUBLANES, :]

    ext_scr[SUBLANES:ts + SUBLANES, :] = cur


def _causal_conv_from_ext(ext_scr, cw_ref, ts):
    kw = cw_ref.shape[0]
    acc = None
    for k in range(kw):
        term = cw_ref[k:k + 1, :] * ext_scr[pl.ds(SUBLANES - (kw - 1) + k, ts), :]
        acc = term if acc is None else acc + term
    return acc


def _rg_mid_kernel(gate_ref, xb_ref, cw_ref, cb_ref, wri_ref, br_ref, bi_ref, lam_ref,
                   wout_ref, x_ref, mgate_ref, o_ref, ext_scr, a_scr, b_scr, hs_scr, h_scr):
    ts, width = xb_ref.shape
    n_blocks, blk, _ = wri_ref.shape

    @pl.when(pl.program_id(1) == 0)
    def _():
        h_scr[...] = jnp.zeros_like(h_scr)

    _shift_in_tile(ext_scr, xb_ref[...].astype(F32), ts)
    xc = _causal_conv_from_ext(ext_scr, cw_ref, ts) + cb_ref[...]

    for g in range(n_blocks):
        cols = slice(g * blk, (g + 1) * blk)
        xg = xc[:, cols]
        ri = jnp.dot(xg.astype(BF16), wri_ref[g], preferred_element_type=F32)
        r = jax.nn.sigmoid(ri[:, :blk] + br_ref[:, cols])
        ig = jax.nn.sigmoid(ri[:, blk:] + bi_ref[:, cols])
        log_a = (RG_C * r) * _log_sigmoid(lam_ref[:, cols])
        a_scr[:, cols] = jnp.exp(log_a)
        b_scr[:, cols] = jnp.sqrt(1.0 - jnp.exp(2.0 * log_a)) * (ig * xg)

    row = lax.broadcasted_iota(jnp.int32, (SUBLANES, width), 0)

    def body(i, h):
        r0 = pl.multiple_of(i * SUBLANES, SUBLANES)
        a = a_scr[pl.ds(r0, SUBLANES), :]
        b = b_scr[pl.ds(r0, SUBLANES), :]
        for sh in (1, 2, 4):
            keep = row >= sh
            a_prev = jnp.where(keep, pltpu.roll(a, sh, 0), 1.0)
            b_prev = jnp.where(keep, pltpu.roll(b, sh, 0), 0.0)
            b = a * b_prev + b
            a = a * a_prev
        hs = a * h + b
        hs_scr[pl.ds(r0, SUBLANES), :] = hs
        return jnp.broadcast_to(hs[SUBLANES - 1:SUBLANES, :], (SUBLANES, width))

    h_scr[...] = lax.fori_loop(0, ts // SUBLANES, body, h_scr[...])
    y = (hs_scr[...] * jax.nn.gelu(gate_ref[...].astype(F32))).astype(BF16)
    _project_residual(y, wout_ref, x_ref, mgate_ref, o_ref)


def _rg_mid_call(p, conv_w, conv_b, w_ri, b_r, b_i, lam, w_out, widx, x, mod_r, layer, batch, ts=256):
    m, two_w = p.shape
    width = two_w // 2
    d = x.shape[1]
    seq = m // batch
    ts = min(ts, seq)
    nst = seq // ts
    n_blocks, blk, _ = w_ri.shape
    full = lambda shape: pl.BlockSpec(shape, lambda b, s: (0,) * len(shape))
    return pl.pallas_call(
        _rg_mid_kernel,
        out_shape=jax.ShapeDtypeStruct((m, d), F32),
        grid=(batch, nst),
        in_specs=[
            pl.BlockSpec((ts, width), lambda b, s: (b * nst + s, 0)),
            pl.BlockSpec((ts, width), lambda b, s: (b * nst + s, 1)),
            full(conv_w.shape),
            full((1, width)),
            full((n_blocks, blk, 2 * blk)),
            full((1, width)),
            full((1, width)),
            full((1, width)),
        ] + _res_specs(w_out, widx, ts, nst, d, layer * batch * 9 + 5),
        out_specs=pl.BlockSpec((ts, d), lambda b, s: (b * nst + s, 0)),
        scratch_shapes=[
            pltpu.VMEM((ts + SUBLANES, width), F32),
            pltpu.VMEM((ts, width), F32),
            pltpu.VMEM((ts, width), F32),
            pltpu.VMEM((ts, width), F32),
            pltpu.VMEM((SUBLANES, width), F32),
        ],
        compiler_params=_cparams(2),
        name="rglru_core",
    )(p, p, conv_w, conv_b.reshape(1, width), w_ri, b_r.reshape(1, width), b_i.reshape(1, width),
      lam.reshape(1, width), w_out, x, mod_r)


def _sc_mid_kernel(bg_ref, cg_ref, xv_ref, cw_ref, wout_ref, x_ref, mgate_ref, o_ref, ext_scr):
    ts = bg_ref.shape[0]
    _shift_in_tile(ext_scr, cg_ref[...].astype(F32) * xv_ref[...].astype(F32), ts)
    conv = _causal_conv_from_ext(ext_scr, cw_ref, ts)
    y = (bg_ref[...].astype(F32) * conv).astype(BF16)
    _project_residual(y, wout_ref, x_ref, mgate_ref, o_ref)


def _sc_mid_call(p, conv_w, w_out, widx, x, mod_r, layer, batch, ts=256):
    m, three_d = p.shape
    d = three_d // 3
    seq = m // batch
    ts = min(ts, seq)
    nst = seq // ts
    return pl.pallas_call(
        _sc_mid_kernel,
        out_shape=jax.ShapeDtypeStruct((m, d), F32),
        grid=(batch, nst),
        in_specs=[
            pl.BlockSpec((ts, d), lambda b, s: (b * nst + s, 0)),
            pl.BlockSpec((ts, d), lambda b, s: (b * nst + s, 1)),
            pl.BlockSpec((ts, d), lambda b, s: (b * nst + s, 2)),
            pl.BlockSpec(conv_w.shape, lambda b, s: (0, 0)),
        ] + _res_specs(w_out, widx, ts, nst, d, layer * batch * 9 + 5),
        out_specs=pl.BlockSpec((ts, d), lambda b, s: (b * nst + s, 0)),
        scratch_shapes=[pltpu.VMEM((ts + SUBLANES, d), F32)],
        compiler_params=_cparams(2),
        name="shortconv_core",
    )(p, p, p, conv_w, w_out, x, mod_r)


def _fox_prep_kernel(q_ref, k_ref, v_ref, fl_ref, bf_ref, qg_ref, kg_ref, qa_ref, ka_ref, vt_ref, carry_scr):
    ts, d = q_ref.shape

    @pl.when(pl.program_id(1) == 0)
    def _():
        carry_scr[...] = jnp.zeros_like(carry_scr)

    logf = _log_sigmoid(fl_ref[...] + bf_ref[...])
    r = lax.broadcasted_iota(jnp.int32, (ts, ts), 0)
    c = lax.broadcasted_iota(jnp.int32, (ts, ts), 1)
    tril = jnp.where(c <= r, 1.0, 0.0).astype(BF16)
    cum = _tril_dot(tril, logf) + carry_scr[0:1, :]
    carry_scr[...] = jnp.broadcast_to(cum[ts - 1:ts, :], carry_scr.shape)
    f_over_scale = cum * math.sqrt(HEAD_DIM)

    lane = lax.broadcasted_iota(jnp.int32, (ts, HEAD_DIM), 1)
    for h in range(d // HEAD_DIM):
        cols = slice(h * HEAD_DIM, (h + 1) * HEAD_DIM)
        hi, mid, lo = (part.astype(F32) for part in _split3(f_over_scale[:, h:h + 1]))
        parts = jnp.where((lane == 0) | (lane == 3), hi, jnp.where((lane == 1) | (lane == 4), mid, lo))
        q_aug = jnp.where(lane < 3, parts, jnp.where(lane < 6, 1.0, 0.0))
        k_aug = jnp.where(lane < 3, 1.0, jnp.where(lane < 6, -parts, 0.0))
        for src, gain, aug, dst in ((q_ref, qg_ref, q_aug, qa_ref), (k_ref, kg_ref, k_aug, ka_ref)):
            t = src[:, cols].astype(F32)
            ms = jnp.mean(t * t, axis=-1, keepdims=True)
            dst[:, 2 * h * HEAD_DIM:(2 * h + 1) * HEAD_DIM] = ((t * lax.rsqrt(ms + EPS)) * gain[...]).astype(BF16)
            dst[:, (2 * h + 1) * HEAD_DIM:(2 * h + 2) * HEAD_DIM] = aug.astype(BF16)
        vt_ref[h] = v_ref[:, cols].astype(F32).T.astype(BF16)


def _fox_prep_call(p, fl, b_f_pad, q_gain, k_gain, batch, ts):
    m = p.shape[0]
    d = p.shape[1] // 4
    n_heads = d // HEAD_DIM
    seq = m // batch
    nst = seq // ts
    lanes = fl.shape[1]
    return pl.pallas_call(
        _fox_prep_kernel,
        out_shape=(
            jax.ShapeDtypeStruct((m, 2 * d), BF16),
            jax.ShapeDtypeStruct((m, 2 * d), BF16),
            jax.ShapeDtypeStruct((batch, nst, n_heads, HEAD_DIM, ts), BF16),
        ),
        grid=(batch, nst),
        in_specs=[
            pl.BlockSpec((ts, d), lambda b, s: (b * nst + s, 0)),
            pl.BlockSpec((ts, d), lambda b, s: (b * nst + s, 1)),
            pl.BlockSpec((ts, d), lambda b, s: (b * nst + s, 2)),
            pl.BlockSpec((ts, lanes), lambda b, s: (b * nst + s, 0)),
            pl.BlockSpec((1, lanes), lambda b, s: (0, 0)),
            pl.BlockSpec((1, HEAD_DIM), lambda b, s: (0, 0)),
            pl.BlockSpec((1, HEAD_DIM), lambda b, s: (0, 0)),
        ],
        out_specs=(
            pl.BlockSpec((ts, 2 * d), lambda b, s: (b * nst + s, 0)),
            pl.BlockSpec((ts, 2 * d), lambda b, s: (b * nst + s, 0)),
            pl.BlockSpec((None, None, n_heads, HEAD_DIM, ts), lambda b, s: (b, s, 0, 0, 0)),
        ),
        scratch_shapes=[pltpu.VMEM((SUBLANES, lanes), F32)],
        compiler_params=_cparams(2),
        name="fox_prep",
    )(p, p, p, fl, b_f_pad, q_gain.reshape(1, HEAD_DIM), k_gain.reshape(1, HEAD_DIM))


def _fox_flash_kernel(q_ref, k_ref, vt_ref, g_ref, o_ref):
    t = q_ref.shape[0]
    _, hp, dh, _ = vt_ref.shape
    qi = pl.program_id(2)
    to_log2 = (1.0 / math.sqrt(dh)) * math.log2(math.e)
    nt = (((1,), (1,)), ((), ()))

    def scores(j, e):
        cols = slice(2 * e * dh, 2 * (e + 1) * dh)
        kj = k_ref[pl.ds(pl.multiple_of(j * t, t), t), cols]
        return lax.dot_general(kj, q_ref[:, cols], nt, preferred_element_type=F32)

    def step(j, carry, masked):
        raw = [scores(j, e) for e in range(hp)]
        out = []
        for e in range(hp):
            m_prev, l_prev, acc = carry[e]
            s = raw[e] * to_log2
            if masked:
                key = lax.broadcasted_iota(jnp.int32, (t, t), 0)
                qry = lax.broadcasted_iota(jnp.int32, (t, t), 1)
                s = jnp.where(key <= qry, s, -jnp.inf)
            m_new = jnp.maximum(m_prev, jnp.max(s, axis=0, keepdims=True))
            alpha = jnp.exp2(m_prev - m_new)
            p = jnp.exp2(s - m_new)
            l_new = alpha * l_prev + jnp.sum(p, axis=0, keepdims=True)
            acc = alpha * acc + jnp.dot(vt_ref[j, e], p.astype(BF16), preferred_element_type=F32)
            out.append((m_new, l_new, acc))
        return tuple(out)

    init = tuple((jnp.full((1, t), -jnp.inf, F32), jnp.zeros((1, t), F32), jnp.zeros((dh, t), F32))
                 for _ in range(hp))

    carry = lax.fori_loop(0, qi, functools.partial(step, masked=False), init)
    fin = step(qi, carry, True)
    for e in range(hp):
        _, l_fin, acc = fin[e]
        cols = slice(e * dh, (e + 1) * dh)
        o = (acc / l_fin).T
        o_ref[:, cols] = (o * jax.nn.sigmoid(g_ref[:, cols].astype(F32))).astype(BF16)


def _fox_flash_call(qa, ka, vt, p, batch, t, hp=4):
    m = qa.shape[0]
    n_heads = vt.shape[2]
    d = n_heads * HEAD_DIM
    seq = m // batch
    nq = seq // t
    hp = math.gcd(hp, n_heads)
    g_blk = 3 * n_heads // hp
    return pl.pallas_call(
        _fox_flash_kernel,
        out_shape=jax.ShapeDtypeStruct((m, d), BF16),
        grid=(batch, n_heads // hp, nq),
        in_specs=[
            pl.BlockSpec((t, 2 * hp * HEAD_DIM), lambda b, h, i: (b * nq + i, h)),
            pl.BlockSpec((seq, 2 * hp * HEAD_DIM), lambda b, h, i: (b, h)),
            pl.BlockSpec((None, nq, hp, HEAD_DIM, t), lambda b, h, i: (b, 0, h, 0, 0)),
            pl.BlockSpec((t, hp * HEAD_DIM), lambda b, h, i: (b * nq + i, g_blk + h)),
        ],
        out_specs=pl.BlockSpec((t, hp * HEAD_DIM), lambda b, h, i: (b * nq + i, h)),
        compiler_params=_cparams(3),
        name="fox_flash",
    )(qa, ka, vt, p)


def _hg_mid_kernel(q_ref, fz_ref, iv_ref, g_ref, lbl_ref, ng_ref, wout_ref, x_ref, mgate_ref, o_ref,
                   qin_scr, kin_scr, kdec_scr, dec_scr, st_scr, y_scr, *, layer):
    ts, d = q_ref.shape
    n_heads = d // HEAD_DIM
    n_chunks = ts // CHUNK

    @pl.when(pl.program_id(1) == 0)
    def _():
        st_scr[...] = jnp.zeros_like(st_scr)

    lg = lbl_ref[...]
    depth = lg.shape[0]
    mx = lg[0:1, :]
    for i in range(1, depth):
        mx = jnp.maximum(mx, lg[i:i + 1, :])
    es = [jnp.exp(lg[i:i + 1, :] - mx) for i in range(depth)]
    tot = es[0]
    for i in range(1, depth):
        tot = tot + es[i]
    lb = jnp.zeros_like(tot)
    for i in range(layer):
        lb = lb + es[i] / tot

    sig = jax.nn.sigmoid(fz_ref[...].astype(F32))
    logf = jnp.log(lb + (1.0 - lb) * sig)
    kk = (1.0 - lb) * (1.0 - sig)

    r = lax.broadcasted_iota(jnp.int32, (ts, ts), 0)
    c = lax.broadcasted_iota(jnp.int32, (ts, ts), 1)
    tril = jnp.where((c <= r) & (jnp.right_shift(c, CHUNK_SHIFT) == jnp.right_shift(r, CHUNK_SHIFT)), 1.0, 0.0).astype(BF16)
    bcum = _tril_dot(tril, logf)

    qin_scr[...] = (q_ref[...].astype(F32) * jnp.exp(bcum)).astype(BF16)
    kin_scr[...] = (kk * jnp.exp(-bcum)).astype(BF16)
    for ci in range(n_chunks):
        rows = slice(ci * CHUNK, (ci + 1) * CHUNK)
        b_last = bcum[(ci + 1) * CHUNK - 1:(ci + 1) * CHUNK, :]
        kdec_scr[rows, :] = (kk[rows, :] * jnp.exp(b_last - bcum[rows, :])).astype(BF16)
        dec_scr[ci] = jnp.broadcast_to(jnp.exp(b_last), (SUBLANES, d))

    rr = lax.broadcasted_iota(jnp.int32, (CHUNK, CHUNK), 0)
    cc = lax.broadcasted_iota(jnp.int32, (CHUNK, CHUNK), 1)
    causal = cc <= rr
    nt = (((1,), (1,)), ((), ()))
    tn = (((0,), (0,)), ((), ()))

    def chunk_body(ci, carry):
        r0 = pl.multiple_of(ci * CHUNK, CHUNK)
        rows = pl.ds(r0, CHUNK)
        for h in range(n_heads):
            cols = slice(h * HEAD_DIM, (h + 1) * HEAD_DIM)
            qh = qin_scr[rows, cols]
            kh = kin_scr[rows, cols]
            kd = kdec_scr[rows, cols]
            vh = iv_ref[rows, cols].astype(BF16)
            st = st_scr[h]
            sc = lax.dot_general(qh, kh, nt, preferred_element_type=F32)
            sc = jnp.where(causal, sc, 0.0).astype(BF16)
            o = jnp.dot(sc, vh, preferred_element_type=F32)
            o = o + lax.dot_general(qh, st.astype(BF16), nt, preferred_element_type=F32)
            u_t = lax.dot_general(vh, kd, tn, preferred_element_type=F32)
            st_scr[h] = st * dec_scr[ci][0:1, cols] + u_t
            ms = jnp.mean(o * o, axis=-1, keepdims=True)
            on = (o * lax.rsqrt(ms + EPS)) * ng_ref[...]
            gh = g_ref[rows, cols].astype(F32)
            y_scr[rows, cols] = (on * (gh * jax.nn.sigmoid(gh))).astype(BF16)
        return carry

    lax.fori_loop(0, n_chunks, chunk_body, 0)
    _project_residual(y_scr[...], wout_ref, x_ref, mgate_ref, o_ref)


def _hg_mid_call(p, lb_logits, norm_gain, w_out, widx, x, mod_r, layer, batch, ts=256):
    m, four_d = p.shape
    d = four_d // 4
    n_heads = d // HEAD_DIM
    seq = m // batch
    ts = min(ts, seq)
    nst = seq // ts
    return pl.pallas_call(
        functools.partial(_hg_mid_kernel, layer=layer),
        out_shape=jax.ShapeDtypeStruct((m, d), F32),
        grid=(batch, nst),
        in_specs=[
            pl.BlockSpec((ts, d), lambda b, s: (b * nst + s, 0)),
            pl.BlockSpec((ts, d), lambda b, s: (b * nst + s, 1)),
            pl.BlockSpec((ts, d), lambda b, s: (b * nst + s, 2)),
            pl.BlockSpec((ts, d), lambda b, s: (b * nst + s, 3)),
            pl.BlockSpec(lb_logits.shape, lambda b, s: (0, 0)),
            pl.BlockSpec((1, HEAD_DIM), lambda b, s: (0, 0)),
        ] + _res_specs(w_out, widx, ts, nst, d, layer * batch * 9 + 5),
        out_specs=pl.BlockSpec((ts, d), lambda b, s: (b * nst + s, 0)),
        scratch_shapes=[
            pltpu.VMEM((ts, d), BF16),
            pltpu.VMEM((ts, d), BF16),
            pltpu.VMEM((ts, d), BF16),
            pltpu.VMEM((ts // CHUNK, SUBLANES, d), F32),
            pltpu.VMEM((n_heads, HEAD_DIM, HEAD_DIM), F32),
            pltpu.VMEM((ts, d), BF16),
        ],
        compiler_params=_cparams(2),
        name="hgrn2_core",
    )(p, p, p, p, lb_logits, norm_gain.reshape(1, HEAD_DIM), w_out, x, mod_r)


def kernel(x, c, ada_w, ada_b, norm_g, ffn_w_gate, ffn_w_up, ffn_w_down, rg_w_in, rg_conv_w, rg_conv_b, rg_w_r, rg_b_r, rg_w_i, rg_b_i, rg_lam, rg_w_out, sc_w_in, sc_conv_w, sc_w_out, fox_w_in, fox_b_f, fox_q_gain, fox_k_gain, fox_w_out, hg_w_in, hg_lb_logits, hg_norm_gain, hg_w_out):
    batch, seq, d = x.shape
    depth = ada_w.shape[0]
    n_heads = d // HEAD_DIM
    n_mixers = 4
    mid_dtype = BF16

    mod = _ada_call(c, ada_w, ada_b)
    mod_r = mod.reshape(depth * batch * 9, 1, d)
    ng_r = norm_g.reshape(depth * 3, 1, d)

    wg, wu, wd = ffn_w_gate, ffn_w_up, ffn_w_down

    xf = x.reshape(batch * seq, d)
    for i in range(depth):
        xf = _ffn_call(xf, mod_r, ng_r, wg, wu, wd, i, 0, i, 0, batch)
        m, j = i % n_mixers, i // n_mixers
        if m == 0:
            p = _mm_mod_call(xf, mod_r, ng_r, rg_w_in, j, rg_w_in.shape[2], i, batch, mid_dtype)
            w_ri = jnp.concatenate([rg_w_r[j], rg_w_i[j]], axis=-1).astype(BF16)
            xf = _rg_mid_call(p, rg_conv_w[j], rg_conv_b[j], w_ri, rg_b_r[j], rg_b_i[j], rg_lam[j],
                              rg_w_out.astype(BF16), j, xf, mod_r, i, batch)
        elif m == 1:
            p = _mm_mod_call(xf, mod_r, ng_r, sc_w_in, j, sc_w_in.shape[2], i, batch, mid_dtype)
            xf = _sc_mid_call(p, sc_conv_w[j], sc_w_out.astype(BF16), j, xf, mod_r, i, batch)
        elif m == 2:
            w_in_t = jnp.swapaxes(fox_w_in, 1, 2)
            p = _mm_mod_call(xf, mod_r, ng_r, w_in_t, j, 4 * d, i, batch, mid_dtype, w_t=True)
            w_fl = jnp.pad(w_in_t[j, 4 * d:], ((0, HEAD_DIM - n_heads), (0, 0)))[None]
            fl = _mm_mod_call(xf, mod_r, ng_r, w_fl, 0, HEAD_DIM, i, batch, F32, w_t=True)
            b_f_pad = jnp.pad(fox_b_f[j], (0, HEAD_DIM - n_heads)).reshape(1, HEAD_DIM)
            t_attn = min(512, seq)
            qa, ka, vt = _fox_prep_call(p, fl, b_f_pad, fox_q_gain[j], fox_k_gain[j], batch, t_attn)
            y = _fox_flash_call(qa, ka, vt, p, batch, t_attn)
            xf = _mm_res_call(y, fox_w_out.astype(BF16), j, xf, mod_r, i, batch)
        else:
            p = _mm_mod_call(xf, mod_r, ng_r, hg_w_in, j, hg_w_in.shape[2], i, batch, mid_dtype)
            xf = _hg_mid_call(p, hg_lb_logits, hg_norm_gain[j], hg_w_out.astype(BF16), j, xf, mod_r, i, batch)
        if i == depth - 1:
            last = tuple(w[i:i + 1, 1:2].astype(BF16) for w in (wg, wu, wd))
            xf = _ffn_call(xf, mod_r, ng_r, *last, 0, 0, i, 2, batch, tf=512)
        else:
            xf = _ffn_call(xf, mod_r, ng_r, wg, wu, wd, i, 1, i, 2, batch)
    return xf.reshape(batch, seq, d)
```

```python
import functools
import math

import jax
import jax.numpy as jnp
from jax import lax
from jax.experimental import pallas as pl
from jax.experimental.pallas import tpu as pltpu

F32 = jnp.float32
BF16 = jnp.bfloat16

EPS = 1e-6
HEAD_DIM = 128
CHUNK = 64
CHUNK_SHIFT = 6
RG_C = 8.0
SUBLANES = 8
VMEM_LIMIT = 60 * 1024 * 1024


def _cparams(n_axes):
    return pltpu.CompilerParams(dimension_semantics=("arbitrary",) * n_axes,
                                vmem_limit_bytes=VMEM_LIMIT)


def _tile(n, preferred):
    if n <= preferred:
        return n
    t = preferred - preferred % HEAD_DIM
    while n % t:
        t -= HEAD_DIM
    return t


def _modulate(x, g, shift, scale):
    ms = jnp.mean(x * x, axis=-1, keepdims=True)
    return (x * lax.rsqrt(ms + EPS)) * (g * (1.0 + scale)) + shift


def _log_sigmoid(x):
    return jnp.minimum(x, 0.0) - jnp.log1p(jnp.exp(-jnp.abs(x)))


def _split3(x):
    hi = x.astype(BF16)
    r1 = x - hi.astype(F32)
    mid = r1.astype(BF16)
    lo = (r1 - mid.astype(F32)).astype(BF16)
    return hi, mid, lo


def _tril_dot(tril, x):
    hi, mid, lo = _split3(x)
    acc = jnp.dot(tril, hi, preferred_element_type=F32)
    acc = acc + jnp.dot(tril, mid, preferred_element_type=F32)
    return acc + jnp.dot(tril, lo, preferred_element_type=F32)


def _ada_kernel(c_ref, w_ref, b_ref, o_ref):
    c = c_ref[...]
    sc = (c * jax.nn.sigmoid(c)).astype(BF16)
    o_ref[...] = jnp.dot(sc, w_ref[...].astype(BF16), preferred_element_type=F32) + b_ref[...]


def _ada_call(c, ada_w, ada_b, tn=1024):
    depth, d, n = ada_w.shape
    b = c.shape[0]
    rows = -(-b // SUBLANES) * SUBLANES
    c_pad = jnp.pad(c, ((0, rows - b), (0, 0)))
    tn = _tile(n, tn)
    out = pl.pallas_call(
        _ada_kernel,
        out_shape=jax.ShapeDtypeStruct((depth, rows, n), F32),
        grid=(depth, n // tn),
        in_specs=[
            pl.BlockSpec((rows, d), lambda l, j: (0, 0)),
            pl.BlockSpec((None, d, tn), lambda l, j: (l, 0, j)),
            pl.BlockSpec((None, 1, tn), lambda l, j: (l, 0, j)),
        ],
        out_specs=pl.BlockSpec((None, rows, tn), lambda l, j: (l, 0, j)),
        compiler_params=_cparams(2),
        name="ada",
    )(c_pad, ada_w, ada_b.reshape(depth, 1, n))
    return out[:, :b]


def _row_chunks(tm, rc):
    return [slice(r, r + rc) for r in range(0, tm, rc)]


def _ffn_kernel(x_ref, shift_ref, scale_ref, gate_ref, g_ref, wg_ref, wu_ref, wd_ref, *rest, rc, nf, n_ahead):
    ahead_in = rest[:n_ahead]
    o_ref = rest[n_ahead]
    ahead_out = rest[n_ahead + 1:2 * n_ahead + 1]
    h_scr = rest[2 * n_ahead + 1]
    f = pl.program_id(1)
    chunks = _row_chunks(x_ref.shape[0], rc)

    def weights():
        for src, dst in zip(ahead_in, ahead_out):
            dst[...] = src[...].astype(BF16)
        return wg_ref[...], wu_ref[...], wd_ref[...]

    def swiglu(h, wg, wu, wd):
        gg = jnp.dot(h, wg, preferred_element_type=F32)
        uu = jnp.dot(h, wu, preferred_element_type=F32)
        a = ((gg * jax.nn.sigmoid(gg)) * uu).astype(BF16)
        return jnp.dot(a, wd, preferred_element_type=F32)

    def run(first, last):
        w = weights()
        half_gate = 0.5 * gate_ref[...] if last else None
        for rows in chunks:
            if first:
                h = _modulate(x_ref[rows, :], g_ref[...], shift_ref[...], scale_ref[...]).astype(BF16)
                h_scr[rows, :] = h
                acc = swiglu(h, *w)
            else:
                acc = o_ref[rows, :] + swiglu(h_scr[rows, :], *w)
            if last:
                acc = x_ref[rows, :] + half_gate * acc
            o_ref[rows, :] = acc

    if nf == 1:
        run(True, True)
    else:
        pl.when(f == 0)(functools.partial(run, True, False))
        pl.when((f > 0) & (f < nf - 1))(functools.partial(run, False, False))
        pl.when(f == nf - 1)(functools.partial(run, False, True))


def _mod_spec(d, base, tiles_per_batch, n_grid_axes, col_axis=None, tn=None):
    width = d if tn is None else tn

    def index(*ids):
        col = 0 if col_axis is None else ids[col_axis]
        return (base + 9 * (ids[0] // tiles_per_batch), 0, col)

    del n_grid_axes
    return pl.BlockSpec((None, 1, width), index)


def _ffn_call(x, mod_r, ng_r, w_bf16, layer, sub, batch, nxt=None, tm=1024, tf=512, rc=512):
    wg, wu, wd = w_bf16
    m, d = x.shape
    f_dim = wg.shape[-1]
    tm = min(tm, m // batch)
    rc = min(rc, tm)
    tf = _tile(f_dim, tf)
    tpb = (m // batch) // tm
    ni, nf = m // tm, f_dim // tf
    base = layer * batch * 9 + sub * 3
    in_specs = [
        pl.BlockSpec((tm, d), lambda i, f: (i, 0)),
        _mod_spec(d, base + 0, tpb, 2),
        _mod_spec(d, base + 1, tpb, 2),
        _mod_spec(d, base + 2, tpb, 2),
        pl.BlockSpec((None, 1, d), lambda i, f: (layer * 3 + sub, 0, 0)),
        pl.BlockSpec((d, tf), lambda i, f: (0, f)),
        pl.BlockSpec((d, tf), lambda i, f: (0, f)),
        pl.BlockSpec((tf, d), lambda i, f: (f, 0)),
    ]
    operands = [x, mod_r, mod_r, mod_r, ng_r, wg, wu, wd]
    out_shape = [jax.ShapeDtypeStruct((m, d), F32)]
    out_specs = [pl.BlockSpec((tm, d), lambda i, f: (i, 0))]
    if nxt is not None:
        (ng, nu, nd), nl, nw = nxt
        dr = d // ni
        in_specs += [
            pl.BlockSpec((None, None, dr, tf), lambda i, f: (nl, nw, i, f)),
            pl.BlockSpec((None, None, dr, tf), lambda i, f: (nl, nw, i, f)),
            pl.BlockSpec((None, None, tf, dr), lambda i, f: (nl, nw, f, i)),
        ]
        operands += [ng, nu, nd]
        out_shape += [jax.ShapeDtypeStruct((d, f_dim), BF16), jax.ShapeDtypeStruct((d, f_dim), BF16),
                      jax.ShapeDtypeStruct((f_dim, d), BF16)]
        out_specs += [pl.BlockSpec((dr, tf), lambda i, f: (i, f)), pl.BlockSpec((dr, tf), lambda i, f: (i, f)),
                      pl.BlockSpec((tf, dr), lambda i, f: (f, i))]
    outs = pl.pallas_call(
        functools.partial(_ffn_kernel, rc=rc, nf=nf, n_ahead=0 if nxt is None else 3),
        out_shape=out_shape,
        grid=(ni, nf),
        in_specs=in_specs,
        out_specs=out_specs,
        scratch_shapes=[pltpu.VMEM((tm, d), BF16)],
        compiler_params=_cparams(2),
        name="ffn",
    )(*operands)
    return outs[0], tuple(outs[1:])


def _mm_mod_kernel(x_ref, shift_ref, scale_ref, g_ref, w_ref, o_ref, h_scr, *, rc, w_t):
    j = pl.program_id(1)
    chunks = _row_chunks(x_ref.shape[0], rc)
    dims = (((1,), (1,)), ((), ())) if w_t else (((1,), (0,)), ((), ()))

    def project(h, w):
        return lax.dot_general(h, w, dims, preferred_element_type=F32).astype(o_ref.dtype)

    @pl.when(j == 0)
    def _():
        w = w_ref[...].astype(BF16)
        for rows in chunks:
            h = _modulate(x_ref[rows, :], g_ref[...], shift_ref[...], scale_ref[...]).astype(BF16)
            h_scr[rows, :] = h
            o_ref[rows, :] = project(h, w)

    @pl.when(j > 0)
    def _():
        w = w_ref[...].astype(BF16)
        for rows in chunks:
            o_ref[rows, :] = project(h_scr[rows, :], w)


def _mm_mod_call(x, mod_r, ng_r, w, widx, n_out, layer, batch, out_dtype, tm=1024, tn=1024, rc=512, w_t=False):
    m, d = x.shape
    tm = min(tm, m // batch)
    rc = min(rc, tm)
    tn = _tile(n_out, tn)
    tpb = (m // batch) // tm
    base = layer * batch * 9 + 3
    if w_t:
        w_spec = pl.BlockSpec((None, tn, d), lambda i, j: (widx, j, 0))
    else:
        w_spec = pl.BlockSpec((None, d, tn), lambda i, j: (widx, 0, j))
    return pl.pallas_call(
        functools.partial(_mm_mod_kernel, rc=rc, w_t=w_t),
        out_shape=jax.ShapeDtypeStruct((m, n_out), out_dtype),
        grid=(m // tm, n_out // tn),
        in_specs=[
            pl.BlockSpec((tm, d), lambda i, j: (i, 0)),
            _mod_spec(d, base + 0, tpb, 2),
            _mod_spec(d, base + 1, tpb, 2),
            pl.BlockSpec((None, 1, d), lambda i, j: (layer * 3 + 1, 0, 0)),
            w_spec,
        ],
        out_specs=pl.BlockSpec((tm, tn), lambda i, j: (i, j)),
        scratch_shapes=[pltpu.VMEM((tm, d), BF16)],
        compiler_params=_cparams(2),
        name="in_proj",
    )(x, mod_r, mod_r, ng_r, w)


def _mm_res_kernel(y_ref, w_ref, x_ref, gate_ref, o_ref):
    acc = jnp.dot(y_ref[...], w_ref[...], preferred_element_type=F32)
    o_ref[...] = x_ref[...] + gate_ref[...] * acc


def _mm_res_call(y, w, widx, x, mod_r, layer, batch, tm=512, tn=2048):
    m, k = y.shape
    d = x.shape[1]
    tm = min(tm, m // batch)
    tn = _tile(d, tn)
    tpb = (m // batch) // tm
    base = layer * batch * 9 + 3 + 2
    return pl.pallas_call(
        _mm_res_kernel,
        out_shape=jax.ShapeDtypeStruct((m, d), F32),
        grid=(m // tm, d // tn),
        in_specs=[
            pl.BlockSpec((tm, k), lambda i, j: (i, 0)),
            pl.BlockSpec((None, k, tn), lambda i, j: (widx, 0, j)),
            pl.BlockSpec((tm, tn), lambda i, j: (i, j)),
            _mod_spec(d, base, tpb, 2, col_axis=1, tn=tn),
        ],
        out_specs=pl.BlockSpec((tm, tn), lambda i, j: (i, j)),
        compiler_params=_cparams(2),
        name="out_proj",
    )(y, w, x, mod_r)


def _res_specs(w_out, widx, ts, nst, d, gate_base):
    k = w_out.shape[1]
    return [
        pl.BlockSpec((None, k, d), lambda b, s: (widx, 0, 0)),
        pl.BlockSpec((ts, d), lambda b, s: (b * nst + s, 0)),
        pl.BlockSpec((None, 1, d), lambda b, s: (gate_base + 9 * b, 0, 0)),
    ]


def _project_residual(y, wout_ref, x_ref, mgate_ref, o_ref):
    o_ref[...] = x_ref[...] + mgate_ref[...] * jnp.dot(y, wout_ref[...], preferred_element_type=F32)


def _shift_in_tile(ext_scr, cur, ts):
    s = pl.program_id(1)

    @pl.when(s == 0)
    def _():
        ext_scr[0:SUBLANES, :] = jnp.zeros((SUBLANES, ext_scr.shape[1]), F32)

    @pl.when(s > 0)
    def _():
        ext_scr[0:SUBLANES, :] = ext_scr[ts:ts + SUBLANES, :]

    ext_scr[SUBLANES:ts + SUBLANES, :] = cur


def _causal_conv_from_ext(ext_scr, cw_ref, ts):
    kw = cw_ref.shape[0]
    acc = None
    for k in range(kw):
        term = cw_ref[k:k + 1, :] * ext_scr[pl.ds(SUBLANES - (kw - 1) + k, ts), :]
        acc = term if acc is None else acc + term
    return acc


def _rg_mid_kernel(gate_ref, xb_ref, cw_ref, cb_ref, wri_ref, br_ref, bi_ref, lam_ref,
                   wout_ref, x_ref, mgate_ref, o_ref, ext_scr, a_scr, b_scr, hs_scr, h_scr):
    ts, width = xb_ref.shape
    n_blocks, blk, _ = wri_ref.shape

    @pl.when(pl.program_id(1) == 0)
    def _():
        h_scr[...] = jnp.zeros_like(h_scr)

    _shift_in_tile(ext_scr, xb_ref[...].astype(F32), ts)
    xc = _causal_conv_from_ext(ext_scr, cw_ref, ts) + cb_ref[...]

    for g in range(n_blocks):
        cols = slice(g * blk, (g + 1) * blk)
        xg = xc[:, cols]
        ri = jnp.dot(xg.astype(BF16), wri_ref[g], preferred_element_type=F32)
        r = jax.nn.sigmoid(ri[:, :blk] + br_ref[:, cols])
        ig = jax.nn.sigmoid(ri[:, blk:] + bi_ref[:, cols])
        log_a = (RG_C * r) * _log_sigmoid(lam_ref[:, cols])
        a_scr[:, cols] = jnp.exp(log_a)
        b_scr[:, cols] = jnp.sqrt(1.0 - jnp.exp(2.0 * log_a)) * (ig * xg)

    row = lax.broadcasted_iota(jnp.int32, (SUBLANES, width), 0)

    def body(i, h):
        r0 = pl.multiple_of(i * SUBLANES, SUBLANES)
        a = a_scr[pl.ds(r0, SUBLANES), :]
        b = b_scr[pl.ds(r0, SUBLANES), :]
        for sh in (1, 2, 4):
            keep = row >= sh
            a_prev = jnp.where(keep, pltpu.roll(a, sh, 0), 1.0)
            b_prev = jnp.where(keep, pltpu.roll(b, sh, 0), 0.0)
            b = a * b_prev + b
            a = a * a_prev
        hs = a * h + b
        hs_scr[pl.ds(r0, SUBLANES), :] = hs
        return jnp.broadcast_to(hs[SUBLANES - 1:SUBLANES, :], (SUBLANES, width))

    h_scr[...] = lax.fori_loop(0, ts // SUBLANES, body, h_scr[...])
    y = (hs_scr[...] * jax.nn.gelu(gate_ref[...].astype(F32))).astype(BF16)
    _project_residual(y, wout_ref, x_ref, mgate_ref, o_ref)


def _rg_mid_call(p, conv_w, conv_b, w_ri, b_r, b_i, lam, w_out, widx, x, mod_r, layer, batch, ts=256):
    m, two_w = p.shape
    width = two_w // 2
    d = x.shape[1]
    seq = m // batch
    ts = min(ts, seq)
    nst = seq // ts
    n_blocks, blk, _ = w_ri.shape
    full = lambda shape: pl.BlockSpec(shape, lambda b, s: (0,) * len(shape))
    return pl.pallas_call(
        _rg_mid_kernel,
        out_shape=jax.ShapeDtypeStruct((m, d), F32),
        grid=(batch, nst),
        in_specs=[
            pl.BlockSpec((ts, width), lambda b, s: (b * nst + s, 0)),
            pl.BlockSpec((ts, width), lambda b, s: (b * nst + s, 1)),
            full(conv_w.shape),
            full((1, width)),
            full((n_blocks, blk, 2 * blk)),
            full((1, width)),
            full((1, width)),
            full((1, width)),
        ] + _res_specs(w_out, widx, ts, nst, d, layer * batch * 9 + 5),
        out_specs=pl.BlockSpec((ts, d), lambda b, s: (b * nst + s, 0)),
        scratch_shapes=[
            pltpu.VMEM((ts + SUBLANES, width), F32),
            pltpu.VMEM((ts, width), F32),
            pltpu.VMEM((ts, width), F32),
            pltpu.VMEM((ts, width), F32),
            pltpu.VMEM((SUBLANES, width), F32),
        ],
        compiler_params=_cparams(2),
        name="rglru_core",
    )(p, p, conv_w, conv_b.reshape(1, width), w_ri, b_r.reshape(1, width), b_i.reshape(1, width),
      lam.reshape(1, width), w_out, x, mod_r)


def _sc_mid_kernel(bg_ref, cg_ref, xv_ref, cw_ref, wout_ref, x_ref, mgate_ref, o_ref, ext_scr):
    ts = bg_ref.shape[0]
    _shift_in_tile(ext_scr, cg_ref[...].astype(F32) * xv_ref[...].astype(F32), ts)
    conv = _causal_conv_from_ext(ext_scr, cw_ref, ts)
    y = (bg_ref[...].astype(F32) * conv).astype(BF16)
    _project_residual(y, wout_ref, x_ref, mgate_ref, o_ref)


def _sc_mid_call(p, conv_w, w_out, widx, x, mod_r, layer, batch, ts=256):
    m, three_d = p.shape
    d = three_d // 3
    seq = m // batch
    ts = min(ts, seq)
    nst = seq // ts
    return pl.pallas_call(
        _sc_mid_kernel,
        out_shape=jax.ShapeDtypeStruct((m, d), F32),
        grid=(batch, nst),
        in_specs=[
            pl.BlockSpec((ts, d), lambda b, s: (b * nst + s, 0)),
            pl.BlockSpec((ts, d), lambda b, s: (b * nst + s, 1)),
            pl.BlockSpec((ts, d), lambda b, s: (b * nst + s, 2)),
            pl.BlockSpec(conv_w.shape, lambda b, s: (0, 0)),
        ] + _res_specs(w_out, widx, ts, nst, d, layer * batch * 9 + 5),
        out_specs=pl.BlockSpec((ts, d), lambda b, s: (b * nst + s, 0)),
        scratch_shapes=[pltpu.VMEM((ts + SUBLANES, d), F32)],
        compiler_params=_cparams(2),
        name="shortconv_core",
    )(p, p, p, conv_w, w_out, x, mod_r)


def _fox_prep_kernel(q_ref, k_ref, v_ref, fl_ref, bf_ref, qg_ref, kg_ref, qa_ref, ka_ref, vt_ref, carry_scr):
    ts, d = q_ref.shape

    @pl.when(pl.program_id(1) == 0)
    def _():
        carry_scr[...] = jnp.zeros_like(carry_scr)

    logf = _log_sigmoid(fl_ref[...] + bf_ref[...])
    r = lax.broadcasted_iota(jnp.int32, (ts, ts), 0)
    c = lax.broadcasted_iota(jnp.int32, (ts, ts), 1)
    tril = jnp.where(c <= r, 1.0, 0.0).astype(BF16)
    cum = _tril_dot(tril, logf) + carry_scr[0:1, :]
    carry_scr[...] = jnp.broadcast_to(cum[ts - 1:ts, :], carry_scr.shape)
    f_over_scale = cum * math.sqrt(HEAD_DIM)

    lane = lax.broadcasted_iota(jnp.int32, (ts, HEAD_DIM), 1)
    for h in range(d // HEAD_DIM):
        cols = slice(h * HEAD_DIM, (h + 1) * HEAD_DIM)
        hi, mid, lo = (part.astype(F32) for part in _split3(f_over_scale[:, h:h + 1]))
        parts = jnp.where((lane == 0) | (lane == 3), hi, jnp.where((lane == 1) | (lane == 4), mid, lo))
        q_aug = jnp.where(lane < 3, parts, jnp.where(lane < 6, 1.0, 0.0))
        k_aug = jnp.where(lane < 3, 1.0, jnp.where(lane < 6, -parts, 0.0))
        for src, gain, aug, dst in ((q_ref, qg_ref, q_aug, qa_ref), (k_ref, kg_ref, k_aug, ka_ref)):
            t = src[:, cols].astype(F32)
            ms = jnp.mean(t * t, axis=-1, keepdims=True)
            dst[:, 2 * h * HEAD_DIM:(2 * h + 1) * HEAD_DIM] = ((t * lax.rsqrt(ms + EPS)) * gain[...]).astype(BF16)
            dst[:, (2 * h + 1) * HEAD_DIM:(2 * h + 2) * HEAD_DIM] = aug.astype(BF16)
        vt_ref[h] = v_ref[:, cols].astype(F32).T.astype(BF16)


def _fox_prep_call(p, fl, b_f_pad, q_gain, k_gain, batch, ts):
    m = p.shape[0]
    d = p.shape[1] // 4
    n_heads = d // HEAD_DIM
    seq = m // batch
    nst = seq // ts
    lanes = fl.shape[1]
    return pl.pallas_call(
        _fox_prep_kernel,
        out_shape=(
            jax.ShapeDtypeStruct((m, 2 * d), BF16),
            jax.ShapeDtypeStruct((m, 2 * d), BF16),
            jax.ShapeDtypeStruct((batch, nst, n_heads, HEAD_DIM, ts), BF16),
        ),
        grid=(batch, nst),
        in_specs=[
            pl.BlockSpec((ts, d), lambda b, s: (b * nst + s, 0)),
            pl.BlockSpec((ts, d), lambda b, s: (b * nst + s, 1)),
            pl.BlockSpec((ts, d), lambda b, s: (b * nst + s, 2)),
            pl.BlockSpec((ts, lanes), lambda b, s: (b * nst + s, 0)),
            pl.BlockSpec((1, lanes), lambda b, s: (0, 0)),
            pl.BlockSpec((1, HEAD_DIM), lambda b, s: (0, 0)),
            pl.BlockSpec((1, HEAD_DIM), lambda b, s: (0, 0)),
        ],
        out_specs=(
            pl.BlockSpec((ts, 2 * d), lambda b, s: (b * nst + s, 0)),
            pl.BlockSpec((ts, 2 * d), lambda b, s: (b * nst + s, 0)),
            pl.BlockSpec((None, None, n_heads, HEAD_DIM, ts), lambda b, s: (b, s, 0, 0, 0)),
        ),
        scratch_shapes=[pltpu.VMEM((SUBLANES, lanes), F32)],
        compiler_params=_cparams(2),
        name="fox_prep",
    )(p, p, p, fl, b_f_pad, q_gain.reshape(1, HEAD_DIM), k_gain.reshape(1, HEAD_DIM))


def _fox_flash_kernel(q_ref, k_ref, vt_ref, g_ref, o_ref):
    t = q_ref.shape[0]
    _, hp, dh, _ = vt_ref.shape
    qi = pl.program_id(2)
    to_log2 = (1.0 / math.sqrt(dh)) * math.log2(math.e)
    nt = (((1,), (1,)), ((), ()))

    def scores(j, e):
        cols = slice(2 * e * dh, 2 * (e + 1) * dh)
        kj = k_ref[pl.ds(pl.multiple_of(j * t, t), t), cols]
        return lax.dot_general(kj, q_ref[:, cols], nt, preferred_element_type=F32)

    def step(j, carry, masked):
        raw = [scores(j, e) for e in range(hp)]
        out = []
        for e in range(hp):
            m_prev, l_prev, acc = carry[e]
            s = raw[e] * to_log2
            if masked:
                key = lax.broadcasted_iota(jnp.int32, (t, t), 0)
                qry = lax.broadcasted_iota(jnp.int32, (t, t), 1)
                s = jnp.where(key <= qry, s, -jnp.inf)
            m_new = jnp.maximum(m_prev, jnp.max(s, axis=0, keepdims=True))
            alpha = jnp.exp2(m_prev - m_new)
            p = jnp.exp2(s - m_new)
            l_new = alpha * l_prev + jnp.sum(p, axis=0, keepdims=True)
            acc = alpha * acc + jnp.dot(vt_ref[j, e], p.astype(BF16), preferred_element_type=F32)
            out.append((m_new, l_new, acc))
        return tuple(out)

    init = tuple((jnp.full((1, t), -jnp.inf, F32), jnp.zeros((1, t), F32), jnp.zeros((dh, t), F32))
                 for _ in range(hp))

    carry = lax.fori_loop(0, qi, functools.partial(step, masked=False), init)
    fin = step(qi, carry, True)
    for e in range(hp):
        _, l_fin, acc = fin[e]
        cols = slice(e * dh, (e + 1) * dh)
        o = (acc / l_fin).T
        o_ref[:, cols] = (o * jax.nn.sigmoid(g_ref[:, cols].astype(F32))).astype(BF16)


def _fox_flash_call(qa, ka, vt, p, batch, t, hp=4):
    m = qa.shape[0]
    n_heads = vt.shape[2]
    d = n_heads * HEAD_DIM
    seq = m // batch
    nq = seq // t
    hp = math.gcd(hp, n_heads)
    g_blk = 3 * n_heads // hp
    return pl.pallas_call(
        _fox_flash_kernel,
        out_shape=jax.ShapeDtypeStruct((m, d), BF16),
        grid=(batch, n_heads // hp, nq),
        in_specs=[
            pl.BlockSpec((t, 2 * hp * HEAD_DIM), lambda b, h, i: (b * nq + i, h)),
            pl.BlockSpec((seq, 2 * hp * HEAD_DIM), lambda b, h, i: (b, h)),
            pl.BlockSpec((None, nq, hp, HEAD_DIM, t), lambda b, h, i: (b, 0, h, 0, 0)),
            pl.BlockSpec((t, hp * HEAD_DIM), lambda b, h, i: (b * nq + i, g_blk + h)),
        ],
        out_specs=pl.BlockSpec((t, hp * HEAD_DIM), lambda b, h, i: (b * nq + i, h)),
        compiler_params=_cparams(3),
        name="fox_flash",
    )(qa, ka, vt, p)


def _hg_mid_kernel(q_ref, fz_ref, iv_ref, g_ref, lbl_ref, ng_ref, wout_ref, x_ref, mgate_ref, o_ref,
                   qin_scr, kin_scr, kdec_scr, dec_scr, st_scr, y_scr, *, layer):
    ts, d = q_ref.shape
    n_heads = d // HEAD_DIM
    n_chunks = ts // CHUNK

    @pl.when(pl.program_id(1) == 0)
    def _():
        st_scr[...] = jnp.zeros_like(st_scr)

    lg = lbl_ref[...]
    depth = lg.shape[0]
    mx = lg[0:1, :]
    for i in range(1, depth):
        mx = jnp.maximum(mx, lg[i:i + 1, :])
    es = [jnp.exp(lg[i:i + 1, :] - mx) for i in range(depth)]
    tot = es[0]
    for i in range(1, depth):
        tot = tot + es[i]
    lb = jnp.zeros_like(tot)
    for i in range(layer):
        lb = lb + es[i] / tot

    sig = jax.nn.sigmoid(fz_ref[...].astype(F32))
    logf = jnp.log(lb + (1.0 - lb) * sig)
    kk = (1.0 - lb) * (1.0 - sig)

    r = lax.broadcasted_iota(jnp.int32, (ts, ts), 0)
    c = lax.broadcasted_iota(jnp.int32, (ts, ts), 1)
    tril = jnp.where((c <= r) & (jnp.right_shift(c, CHUNK_SHIFT) == jnp.right_shift(r, CHUNK_SHIFT)), 1.0, 0.0).astype(BF16)
    bcum = _tril_dot(tril, logf)

    qin_scr[...] = (q_ref[...].astype(F32) * jnp.exp(bcum)).astype(BF16)
    kin_scr[...] = (kk * jnp.exp(-bcum)).astype(BF16)
    for ci in range(n_chunks):
        rows = slice(ci * CHUNK, (ci + 1) * CHUNK)
        b_last = bcum[(ci + 1) * CHUNK - 1:(ci + 1) * CHUNK, :]
        kdec_scr[rows, :] = (kk[rows, :] * jnp.exp(b_last - bcum[rows, :])).astype(BF16)
        dec_scr[ci] = jnp.broadcast_to(jnp.exp(b_last), (SUBLANES, d))

    rr = lax.broadcasted_iota(jnp.int32, (CHUNK, CHUNK), 0)
    cc = lax.broadcasted_iota(jnp.int32, (CHUNK, CHUNK), 1)
    causal = cc <= rr
    nt = (((1,), (1,)), ((), ()))
    tn = (((0,), (0,)), ((), ()))

    def chunk_body(ci, carry):
        r0 = pl.multiple_of(ci * CHUNK, CHUNK)
        rows = pl.ds(r0, CHUNK)
        heads = [slice(h * HEAD_DIM, (h + 1) * HEAD_DIM) for h in range(n_heads)]
        qh = [qin_scr[rows, c] for c in heads]
        vh = [iv_ref[rows, c].astype(BF16) for c in heads]
        st = [st_scr[h] for h in range(n_heads)]
        sc = [lax.dot_general(qh[h], kin_scr[rows, c], nt, preferred_element_type=F32)
              for h, c in enumerate(heads)]
        o_inter = [lax.dot_general(qh[h], st[h].astype(BF16), nt, preferred_element_type=F32)
                   for h in range(n_heads)]
        u_t = [lax.dot_general(vh[h], kdec_scr[rows, c], tn, preferred_element_type=F32)
               for h, c in enumerate(heads)]
        for h, c in enumerate(heads):
            scm = jnp.where(causal, sc[h], 0.0).astype(BF16)
            o = jnp.dot(scm, vh[h], preferred_element_type=F32) + o_inter[h]
            st_scr[h] = st[h] * dec_scr[ci][0:1, c] + u_t[h]
            ms = jnp.mean(o * o, axis=-1, keepdims=True)
            on = (o * lax.rsqrt(ms + EPS)) * ng_ref[...]
            gh = g_ref[rows, c].astype(F32)
            y_scr[rows, c] = (on * (gh * jax.nn.sigmoid(gh))).astype(BF16)
        return carry

    lax.fori_loop(0, n_chunks, chunk_body, 0)
    _project_residual(y_scr[...], wout_ref, x_ref, mgate_ref, o_ref)


def _hg_mid_call(p, lb_logits, norm_gain, w_out, widx, x, mod_r, layer, batch, ts=256):
    m, four_d = p.shape
    d = four_d // 4
    n_heads = d // HEAD_DIM
    seq = m // batch
    ts = min(ts, seq)
    nst = seq // ts
    return pl.pallas_call(
        functools.partial(_hg_mid_kernel, layer=layer),
        out_shape=jax.ShapeDtypeStruct((m, d), F32),
        grid=(batch, nst),
        in_specs=[
            pl.BlockSpec((ts, d), lambda b, s: (b * nst + s, 0)),
            pl.BlockSpec((ts, d), lambda b, s: (b * nst + s, 1)),
            pl.BlockSpec((ts, d), lambda b, s: (b * nst + s, 2)),
            pl.BlockSpec((ts, d), lambda b, s: (b * nst + s, 3)),
            pl.BlockSpec(lb_logits.shape, lambda b, s: (0, 0)),
            pl.BlockSpec((1, HEAD_DIM), lambda b, s: (0, 0)),
        ] + _res_specs(w_out, widx, ts, nst, d, layer * batch * 9 + 5),
        out_specs=pl.BlockSpec((ts, d), lambda b, s: (b * nst + s, 0)),
        scratch_shapes=[
            pltpu.VMEM((ts, d), BF16),
            pltpu.VMEM((ts, d), BF16),
            pltpu.VMEM((ts, d), BF16),
            pltpu.VMEM((ts // CHUNK, SUBLANES, d), F32),
            pltpu.VMEM((n_heads, HEAD_DIM, HEAD_DIM), F32),
            pltpu.VMEM((ts, d), BF16),
        ],
        compiler_params=_cparams(2),
        name="hgrn2_core",
    )(p, p, p, p, lb_logits, norm_gain.reshape(1, HEAD_DIM), w_out, x, mod_r)


def kernel(x, c, ada_w, ada_b, norm_g, ffn_w_gate, ffn_w_up, ffn_w_down, rg_w_in, rg_conv_w, rg_conv_b, rg_w_r, rg_b_r, rg_w_i, rg_b_i, rg_lam, rg_w_out, sc_w_in, sc_conv_w, sc_w_out, fox_w_in, fox_b_f, fox_q_gain, fox_k_gain, fox_w_out, hg_w_in, hg_lb_logits, hg_norm_gain, hg_w_out):
    batch, seq, d = x.shape
    depth = ada_w.shape[0]
    n_heads = d // HEAD_DIM
    n_mixers = 4
    mid_dtype = BF16

    mod = _ada_call(c, ada_w, ada_b)
    mod_r = mod.reshape(depth * batch * 9, 1, d)
    ng_r = norm_g.reshape(depth * 3, 1, d)

    ffn_w = (ffn_w_gate, ffn_w_up, ffn_w_down)
    w_cur = tuple(w[0, 0].astype(BF16) for w in ffn_w)

    def ffn(xf, w_cur, k, sub):
        nxt = (ffn_w, (k + 1) // 2, (k + 1) % 2) if k + 1 < 2 * depth else None
        return _ffn_call(xf, mod_r, ng_r, w_cur, k // 2, sub, batch, nxt)

    xf = x.reshape(batch * seq, d)
    for i in range(depth):
        xf, w_cur = ffn(xf, w_cur, 2 * i, 0)
        m, j = i % n_mixers, i // n_mixers
        if m == 0:
            p = _mm_mod_call(xf, mod_r, ng_r, rg_w_in, j, rg_w_in.shape[2], i, batch, mid_dtype)
            w_ri = jnp.concatenate([rg_w_r[j], rg_w_i[j]], axis=-1).astype(BF16)
            xf = _rg_mid_call(p, rg_conv_w[j], rg_conv_b[j], w_ri, rg_b_r[j], rg_b_i[j], rg_lam[j],
                              rg_w_out.astype(BF16), j, xf, mod_r, i, batch)
        elif m == 1:
            p = _mm_mod_call(xf, mod_r, ng_r, sc_w_in, j, sc_w_in.shape[2], i, batch, mid_dtype)
            xf = _sc_mid_call(p, sc_conv_w[j], sc_w_out.astype(BF16), j, xf, mod_r, i, batch)
        elif m == 2:
            w_in_t = jnp.swapaxes(fox_w_in, 1, 2)
            p = _mm_mod_call(xf, mod_r, ng_r, w_in_t, j, 4 * d, i, batch, mid_dtype, w_t=True)
            w_fl = jnp.pad(w_in_t[j, 4 * d:], ((0, HEAD_DIM - n_heads), (0, 0)))[None]
            fl = _mm_mod_call(xf, mod_r, ng_r, w_fl, 0, HEAD_DIM, i, batch, F32, w_t=True)
            b_f_pad = jnp.pad(fox_b_f[j], (0, HEAD_DIM - n_heads)).reshape(1, HEAD_DIM)
            t_attn = min(512, seq)
            qa, ka, vt = _fox_prep_call(p, fl, b_f_pad, fox_q_gain[j], fox_k_gain[j], batch, t_attn)
            y = _fox_flash_call(qa, ka, vt, p, batch, t_attn)
            xf = _mm_res_call(y, fox_w_out.astype(BF16), j, xf, mod_r, i, batch)
        else:
            p = _mm_mod_call(xf, mod_r, ng_r, hg_w_in, j, hg_w_in.shape[2], i, batch, mid_dtype)
            xf = _hg_mid_call(p, hg_lb_logits, hg_norm_gain[j], hg_w_out.astype(BF16), j, xf, mod_r, i, batch)
        xf, w_cur = ffn(xf, w_cur, 2 * i + 1, 2)
    return xf.reshape(batch, seq, d)
```

```python
import functools
import math

import jax
import jax.numpy as jnp
from jax import lax
from jax.experimental import pallas as pl
from jax.experimental.pallas import tpu as pltpu

F32 = jnp.float32
BF16 = jnp.bfloat16

EPS = 1e-6
HEAD_DIM = 128
CHUNK = 64
CHUNK_SHIFT = 6
RG_C = 8.0
SUBLANES = 8
VMEM_LIMIT = 60 * 1024 * 1024


def _cparams(n_axes):
    return pltpu.CompilerParams(dimension_semantics=("arbitrary",) * n_axes,
                                vmem_limit_bytes=VMEM_LIMIT)


def _tile(n, preferred):
    if n <= preferred:
        return n
    t = preferred - preferred % HEAD_DIM
    while n % t:
        t -= HEAD_DIM
    return t


def _modulate(x, g, shift, scale):
    ms = jnp.mean(x * x, axis=-1, keepdims=True)
    return (x * lax.rsqrt(ms + EPS)) * (g * (1.0 + scale)) + shift


def _log_sigmoid(x):
    return jnp.minimum(x, 0.0) - jnp.log1p(jnp.exp(-jnp.abs(x)))


def _split3(x):
    hi = x.astype(BF16)
    r1 = x - hi.astype(F32)
    mid = r1.astype(BF16)
    lo = (r1 - mid.astype(F32)).astype(BF16)
    return hi, mid, lo


def _tril_dot(tril, x):
    hi, mid, lo = _split3(x)
    acc = jnp.dot(tril, hi, preferred_element_type=F32)
    acc = acc + jnp.dot(tril, mid, preferred_element_type=F32)
    return acc + jnp.dot(tril, lo, preferred_element_type=F32)


def _ada_kernel(c_ref, w_ref, b_ref, o_ref):
    c = c_ref[...]
    sc = (c * jax.nn.sigmoid(c)).astype(BF16)
    o_ref[...] = jnp.dot(sc, w_ref[...].astype(BF16), preferred_element_type=F32) + b_ref[...]


def _ada_call(c, ada_w, ada_b, tn=1024):
    depth, d, n = ada_w.shape
    b = c.shape[0]
    rows = -(-b // SUBLANES) * SUBLANES
    c_pad = jnp.pad(c, ((0, rows - b), (0, 0)))
    tn = _tile(n, tn)
    out = pl.pallas_call(
        _ada_kernel,
        out_shape=jax.ShapeDtypeStruct((depth, rows, n), F32),
        grid=(depth, n // tn),
        in_specs=[
            pl.BlockSpec((rows, d), lambda l, j: (0, 0)),
            pl.BlockSpec((None, d, tn), lambda l, j: (l, 0, j)),
            pl.BlockSpec((None, 1, tn), lambda l, j: (l, 0, j)),
        ],
        out_specs=pl.BlockSpec((None, rows, tn), lambda l, j: (l, 0, j)),
        compiler_params=_cparams(2),
        name="ada",
    )(c_pad, ada_w, ada_b.reshape(depth, 1, n))
    return out[:, :b]


def _row_chunks(tm, rc):
    return [slice(r, r + rc) for r in range(0, tm, rc)]


def _ffn_kernel(x_ref, shift_ref, scale_ref, gate_ref, g_ref, wg_ref, wu_ref, wd_ref, *rest, rc, nf, n_ahead):
    ahead_in = rest[:n_ahead]
    o_ref = rest[n_ahead]
    ahead_out = rest[n_ahead + 1:2 * n_ahead + 1]
    h_scr = rest[2 * n_ahead + 1]
    f = pl.program_id(1)
    chunks = _row_chunks(x_ref.shape[0], rc)

    def weights():
        for src, dst in zip(ahead_in, ahead_out):
            dst[...] = src[...].astype(BF16)
        return wg_ref[...].astype(BF16), wu_ref[...].astype(BF16), wd_ref[...].astype(BF16)

    def swiglu(h, wg, wu, wd):
        gg = jnp.dot(h, wg, preferred_element_type=F32)
        uu = jnp.dot(h, wu, preferred_element_type=F32)
        a = ((gg * jax.nn.sigmoid(gg)) * uu).astype(BF16)
        return jnp.dot(a, wd, preferred_element_type=F32)

    def run(first, last):
        w = weights()
        half_gate = 0.5 * gate_ref[...] if last else None
        for rows in chunks:
            if first:
                h = _modulate(x_ref[rows, :], g_ref[...], shift_ref[...], scale_ref[...]).astype(BF16)
                h_scr[rows, :] = h
                acc = swiglu(h, *w)
            else:
                acc = o_ref[rows, :] + swiglu(h_scr[rows, :], *w)
            if last:
                acc = x_ref[rows, :] + half_gate * acc
            o_ref[rows, :] = acc

    if nf == 1:
        run(True, True)
    else:
        pl.when(f == 0)(functools.partial(run, True, False))
        pl.when((f > 0) & (f < nf - 1))(functools.partial(run, False, False))
        pl.when(f == nf - 1)(functools.partial(run, False, True))


def _mod_spec(d, base, tiles_per_batch, n_grid_axes, col_axis=None, tn=None):
    width = d if tn is None else tn

    def index(*ids):
        col = 0 if col_axis is None else ids[col_axis]
        return (base + 9 * (ids[0] // tiles_per_batch), 0, col)

    del n_grid_axes
    return pl.BlockSpec((None, 1, width), index)


def _ffn_call(x, mod_r, ng_r, w, layer, sub, batch, nxt=None, w_index=None, tm=1024, tf=512, rc=512):
    wg, wu, wd = w
    m, d = x.shape
    f_dim = wg.shape[-1]
    tm = min(tm, m // batch)
    rc = min(rc, tm)
    tf = _tile(f_dim, tf)
    tpb = (m // batch) // tm
    ni, nf = m // tm, f_dim // tf
    base = layer * batch * 9 + sub * 3
    if w_index is None:
        w_specs = [pl.BlockSpec((d, tf), lambda i, f: (0, f)), pl.BlockSpec((d, tf), lambda i, f: (0, f)),
                   pl.BlockSpec((tf, d), lambda i, f: (f, 0))]
    else:
        wl, ww = w_index
        w_specs = [pl.BlockSpec((None, None, d, tf), lambda i, f: (wl, ww, 0, f)),
                   pl.BlockSpec((None, None, d, tf), lambda i, f: (wl, ww, 0, f)),
                   pl.BlockSpec((None, None, tf, d), lambda i, f: (wl, ww, f, 0))]
    in_specs = [
        pl.BlockSpec((tm, d), lambda i, f: (i, 0)),
        _mod_spec(d, base + 0, tpb, 2),
        _mod_spec(d, base + 1, tpb, 2),
        _mod_spec(d, base + 2, tpb, 2),
        pl.BlockSpec((None, 1, d), lambda i, f: (layer * 3 + sub, 0, 0)),
    ] + w_specs
    operands = [x, mod_r, mod_r, mod_r, ng_r, wg, wu, wd]
    out_shape = [jax.ShapeDtypeStruct((m, d), F32)]
    out_specs = [pl.BlockSpec((tm, d), lambda i, f: (i, 0))]
    if nxt is not None:
        (ng, nu, nd), nl, nw = nxt
        dr = d // ni
        in_specs += [
            pl.BlockSpec((None, None, dr, tf), lambda i, f: (nl, nw, i, f)),
            pl.BlockSpec((None, None, dr, tf), lambda i, f: (nl, nw, i, f)),
            pl.BlockSpec((None, None, tf, dr), lambda i, f: (nl, nw, f, i)),
        ]
        operands += [ng, nu, nd]
        out_shape += [jax.ShapeDtypeStruct((d, f_dim), BF16), jax.ShapeDtypeStruct((d, f_dim), BF16),
                      jax.ShapeDtypeStruct((f_dim, d), BF16)]
        out_specs += [pl.BlockSpec((dr, tf), lambda i, f: (i, f)), pl.BlockSpec((dr, tf), lambda i, f: (i, f)),
                      pl.BlockSpec((tf, dr), lambda i, f: (f, i))]
    outs = pl.pallas_call(
        functools.partial(_ffn_kernel, rc=rc, nf=nf, n_ahead=0 if nxt is None else 3),
        out_shape=out_shape,
        grid=(ni, nf),
        in_specs=in_specs,
        out_specs=out_specs,
        scratch_shapes=[pltpu.VMEM((tm, d), BF16)],
        compiler_params=_cparams(2),
        name="ffn",
    )(*operands)
    return outs[0], tuple(outs[1:])


def _mm_mod_kernel(x_ref, shift_ref, scale_ref, g_ref, w_ref, o_ref, h_scr, *, rc, w_t):
    j = pl.program_id(1)
    chunks = _row_chunks(x_ref.shape[0], rc)
    dims = (((1,), (1,)), ((), ())) if w_t else (((1,), (0,)), ((), ()))

    def project(h, w):
        return lax.dot_general(h, w, dims, preferred_element_type=F32).astype(o_ref.dtype)

    @pl.when(j == 0)
    def _():
        w = w_ref[...].astype(BF16)
        for rows in chunks:
            h = _modulate(x_ref[rows, :], g_ref[...], shift_ref[...], scale_ref[...]).astype(BF16)
            h_scr[rows, :] = h
            o_ref[rows, :] = project(h, w)

    @pl.when(j > 0)
    def _():
        w = w_ref[...].astype(BF16)
        for rows in chunks:
            o_ref[rows, :] = project(h_scr[rows, :], w)


def _mm_mod_call(x, mod_r, ng_r, w, widx, n_out, layer, batch, out_dtype, tm=2048, tn=512, rc=512, w_t=False):
    m, d = x.shape
    tm = min(tm, m // batch)
    rc = min(rc, tm)
    tn = _tile(n_out, tn)
    tpb = (m // batch) // tm
    base = layer * batch * 9 + 3
    if w_t:
        w_spec = pl.BlockSpec((None, tn, d), lambda i, j: (widx, j, 0))
    else:
        w_spec = pl.BlockSpec((None, d, tn), lambda i, j: (widx, 0, j))
    return pl.pallas_call(
        functools.partial(_mm_mod_kernel, rc=rc, w_t=w_t),
        out_shape=jax.ShapeDtypeStruct((m, n_out), out_dtype),
        grid=(m // tm, n_out // tn),
        in_specs=[
            pl.BlockSpec((tm, d), lambda i, j: (i, 0)),
            _mod_spec(d, base + 0, tpb, 2),
            _mod_spec(d, base + 1, tpb, 2),
            pl.BlockSpec((None, 1, d), lambda i, j: (layer * 3 + 1, 0, 0)),
            w_spec,
        ],
        out_specs=pl.BlockSpec((tm, tn), lambda i, j: (i, j)),
        scratch_shapes=[pltpu.VMEM((tm, d), BF16)],
        compiler_params=_cparams(2),
        name="in_proj",
    )(x, mod_r, mod_r, ng_r, w)


def _mm_res_kernel(y_ref, w_ref, x_ref, gate_ref, o_ref):
    acc = jnp.dot(y_ref[...], w_ref[...], preferred_element_type=F32)
    o_ref[...] = x_ref[...] + gate_ref[...] * acc


def _mm_res_call(y, w, widx, x, mod_r, layer, batch, tm=512, tn=2048):
    m, k = y.shape
    d = x.shape[1]
    tm = min(tm, m // batch)
    tn = _tile(d, tn)
    tpb = (m // batch) // tm
    base = layer * batch * 9 + 3 + 2
    return pl.pallas_call(
        _mm_res_kernel,
        out_shape=jax.ShapeDtypeStruct((m, d), F32),
        grid=(m // tm, d // tn),
        in_specs=[
            pl.BlockSpec((tm, k), lambda i, j: (i, 0)),
            pl.BlockSpec((None, k, tn), lambda i, j: (widx, 0, j)),
            pl.BlockSpec((tm, tn), lambda i, j: (i, j)),
            _mod_spec(d, base, tpb, 2, col_axis=1, tn=tn),
        ],
        out_specs=pl.BlockSpec((tm, tn), lambda i, j: (i, j)),
        compiler_params=_cparams(2),
        name="out_proj",
    )(y, w, x, mod_r)


def _res_specs(w_out, widx, ts, nst, d, gate_base):
    k = w_out.shape[1]
    return [
        pl.BlockSpec((None, k, d), lambda b, s: (widx, 0, 0)),
        pl.BlockSpec((ts, d), lambda b, s: (b * nst + s, 0)),
        pl.BlockSpec((None, 1, d), lambda b, s: (gate_base + 9 * b, 0, 0)),
    ]


def _project_residual(y, wout_ref, x_ref, mgate_ref, o_ref):
    o_ref[...] = x_ref[...] + mgate_ref[...] * jnp.dot(y, wout_ref[...], preferred_element_type=F32)


def _shift_in_tile(ext_scr, cur, ts):
    s = pl.program_id(1)

    @pl.when(s == 0)
    def _():
        ext_scr[0:SUBLANES, :] = jnp.zeros((SUBLANES, ext_scr.shape[1]), F32)

    @pl.when(s > 0)
    def _():
        ext_scr[0:SUBLANES, :] = ext_scr[ts:ts + SUBLANES, :]

    ext_scr[SUBLANES:ts + SUBLANES, :] = cur


def _causal_conv_from_ext(ext_scr, cw_ref, ts):
    kw = cw_ref.shape[0]
    acc = None
    for k in range(kw):
        term = cw_ref[k:k + 1, :] * ext_scr[pl.ds(SUBLANES - (kw - 1) + k, ts), :]
        acc = term if acc is None else acc + term
    return acc


def _rg_mid_kernel(gate_ref, xb_ref, cw_ref, cb_ref, wri_ref, br_ref, bi_ref, lam_ref,
                   wout_ref, x_ref, mgate_ref, o_ref, ext_scr, a_scr, b_scr, hs_scr, h_scr):
    ts, width = xb_ref.shape
    n_blocks, blk, _ = wri_ref.shape

    @pl.when(pl.program_id(1) == 0)
    def _():
        h_scr[...] = jnp.zeros_like(h_scr)

    _shift_in_tile(ext_scr, xb_ref[...].astype(F32), ts)
    xc = _causal_conv_from_ext(ext_scr, cw_ref, ts) + cb_ref[...]

    for g in range(n_blocks):
        cols = slice(g * blk, (g + 1) * blk)
        xg = xc[:, cols]
        ri = jnp.dot(xg.astype(BF16), wri_ref[g], preferred_element_type=F32)
        r = jax.nn.sigmoid(ri[:, :blk] + br_ref[:, cols])
        ig = jax.nn.sigmoid(ri[:, blk:] + bi_ref[:, cols])
        log_a = (RG_C * r) * _log_sigmoid(lam_ref[:, cols])
        a = jnp.exp(log_a)
        a_scr[:, cols] = a
        z = 1.0 - a * a
        b_scr[:, cols] = jnp.where(z > 0.0, z * lax.rsqrt(z), z) * (ig * xg)

    row = lax.broadcasted_iota(jnp.int32, (SUBLANES, width), 0)

    def body(i, h):
        r0 = i * SUBLANES
        a = a_scr[pl.ds(r0, SUBLANES), :]
        b = b_scr[pl.ds(r0, SUBLANES), :]
        for sh in (1, 2, 4):
            keep = row >= sh
            a_prev = jnp.where(keep, pltpu.roll(a, sh, 0), 1.0)
            b_prev = jnp.where(keep, pltpu.roll(b, sh, 0), 0.0)
            b = a * b_prev + b
            a = a * a_prev
        hs = a * h + b
        hs_scr[pl.ds(r0, SUBLANES), :] = hs
        return jnp.broadcast_to(hs[SUBLANES - 1:SUBLANES, :], (SUBLANES, width))

    h = h_scr[...]
    for i in range(ts // SUBLANES):
        h = body(i, h)
    h_scr[...] = h
    y = (hs_scr[...] * jax.nn.gelu(gate_ref[...].astype(F32))).astype(BF16)
    _project_residual(y, wout_ref, x_ref, mgate_ref, o_ref)


def _rg_mid_call(p, conv_w, conv_b, w_ri, b_r, b_i, lam, w_out, widx, x, mod_r, layer, batch, ts=256):
    m, two_w = p.shape
    width = two_w // 2
    d = x.shape[1]
    seq = m // batch
    ts = min(ts, seq)
    nst = seq // ts
    n_blocks, blk, _ = w_ri.shape
    full = lambda shape: pl.BlockSpec(shape, lambda b, s: (0,) * len(shape))
    return pl.pallas_call(
        _rg_mid_kernel,
        out_shape=jax.ShapeDtypeStruct((m, d), F32),
        grid=(batch, nst),
        in_specs=[
            pl.BlockSpec((ts, width), lambda b, s: (b * nst + s, 0)),
            pl.BlockSpec((ts, width), lambda b, s: (b * nst + s, 1)),
            full(conv_w.shape),
            full((1, width)),
            full((n_blocks, blk, 2 * blk)),
            full((1, width)),
            full((1, width)),
            full((1, width)),
        ] + _res_specs(w_out, widx, ts, nst, d, layer * batch * 9 + 5),
        out_specs=pl.BlockSpec((ts, d), lambda b, s: (b * nst + s, 0)),
        scratch_shapes=[
            pltpu.VMEM((ts + SUBLANES, width), F32),
            pltpu.VMEM((ts, width), F32),
            pltpu.VMEM((ts, width), F32),
            pltpu.VMEM((ts, width), F32),
            pltpu.VMEM((SUBLANES, width), F32),
        ],
        compiler_params=_cparams(2),
        name="rglru_core",
    )(p, p, conv_w, conv_b.reshape(1, width), w_ri, b_r.reshape(1, width), b_i.reshape(1, width),
      lam.reshape(1, width), w_out, x, mod_r)


def _sc_mid_kernel(bg_ref, cg_ref, xv_ref, cw_ref, wout_ref, x_ref, mgate_ref, o_ref, ext_scr):
    ts = bg_ref.shape[0]
    _shift_in_tile(ext_scr, cg_ref[...].astype(F32) * xv_ref[...].astype(F32), ts)
    conv = _causal_conv_from_ext(ext_scr, cw_ref, ts)
    y = (bg_ref[...].astype(F32) * conv).astype(BF16)
    _project_residual(y, wout_ref, x_ref, mgate_ref, o_ref)


def _sc_mid_call(p, conv_w, w_out, widx, x, mod_r, layer, batch, ts=256):
    m, three_d = p.shape
    d = three_d // 3
    seq = m // batch
    ts = min(ts, seq)
    nst = seq // ts
    return pl.pallas_call(
        _sc_mid_kernel,
        out_shape=jax.ShapeDtypeStruct((m, d), F32),
        grid=(batch, nst),
        in_specs=[
            pl.BlockSpec((ts, d), lambda b, s: (b * nst + s, 0)),
            pl.BlockSpec((ts, d), lambda b, s: (b * nst + s, 1)),
            pl.BlockSpec((ts, d), lambda b, s: (b * nst + s, 2)),
            pl.BlockSpec(conv_w.shape, lambda b, s: (0, 0)),
        ] + _res_specs(w_out, widx, ts, nst, d, layer * batch * 9 + 5),
        out_specs=pl.BlockSpec((ts, d), lambda b, s: (b * nst + s, 0)),
        scratch_shapes=[pltpu.VMEM((ts + SUBLANES, d), F32)],
        compiler_params=_cparams(2),
        name="shortconv_core",
    )(p, p, p, conv_w, w_out, x, mod_r)


def _fox_prep_kernel(q_ref, k_ref, v_ref, fl_ref, bf_ref, qg_ref, kg_ref, qa_ref, ka_ref, vt_ref, carry_scr):
    ts, d = q_ref.shape

    @pl.when(pl.program_id(1) == 0)
    def _():
        carry_scr[...] = jnp.zeros_like(carry_scr)

    logf = _log_sigmoid(fl_ref[...] + bf_ref[...])
    r = lax.broadcasted_iota(jnp.int32, (ts, ts), 0)
    c = lax.broadcasted_iota(jnp.int32, (ts, ts), 1)
    tril = jnp.where(c <= r, 1.0, 0.0).astype(BF16)
    cum = _tril_dot(tril, logf) + carry_scr[0:1, :]
    carry_scr[...] = jnp.broadcast_to(cum[ts - 1:ts, :], carry_scr.shape)
    f_over_scale = cum * math.sqrt(HEAD_DIM)

    lane = lax.broadcasted_iota(jnp.int32, (ts, HEAD_DIM), 1)
    for h in range(d // HEAD_DIM):
        cols = slice(h * HEAD_DIM, (h + 1) * HEAD_DIM)
        hi, mid, lo = (part.astype(F32) for part in _split3(f_over_scale[:, h:h + 1]))
        parts = jnp.where((lane == 0) | (lane == 3), hi, jnp.where((lane == 1) | (lane == 4), mid, lo))
        q_aug = jnp.where(lane < 3, parts, jnp.where(lane < 6, 1.0, 0.0))
        k_aug = jnp.where(lane < 3, 1.0, jnp.where(lane < 6, -parts, 0.0))
        for src, gain, aug, dst in ((q_ref, qg_ref, q_aug, qa_ref), (k_ref, kg_ref, k_aug, ka_ref)):
            t = src[:, cols].astype(F32)
            ms = jnp.mean(t * t, axis=-1, keepdims=True)
            dst[:, 2 * h * HEAD_DIM:(2 * h + 1) * HEAD_DIM] = ((t * lax.rsqrt(ms + EPS)) * gain[...]).astype(BF16)
            dst[:, (2 * h + 1) * HEAD_DIM:(2 * h + 2) * HEAD_DIM] = aug.astype(BF16)
        vt_ref[h] = v_ref[:, cols].astype(F32).T.astype(BF16)


def _fox_prep_call(p, fl, b_f_pad, q_gain, k_gain, batch, ts):
    m = p.shape[0]
    d = p.shape[1] // 4
    n_heads = d // HEAD_DIM
    seq = m // batch
    nst = seq // ts
    lanes = fl.shape[1]
    return pl.pallas_call(
        _fox_prep_kernel,
        out_shape=(
            jax.ShapeDtypeStruct((m, 2 * d), BF16),
            jax.ShapeDtypeStruct((m, 2 * d), BF16),
            jax.ShapeDtypeStruct((batch, nst, n_heads, HEAD_DIM, ts), BF16),
        ),
        grid=(batch, nst),
        in_specs=[
            pl.BlockSpec((ts, d), lambda b, s: (b * nst + s, 0)),
            pl.BlockSpec((ts, d), lambda b, s: (b * nst + s, 1)),
            pl.BlockSpec((ts, d), lambda b, s: (b * nst + s, 2)),
            pl.BlockSpec((ts, lanes), lambda b, s: (b * nst + s, 0)),
            pl.BlockSpec((1, lanes), lambda b, s: (0, 0)),
            pl.BlockSpec((1, HEAD_DIM), lambda b, s: (0, 0)),
            pl.BlockSpec((1, HEAD_DIM), lambda b, s: (0, 0)),
        ],
        out_specs=(
            pl.BlockSpec((ts, 2 * d), lambda b, s: (b * nst + s, 0)),
            pl.BlockSpec((ts, 2 * d), lambda b, s: (b * nst + s, 0)),
            pl.BlockSpec((None, None, n_heads, HEAD_DIM, ts), lambda b, s: (b, s, 0, 0, 0)),
        ),
        scratch_shapes=[pltpu.VMEM((SUBLANES, lanes), F32)],
        compiler_params=_cparams(2),
        name="fox_prep",
    )(p, p, p, fl, b_f_pad, q_gain.reshape(1, HEAD_DIM), k_gain.reshape(1, HEAD_DIM))


def _fox_flash_kernel(q_ref, k_ref, vt_ref, g_ref, o_ref):
    t = q_ref.shape[0]
    _, hp, dh, _ = vt_ref.shape
    qi = pl.program_id(2)
    to_log2 = (1.0 / math.sqrt(dh)) * math.log2(math.e)
    nt = (((1,), (1,)), ((), ()))

    def scores(j, e):
        cols = slice(2 * e * dh, 2 * (e + 1) * dh)
        kj = k_ref[pl.ds(pl.multiple_of(j * t, t), t), cols]
        return lax.dot_general(kj, q_ref[:, cols], nt, preferred_element_type=F32)

    def step(j, carry, masked):
        raw = [scores(j, e) for e in range(hp)]
        out = []
        for e in range(hp):
            m_prev, l_prev, acc = carry[e]
            s = raw[e] * to_log2
            if masked:
                key = lax.broadcasted_iota(jnp.int32, (t, t), 0)
                qry = lax.broadcasted_iota(jnp.int32, (t, t), 1)
                s = jnp.where(key <= qry, s, -jnp.inf)
            m_new = jnp.maximum(m_prev, jnp.max(s, axis=0, keepdims=True))
            alpha = jnp.exp2(m_prev - m_new)
            p = jnp.exp2(s - m_new)
            l_new = alpha * l_prev + jnp.sum(p, axis=0, keepdims=True)
            acc = alpha * acc + jnp.dot(vt_ref[j, e], p.astype(BF16), preferred_element_type=F32)
            out.append((m_new, l_new, acc))
        return tuple(out)

    init = tuple((jnp.full((1, t), -jnp.inf, F32), jnp.zeros((1, t), F32), jnp.zeros((dh, t), F32))
                 for _ in range(hp))

    carry = lax.fori_loop(0, qi, functools.partial(step, masked=False), init)
    fin = step(qi, carry, True)
    for e in range(hp):
        _, l_fin, acc = fin[e]
        cols = slice(e * dh, (e + 1) * dh)
        o = (acc / l_fin).T
        o_ref[:, cols] = (o * jax.nn.sigmoid(g_ref[:, cols].astype(F32))).astype(BF16)


def _fox_flash_call(qa, ka, vt, p, batch, t, hp=4):
    m = qa.shape[0]
    n_heads = vt.shape[2]
    d = n_heads * HEAD_DIM
    seq = m // batch
    nq = seq // t
    hp = math.gcd(hp, n_heads)
    g_blk = 3 * n_heads // hp
    return pl.pallas_call(
        _fox_flash_kernel,
        out_shape=jax.ShapeDtypeStruct((m, d), BF16),
        grid=(batch, n_heads // hp, nq),
        in_specs=[
            pl.BlockSpec((t, 2 * hp * HEAD_DIM), lambda b, h, i: (b * nq + i, h)),
            pl.BlockSpec((seq, 2 * hp * HEAD_DIM), lambda b, h, i: (b, h)),
            pl.BlockSpec((None, nq, hp, HEAD_DIM, t), lambda b, h, i: (b, 0, h, 0, 0)),
            pl.BlockSpec((t, hp * HEAD_DIM), lambda b, h, i: (b * nq + i, g_blk + h)),
        ],
        out_specs=pl.BlockSpec((t, hp * HEAD_DIM), lambda b, h, i: (b * nq + i, h)),
        compiler_params=_cparams(3),
        name="fox_flash",
    )(qa, ka, vt, p)


def _hg_mid_kernel(q_ref, fz_ref, iv_ref, g_ref, lbl_ref, ng_ref, wout_ref, x_ref, mgate_ref, o_ref,
                   qin_scr, kin_scr, kdec_scr, dec_scr, st_scr, y_scr, *, layer):
    ts, d = q_ref.shape
    n_heads = d // HEAD_DIM
    n_chunks = ts // CHUNK

    @pl.when(pl.program_id(1) == 0)
    def _():
        st_scr[...] = jnp.zeros_like(st_scr)

    lg = lbl_ref[...]
    depth = lg.shape[0]
    mx = lg[0:1, :]
    for i in range(1, depth):
        mx = jnp.maximum(mx, lg[i:i + 1, :])
    es = [jnp.exp(lg[i:i + 1, :] - mx) for i in range(depth)]
    tot = es[0]
    for i in range(1, depth):
        tot = tot + es[i]
    lb = jnp.zeros_like(tot)
    for i in range(layer):
        lb = lb + es[i] / tot

    sig = jax.nn.sigmoid(fz_ref[...].astype(F32))
    logf = jnp.log(lb + (1.0 - lb) * sig)
    kk = (1.0 - lb) * (1.0 - sig)

    r = lax.broadcasted_iota(jnp.int32, (ts, ts), 0)
    c = lax.broadcasted_iota(jnp.int32, (ts, ts), 1)
    tril = jnp.where((c <= r) & (jnp.right_shift(c, CHUNK_SHIFT) == jnp.right_shift(r, CHUNK_SHIFT)), 1.0, 0.0).astype(BF16)
    bcum = _tril_dot(tril, logf)

    qin_scr[...] = (q_ref[...].astype(F32) * jnp.exp(bcum)).astype(BF16)
    kin_scr[...] = (kk * jnp.exp(-bcum)).astype(BF16)
    for ci in range(n_chunks):
        rows = slice(ci * CHUNK, (ci + 1) * CHUNK)
        b_last = bcum[(ci + 1) * CHUNK - 1:(ci + 1) * CHUNK, :]
        kdec_scr[rows, :] = (kk[rows, :] * jnp.exp(b_last - bcum[rows, :])).astype(BF16)
        dec_scr[ci] = jnp.broadcast_to(jnp.exp(b_last), (SUBLANES, d))

    rr = lax.broadcasted_iota(jnp.int32, (CHUNK, CHUNK), 0)
    cc = lax.broadcasted_iota(jnp.int32, (CHUNK, CHUNK), 1)
    causal = cc <= rr
    nt = (((1,), (1,)), ((), ()))
    tn = (((0,), (0,)), ((), ()))

    def chunk_body(ci, carry):
        r0 = ci * CHUNK
        rows = pl.ds(r0, CHUNK)
        heads = [slice(h * HEAD_DIM, (h + 1) * HEAD_DIM) for h in range(n_heads)]
        qh = [qin_scr[rows, c] for c in heads]
        vh = [iv_ref[rows, c].astype(BF16) for c in heads]
        st = [st_scr[h] for h in range(n_heads)]
        sc = [lax.dot_general(qh[h], kin_scr[rows, c], nt, preferred_element_type=F32)
              for h, c in enumerate(heads)]
        o_inter = [lax.dot_general(qh[h], st[h].astype(BF16), nt, preferred_element_type=F32)
                   for h in range(n_heads)]
        u_t = [lax.dot_general(vh[h], kdec_scr[rows, c], tn, preferred_element_type=F32)
               for h, c in enumerate(heads)]
        for h, c in enumerate(heads):
            scm = jnp.where(causal, sc[h], 0.0).astype(BF16)
            o = jnp.dot(scm, vh[h], preferred_element_type=F32) + o_inter[h]
            st_scr[h] = st[h] * dec_scr[ci][0:1, c] + u_t[h]
            ms = jnp.mean(o * o, axis=-1, keepdims=True)
            on = (o * lax.rsqrt(ms + EPS)) * ng_ref[...]
            gh = g_ref[rows, c].astype(F32)
            y_scr[rows, c] = (on * (gh * jax.nn.sigmoid(gh))).astype(BF16)
        return carry

    for ci in range(n_chunks):
        chunk_body(ci, 0)
    _project_residual(y_scr[...], wout_ref, x_ref, mgate_ref, o_ref)


def _hg_mid_call(p, lb_logits, norm_gain, w_out, widx, x, mod_r, layer, batch, ts=256):
    m, four_d = p.shape
    d = four_d // 4
    n_heads = d // HEAD_DIM
    seq = m // batch
    ts = min(ts, seq)
    nst = seq // ts
    return pl.pallas_call(
        functools.partial(_hg_mid_kernel, layer=layer),
        out_shape=jax.ShapeDtypeStruct((m, d), F32),
        grid=(batch, nst),
        in_specs=[
            pl.BlockSpec((ts, d), lambda b, s: (b * nst + s, 0)),
            pl.BlockSpec((ts, d), lambda b, s: (b * nst + s, 1)),
            pl.BlockSpec((ts, d), lambda b, s: (b * nst + s, 2)),
            pl.BlockSpec((ts, d), lambda b, s: (b * nst + s, 3)),
            pl.BlockSpec(lb_logits.shape, lambda b, s: (0, 0)),
            pl.BlockSpec((1, HEAD_DIM), lambda b, s: (0, 0)),
        ] + _res_specs(w_out, widx, ts, nst, d, layer * batch * 9 + 5),
        out_specs=pl.BlockSpec((ts, d), lambda b, s: (b * nst + s, 0)),
        scratch_shapes=[
            pltpu.VMEM((ts, d), BF16),
            pltpu.VMEM((ts, d), BF16),
            pltpu.VMEM((ts, d), BF16),
            pltpu.VMEM((ts // CHUNK, SUBLANES, d), F32),
            pltpu.VMEM((n_heads, HEAD_DIM, HEAD_DIM), F32),
            pltpu.VMEM((ts, d), BF16),
        ],
        compiler_params=_cparams(2),
        name="hgrn2_core",
    )(p, p, p, p, lb_logits, norm_gain.reshape(1, HEAD_DIM), w_out, x, mod_r)


def kernel(x, c, ada_w, ada_b, norm_g, ffn_w_gate, ffn_w_up, ffn_w_down, rg_w_in, rg_conv_w, rg_conv_b, rg_w_r, rg_b_r, rg_w_i, rg_b_i, rg_lam, rg_w_out, sc_w_in, sc_conv_w, sc_w_out, fox_w_in, fox_b_f, fox_q_gain, fox_k_gain, fox_w_out, hg_w_in, hg_lb_logits, hg_norm_gain, hg_w_out):
    batch, seq, d = x.shape
    depth = ada_w.shape[0]
    n_heads = d // HEAD_DIM
    n_mixers = 4
    mid_dtype = BF16

    mod = _ada_call(c, ada_w, ada_b)
    mod_r = mod.reshape(depth * batch * 9, 1, d)
    ng_r = norm_g.reshape(depth * 3, 1, d)

    ffn_w = (ffn_w_gate, ffn_w_up, ffn_w_down)
    w_cur = None

    def ffn(xf, w_cur, k, sub):
        nxt = (ffn_w, (k + 1) // 2, (k + 1) % 2) if k + 1 < 2 * depth else None
        if w_cur is None:
            return _ffn_call(xf, mod_r, ng_r, ffn_w, k // 2, sub, batch, nxt, w_index=(0, 0), tf=256)
        return _ffn_call(xf, mod_r, ng_r, w_cur, k // 2, sub, batch, nxt)

    xf = x.reshape(batch * seq, d)
    for i in range(depth):
        xf, w_cur = ffn(xf, w_cur, 2 * i, 0)
        m, j = i % n_mixers, i // n_mixers
        if m == 0:
            p = _mm_mod_call(xf, mod_r, ng_r, rg_w_in, j, rg_w_in.shape[2], i, batch, mid_dtype)
            w_ri = jnp.concatenate([rg_w_r[j], rg_w_i[j]], axis=-1).astype(BF16)
            xf = _rg_mid_call(p, rg_conv_w[j], rg_conv_b[j], w_ri, rg_b_r[j], rg_b_i[j], rg_lam[j],
                              rg_w_out.astype(BF16), j, xf, mod_r, i, batch)
        elif m == 1:
            p = _mm_mod_call(xf, mod_r, ng_r, sc_w_in, j, sc_w_in.shape[2], i, batch, mid_dtype)
            xf = _sc_mid_call(p, sc_conv_w[j], sc_w_out.astype(BF16), j, xf, mod_r, i, batch)
        elif m == 2:
            w_in_t = jnp.swapaxes(fox_w_in, 1, 2)
            p = _mm_mod_call(xf, mod_r, ng_r, w_in_t, j, 4 * d, i, batch, mid_dtype, w_t=True)
            w_fl = jnp.pad(w_in_t[j, 4 * d:], ((0, HEAD_DIM - n_heads), (0, 0)))[None]
            fl = _mm_mod_call(xf, mod_r, ng_r, w_fl, 0, HEAD_DIM, i, batch, F32, w_t=True)
            b_f_pad = jnp.pad(fox_b_f[j], (0, HEAD_DIM - n_heads)).reshape(1, HEAD_DIM)
            t_attn = min(512, seq)
            qa, ka, vt = _fox_prep_call(p, fl, b_f_pad, fox_q_gain[j], fox_k_gain[j], batch, t_attn)
            y = _fox_flash_call(qa, ka, vt, p, batch, t_attn)
            xf = _mm_res_call(y, fox_w_out.astype(BF16), j, xf, mod_r, i, batch)
        else:
            p = _mm_mod_call(xf, mod_r, ng_r, hg_w_in, j, hg_w_in.shape[2], i, batch, mid_dtype)
            xf = _hg_mid_call(p, hg_lb_logits, hg_norm_gain[j], hg_w_out.astype(BF16), j, xf, mod_r, i, batch)
        xf, w_cur = ffn(xf, w_cur, 2 * i + 1, 2)
    return xf.reshape(batch, seq, d)
```

```python
import functools
import math

import jax
import jax.numpy as jnp
from jax import lax
from jax.experimental import pallas as pl
from jax.experimental.pallas import tpu as pltpu

F32 = jnp.float32
BF16 = jnp.bfloat16

EPS = 1e-6
HEAD_DIM = 128
CHUNK = 64
CHUNK_SHIFT = 6
RG_C = 8.0
SUBLANES = 8
VMEM_LIMIT = 60 * 1024 * 1024


def _cparams(n_axes):
    return pltpu.CompilerParams(dimension_semantics=("arbitrary",) * n_axes,
                                vmem_limit_bytes=VMEM_LIMIT)


def _tile(n, preferred):
    if n <= preferred:
        return n
    t = preferred - preferred % HEAD_DIM
    while n % t:
        t -= HEAD_DIM
    return t


def _modulate(x, g, shift, scale):
    ms = jnp.mean(x * x, axis=-1, keepdims=True)
    return (x * lax.rsqrt(ms + EPS)) * (g * (1.0 + scale)) + shift


def _log_sigmoid(x):
    return jnp.minimum(x, 0.0) - jnp.log1p(jnp.exp(-jnp.abs(x)))


def _split3(x):
    hi = x.astype(BF16)
    r1 = x - hi.astype(F32)
    mid = r1.astype(BF16)
    lo = (r1 - mid.astype(F32)).astype(BF16)
    return hi, mid, lo


def _tril_dot(tril, x):
    hi, mid, lo = _split3(x)
    acc = jnp.dot(tril, hi, preferred_element_type=F32)
    acc = acc + jnp.dot(tril, mid, preferred_element_type=F32)
    return acc + jnp.dot(tril, lo, preferred_element_type=F32)


def _ada_kernel(c_ref, w_ref, b_ref, o_ref):
    c = c_ref[...]
    sc = (c * jax.nn.sigmoid(c)).astype(BF16)
    o_ref[...] = jnp.dot(sc, w_ref[...].astype(BF16), preferred_element_type=F32) + b_ref[...]


def _ada_call(c, ada_w, ada_b, tn=1024):
    depth, d, n = ada_w.shape
    b = c.shape[0]
    rows = -(-b // SUBLANES) * SUBLANES
    c_pad = jnp.pad(c, ((0, rows - b), (0, 0)))
    tn = _tile(n, tn)
    out = pl.pallas_call(
        _ada_kernel,
        out_shape=jax.ShapeDtypeStruct((depth, rows, n), F32),
        grid=(depth, n // tn),
        in_specs=[
            pl.BlockSpec((rows, d), lambda l, j: (0, 0)),
            pl.BlockSpec((None, d, tn), lambda l, j: (l, 0, j)),
            pl.BlockSpec((None, 1, tn), lambda l, j: (l, 0, j)),
        ],
        out_specs=pl.BlockSpec((None, rows, tn), lambda l, j: (l, 0, j)),
        compiler_params=_cparams(2),
        name="ada",
    )(c_pad, ada_w, ada_b.reshape(depth, 1, n))
    return out[:, :b]


def _row_chunks(tm, rc):
    return [slice(r, r + rc) for r in range(0, tm, rc)]


def _ffn_kernel(x_ref, shift_ref, scale_ref, gate_ref, g_ref, wg_ref, wu_ref, wd_ref, *rest, rc, nf, n_ahead):
    ahead_in = rest[:n_ahead]
    o_ref = rest[n_ahead]
    ahead_out = rest[n_ahead + 1:2 * n_ahead + 1]
    h_scr = rest[2 * n_ahead + 1]
    f = pl.program_id(1)
    chunks = _row_chunks(x_ref.shape[0], rc)

    def weights():
        for src, dst in zip(ahead_in, ahead_out):
            dst[...] = src[...].astype(BF16)
        return wg_ref[...].astype(BF16), wu_ref[...].astype(BF16), wd_ref[...].astype(BF16)

    def swiglu(h, wg, wu, wd):
        gg = jnp.dot(h, wg, preferred_element_type=F32)
        uu = jnp.dot(h, wu, preferred_element_type=F32)
        a = ((gg * jax.nn.sigmoid(gg)) * uu).astype(BF16)
        return jnp.dot(a, wd, preferred_element_type=F32)

    def run(first, last):
        w = weights()
        half_gate = 0.5 * gate_ref[...] if last else None
        for rows in chunks:
            if first:
                h = _modulate(x_ref[rows, :], g_ref[...], shift_ref[...], scale_ref[...]).astype(BF16)
                h_scr[rows, :] = h
                acc = swiglu(h, *w)
            else:
                acc = o_ref[rows, :] + swiglu(h_scr[rows, :], *w)
            if last:
                acc = x_ref[rows, :] + half_gate * acc
            o_ref[rows, :] = acc

    if nf == 1:
        run(True, True)
    else:
        pl.when(f == 0)(functools.partial(run, True, False))
        pl.when((f > 0) & (f < nf - 1))(functools.partial(run, False, False))
        pl.when(f == nf - 1)(functools.partial(run, False, True))


def _mod_spec(d, base, tiles_per_batch, n_grid_axes, col_axis=None, tn=None):
    width = d if tn is None else tn

    def index(*ids):
        col = 0 if col_axis is None else ids[col_axis]
        return (base + 9 * (ids[0] // tiles_per_batch), 0, col)

    del n_grid_axes
    return pl.BlockSpec((None, 1, width), index)


def _ffn_call(x, mod_r, ng_r, w, layer, sub, batch, nxt=None, w_index=None, tm=1024, tf=512, rc=512):
    wg, wu, wd = w
    m, d = x.shape
    f_dim = wg.shape[-1]
    tm = min(tm, m // batch)
    rc = min(rc, tm)
    tf = _tile(f_dim, tf)
    tpb = (m // batch) // tm
    ni, nf = m // tm, f_dim // tf
    base = layer * batch * 9 + sub * 3
    if w_index is None:
        w_specs = [pl.BlockSpec((d, tf), lambda i, f: (0, f)), pl.BlockSpec((d, tf), lambda i, f: (0, f)),
                   pl.BlockSpec((tf, d), lambda i, f: (f, 0))]
    else:
        wl, ww = w_index
        w_specs = [pl.BlockSpec((None, None, d, tf), lambda i, f: (wl, ww, 0, f)),
                   pl.BlockSpec((None, None, d, tf), lambda i, f: (wl, ww, 0, f)),
                   pl.BlockSpec((None, None, tf, d), lambda i, f: (wl, ww, f, 0))]
    in_specs = [
        pl.BlockSpec((tm, d), lambda i, f: (i, 0)),
        _mod_spec(d, base + 0, tpb, 2),
        _mod_spec(d, base + 1, tpb, 2),
        _mod_spec(d, base + 2, tpb, 2),
        pl.BlockSpec((None, 1, d), lambda i, f: (layer * 3 + sub, 0, 0)),
    ] + w_specs
    operands = [x, mod_r, mod_r, mod_r, ng_r, wg, wu, wd]
    out_shape = [jax.ShapeDtypeStruct((m, d), F32)]
    out_specs = [pl.BlockSpec((tm, d), lambda i, f: (i, 0))]
    if nxt is not None:
        (ng, nu, nd), nl, nw = nxt
        dr = d // ni
        in_specs += [
            pl.BlockSpec((None, None, dr, tf), lambda i, f: (nl, nw, i, f)),
            pl.BlockSpec((None, None, dr, tf), lambda i, f: (nl, nw, i, f)),
            pl.BlockSpec((None, None, tf, dr), lambda i, f: (nl, nw, f, i)),
        ]
        operands += [ng, nu, nd]
        out_shape += [jax.ShapeDtypeStruct((d, f_dim), BF16), jax.ShapeDtypeStruct((d, f_dim), BF16),
                      jax.ShapeDtypeStruct((f_dim, d), BF16)]
        out_specs += [pl.BlockSpec((dr, tf), lambda i, f: (i, f)), pl.BlockSpec((dr, tf), lambda i, f: (i, f)),
                      pl.BlockSpec((tf, dr), lambda i, f: (f, i))]
    outs = pl.pallas_call(
        functools.partial(_ffn_kernel, rc=rc, nf=nf, n_ahead=0 if nxt is None else 3),
        out_shape=out_shape,
        grid=(ni, nf),
        in_specs=in_specs,
        out_specs=out_specs,
        scratch_shapes=[pltpu.VMEM((tm, d), BF16)],
        compiler_params=_cparams(2),
        name="ffn",
    )(*operands)
    return outs[0], tuple(outs[1:])


def _mm_mod_kernel(x_ref, shift_ref, scale_ref, g_ref, w_ref, *rest, rc, w_t, side):
    if side:
        ws_ref, o_ref, os_ref, h_scr = rest
    else:
        o_ref, h_scr = rest
    j = pl.program_id(1)
    chunks = _row_chunks(x_ref.shape[0], rc)
    nt = (((1,), (1,)), ((), ()))
    dims = nt if w_t else (((1,), (0,)), ((), ()))

    def project(h, w):
        return lax.dot_general(h, w, dims, preferred_element_type=F32).astype(o_ref.dtype)

    @pl.when(j == 0)
    def _():
        w = w_ref[...].astype(BF16)
        for rows in chunks:
            h = _modulate(x_ref[rows, :], g_ref[...], shift_ref[...], scale_ref[...]).astype(BF16)
            h_scr[rows, :] = h
            o_ref[rows, :] = project(h, w)
            if side:
                os_ref[rows, :] = lax.dot_general(h, ws_ref[...].astype(BF16), nt, preferred_element_type=F32)

    @pl.when(j > 0)
    def _():
        w = w_ref[...].astype(BF16)
        for rows in chunks:
            o_ref[rows, :] = project(h_scr[rows, :], w)


def _mm_mod_call(x, mod_r, ng_r, w, widx, n_out, layer, batch, out_dtype, tm=2048, tn=512, rc=512, w_t=False,
                 w_side=None):
    m, d = x.shape
    tm = min(tm, m // batch)
    rc = min(rc, tm)
    tn = _tile(n_out, tn)
    tpb = (m // batch) // tm
    base = layer * batch * 9 + 3
    if w_t:
        w_spec = pl.BlockSpec((None, tn, d), lambda i, j: (widx, j, 0))
    else:
        w_spec = pl.BlockSpec((None, d, tn), lambda i, j: (widx, 0, j))
    in_specs = [
        pl.BlockSpec((tm, d), lambda i, j: (i, 0)),
        _mod_spec(d, base + 0, tpb, 2),
        _mod_spec(d, base + 1, tpb, 2),
        pl.BlockSpec((None, 1, d), lambda i, j: (layer * 3 + 1, 0, 0)),
        w_spec,
    ]
    operands = [x, mod_r, mod_r, ng_r, w]
    out_shape = [jax.ShapeDtypeStruct((m, n_out), out_dtype)]
    out_specs = [pl.BlockSpec((tm, tn), lambda i, j: (i, j))]
    if w_side is not None:
        n_side = w_side.shape[0]
        in_specs.append(pl.BlockSpec((n_side, d), lambda i, j: (0, 0)))
        operands.append(w_side)
        out_shape.append(jax.ShapeDtypeStruct((m, n_side), F32))
        out_specs.append(pl.BlockSpec((tm, n_side), lambda i, j: (i, 0)))
    outs = pl.pallas_call(
        functools.partial(_mm_mod_kernel, rc=rc, w_t=w_t, side=w_side is not None),
        out_shape=out_shape,
        grid=(m // tm, n_out // tn),
        in_specs=in_specs,
        out_specs=out_specs,
        scratch_shapes=[pltpu.VMEM((tm, d), BF16)],
        compiler_params=_cparams(2),
        name="in_proj",
    )(*operands)
    return outs[0] if w_side is None else tuple(outs)


def _mm_res_kernel(y_ref, w_ref, x_ref, gate_ref, o_ref):
    acc = jnp.dot(y_ref[...], w_ref[...], preferred_element_type=F32)
    o_ref[...] = x_ref[...] + gate_ref[...] * acc


def _mm_res_call(y, w, widx, x, mod_r, layer, batch, tm=512, tn=2048):
    m, k = y.shape
    d = x.shape[1]
    tm = min(tm, m // batch)
    tn = _tile(d, tn)
    tpb = (m // batch) // tm
    base = layer * batch * 9 + 3 + 2
    return pl.pallas_call(
        _mm_res_kernel,
        out_shape=jax.ShapeDtypeStruct((m, d), F32),
        grid=(m // tm, d // tn),
        in_specs=[
            pl.BlockSpec((tm, k), lambda i, j: (i, 0)),
            pl.BlockSpec((None, k, tn), lambda i, j: (widx, 0, j)),
            pl.BlockSpec((tm, tn), lambda i, j: (i, j)),
            _mod_spec(d, base, tpb, 2, col_axis=1, tn=tn),
        ],
        out_specs=pl.BlockSpec((tm, tn), lambda i, j: (i, j)),
        compiler_params=_cparams(2),
        name="out_proj",
    )(y, w, x, mod_r)


def _res_specs(w_out, widx, ts, nst, d, gate_base):
    k = w_out.shape[1]
    return [
        pl.BlockSpec((None, k, d), lambda b, s: (widx, 0, 0)),
        pl.BlockSpec((ts, d), lambda b, s: (b * nst + s, 0)),
        pl.BlockSpec((None, 1, d), lambda b, s: (gate_base + 9 * b, 0, 0)),
    ]


def _project_residual(y, wout_ref, x_ref, mgate_ref, o_ref):
    o_ref[...] = x_ref[...] + mgate_ref[...] * jnp.dot(y, wout_ref[...], preferred_element_type=F32)


def _shift_in_tile(ext_scr, cur, ts):
    s = pl.program_id(1)

    @pl.when(s == 0)
    def _():
        ext_scr[0:SUBLANES, :] = jnp.zeros((SUBLANES, ext_scr.shape[1]), F32)

    @pl.when(s > 0)
    def _():
        ext_scr[0:SUBLANES, :] = ext_scr[ts:ts + SUBLANES, :]

    ext_scr[SUBLANES:ts + SUBLANES, :] = cur


def _causal_conv_from_ext(ext_scr, cw_ref, ts):
    kw = cw_ref.shape[0]
    ext = ext_scr[...]
    acc = None
    for k in range(kw):
        shift = kw - 1 - k
        tap = ext if shift == 0 else pltpu.roll(ext, shift, 0)
        term = cw_ref[k:k + 1, :] * tap[SUBLANES:, :]
        acc = term if acc is None else acc + term
    return acc


def _rg_mid_kernel(gate_ref, xb_ref, cw_ref, cb_ref, wri_ref, br_ref, bi_ref, lam_ref,
                   wout_ref, x_ref, mgate_ref, o_ref, ext_scr, a_scr, b_scr, hs_scr, h_scr):
    ts, width = xb_ref.shape
    n_blocks, blk, _ = wri_ref.shape

    @pl.when(pl.program_id(1) == 0)
    def _():
        h_scr[...] = jnp.zeros_like(h_scr)

    _shift_in_tile(ext_scr, xb_ref[...].astype(F32), ts)
    xc = _causal_conv_from_ext(ext_scr, cw_ref, ts) + cb_ref[...]

    for g in range(n_blocks):
        cols = slice(g * blk, (g + 1) * blk)
        xg = xc[:, cols]
        ri = jnp.dot(xg.astype(BF16), wri_ref[g], preferred_element_type=F32)
        r = jax.nn.sigmoid(ri[:, :blk] + br_ref[:, cols])
        ig = jax.nn.sigmoid(ri[:, blk:] + bi_ref[:, cols])
        log_a = (RG_C * r) * _log_sigmoid(lam_ref[:, cols])
        a = jnp.exp(log_a)
        a_scr[:, cols] = a
        z = 1.0 - a * a
        b_scr[:, cols] = jnp.where(z > 0.0, z * lax.rsqrt(z), z) * (ig * xg)

    row = lax.broadcasted_iota(jnp.int32, (SUBLANES, width), 0)

    def body(i, h):
        r0 = i * SUBLANES
        a = a_scr[pl.ds(r0, SUBLANES), :]
        b = b_scr[pl.ds(r0, SUBLANES), :]
        for sh in (1, 2, 4):
            keep = row >= sh
            a_prev = jnp.where(keep, pltpu.roll(a, sh, 0), 1.0)
            b_prev = jnp.where(keep, pltpu.roll(b, sh, 0), 0.0)
            b = a * b_prev + b
            a = a * a_prev
        hs = a * h + b
        hs_scr[pl.ds(r0, SUBLANES), :] = hs
        return jnp.broadcast_to(hs[SUBLANES - 1:SUBLANES, :], (SUBLANES, width))

    h = h_scr[...]
    for i in range(ts // SUBLANES):
        h = body(i, h)
    h_scr[...] = h
    y = (hs_scr[...] * jax.nn.gelu(gate_ref[...].astype(F32))).astype(BF16)
    _project_residual(y, wout_ref, x_ref, mgate_ref, o_ref)


def _rg_mid_call(p, conv_w, conv_b, w_ri, b_r, b_i, lam, w_out, widx, x, mod_r, layer, batch, ts=256):
    m, two_w = p.shape
    width = two_w // 2
    d = x.shape[1]
    seq = m // batch
    ts = min(ts, seq)
    nst = seq // ts
    n_blocks, blk, _ = w_ri.shape
    full = lambda shape: pl.BlockSpec(shape, lambda b, s: (0,) * len(shape))
    return pl.pallas_call(
        _rg_mid_kernel,
        out_shape=jax.ShapeDtypeStruct((m, d), F32),
        grid=(batch, nst),
        in_specs=[
            pl.BlockSpec((ts, width), lambda b, s: (b * nst + s, 0)),
            pl.BlockSpec((ts, width), lambda b, s: (b * nst + s, 1)),
            full(conv_w.shape),
            full((1, width)),
            full((n_blocks, blk, 2 * blk)),
            full((1, width)),
            full((1, width)),
            full((1, width)),
        ] + _res_specs(w_out, widx, ts, nst, d, layer * batch * 9 + 5),
        out_specs=pl.BlockSpec((ts, d), lambda b, s: (b * nst + s, 0)),
        scratch_shapes=[
            pltpu.VMEM((ts + SUBLANES, width), F32),
            pltpu.VMEM((ts, width), F32),
            pltpu.VMEM((ts, width), F32),
            pltpu.VMEM((ts, width), F32),
            pltpu.VMEM((SUBLANES, width), F32),
        ],
        compiler_params=_cparams(2),
        name="rglru_core",
    )(p, p, conv_w, conv_b.reshape(1, width), w_ri, b_r.reshape(1, width), b_i.reshape(1, width),
      lam.reshape(1, width), w_out, x, mod_r)


def _sc_mid_kernel(bg_ref, cg_ref, xv_ref, cw_ref, wout_ref, x_ref, mgate_ref, o_ref, ext_scr):
    ts = bg_ref.shape[0]
    _shift_in_tile(ext_scr, cg_ref[...].astype(F32) * xv_ref[...].astype(F32), ts)
    conv = _causal_conv_from_ext(ext_scr, cw_ref, ts)
    y = (bg_ref[...].astype(F32) * conv).astype(BF16)
    _project_residual(y, wout_ref, x_ref, mgate_ref, o_ref)


def _sc_mid_call(p, conv_w, w_out, widx, x, mod_r, layer, batch, ts=256):
    m, three_d = p.shape
    d = three_d // 3
    seq = m // batch
    ts = min(ts, seq)
    nst = seq // ts
    return pl.pallas_call(
        _sc_mid_kernel,
        out_shape=jax.ShapeDtypeStruct((m, d), F32),
        grid=(batch, nst),
        in_specs=[
            pl.BlockSpec((ts, d), lambda b, s: (b * nst + s, 0)),
            pl.BlockSpec((ts, d), lambda b, s: (b * nst + s, 1)),
            pl.BlockSpec((ts, d), lambda b, s: (b * nst + s, 2)),
            pl.BlockSpec(conv_w.shape, lambda b, s: (0, 0)),
        ] + _res_specs(w_out, widx, ts, nst, d, layer * batch * 9 + 5),
        out_specs=pl.BlockSpec((ts, d), lambda b, s: (b * nst + s, 0)),
        scratch_shapes=[pltpu.VMEM((ts + SUBLANES, d), F32)],
        compiler_params=_cparams(2),
        name="shortconv_core",
    )(p, p, p, conv_w, w_out, x, mod_r)


def _fox_prep_kernel(q_ref, k_ref, v_ref, fl_ref, bf_ref, qg_ref, kg_ref, qa_ref, ka_ref, vt_ref, carry_scr):
    ts, d = q_ref.shape

    @pl.when(pl.program_id(1) == 0)
    def _():
        carry_scr[...] = jnp.zeros_like(carry_scr)

    logf = _log_sigmoid(fl_ref[...] + bf_ref[...])
    r = lax.broadcasted_iota(jnp.int32, (ts, ts), 0)
    c = lax.broadcasted_iota(jnp.int32, (ts, ts), 1)
    tril = jnp.where(c <= r, 1.0, 0.0).astype(BF16)
    cum = _tril_dot(tril, logf) + carry_scr[0:1, :]
    carry_scr[...] = jnp.broadcast_to(cum[ts - 1:ts, :], carry_scr.shape)
    f_over_scale = cum * math.sqrt(HEAD_DIM)

    lane = lax.broadcasted_iota(jnp.int32, (ts, HEAD_DIM), 1)
    for h in range(d // HEAD_DIM):
        cols = slice(h * HEAD_DIM, (h + 1) * HEAD_DIM)
        hi, mid, lo = (part.astype(F32) for part in _split3(f_over_scale[:, h:h + 1]))
        parts = jnp.where((lane == 0) | (lane == 3), hi, jnp.where((lane == 1) | (lane == 4), mid, lo))
        q_aug = jnp.where(lane < 3, parts, jnp.where(lane < 6, 1.0, 0.0))
        k_aug = jnp.where(lane < 3, 1.0, jnp.where(lane < 6, -parts, 0.0))
        for src, gain, aug, dst in ((q_ref, qg_ref, q_aug, qa_ref), (k_ref, kg_ref, k_aug, ka_ref)):
            t = src[:, cols].astype(F32)
            ms = jnp.mean(t * t, axis=-1, keepdims=True)
            dst[:, 2 * h * HEAD_DIM:(2 * h + 1) * HEAD_DIM] = ((t * lax.rsqrt(ms + EPS)) * gain[...]).astype(BF16)
            dst[:, (2 * h + 1) * HEAD_DIM:(2 * h + 2) * HEAD_DIM] = aug.astype(BF16)
        vt_ref[h] = v_ref[:, cols].astype(F32).T.astype(BF16)


def _fox_prep_call(p, fl, b_f_pad, q_gain, k_gain, batch, ts):
    m = p.shape[0]
    d = p.shape[1] // 4
    n_heads = d // HEAD_DIM
    seq = m // batch
    nst = seq // ts
    lanes = fl.shape[1]
    return pl.pallas_call(
        _fox_prep_kernel,
        out_shape=(
            jax.ShapeDtypeStruct((m, 2 * d), BF16),
            jax.ShapeDtypeStruct((m, 2 * d), BF16),
            jax.ShapeDtypeStruct((batch, nst, n_heads, HEAD_DIM, ts), BF16),
        ),
        grid=(batch, nst),
        in_specs=[
            pl.BlockSpec((ts, d), lambda b, s: (b * nst + s, 0)),
            pl.BlockSpec((ts, d), lambda b, s: (b * nst + s, 1)),
            pl.BlockSpec((ts, d), lambda b, s: (b * nst + s, 2)),
            pl.BlockSpec((ts, lanes), lambda b, s: (b * nst + s, 0)),
            pl.BlockSpec((1, lanes), lambda b, s: (0, 0)),
            pl.BlockSpec((1, HEAD_DIM), lambda b, s: (0, 0)),
            pl.BlockSpec((1, HEAD_DIM), lambda b, s: (0, 0)),
        ],
        out_specs=(
            pl.BlockSpec((ts, 2 * d), lambda b, s: (b * nst + s, 0)),
            pl.BlockSpec((ts, 2 * d), lambda b, s: (b * nst + s, 0)),
            pl.BlockSpec((None, None, n_heads, HEAD_DIM, ts), lambda b, s: (b, s, 0, 0, 0)),
        ),
        scratch_shapes=[pltpu.VMEM((SUBLANES, lanes), F32)],
        compiler_params=_cparams(2),
        name="fox_prep",
    )(p, p, p, fl, b_f_pad, q_gain.reshape(1, HEAD_DIM), k_gain.reshape(1, HEAD_DIM))


def _fox_flash_kernel(q_ref, k_ref, vt_ref, g_ref, o_ref):
    t = q_ref.shape[0]
    _, hp, dh, _ = vt_ref.shape
    qi = pl.program_id(2)
    to_log2 = (1.0 / math.sqrt(dh)) * math.log2(math.e)
    nt = (((1,), (1,)), ((), ()))

    def scores(j, e):
        cols = slice(2 * e * dh, 2 * (e + 1) * dh)
        kj = k_ref[pl.ds(pl.multiple_of(j * t, t), t), cols]
        return lax.dot_general(kj, q_ref[:, cols], nt, preferred_element_type=F32)

    def step(j, carry, masked):
        raw = [scores(j, e) for e in range(hp)]
        out = []
        for e in range(hp):
            m_prev, l_prev, acc = carry[e]
            s = raw[e] * to_log2
            if masked:
                key = lax.broadcasted_iota(jnp.int32, (t, t), 0)
                qry = lax.broadcasted_iota(jnp.int32, (t, t), 1)
                s = jnp.where(key <= qry, s, -jnp.inf)
            m_new = jnp.maximum(m_prev, jnp.max(s, axis=0, keepdims=True))
            alpha = jnp.exp2(m_prev - m_new)
            p = jnp.exp2(s - m_new)
            l_new = alpha * l_prev + jnp.sum(p, axis=0, keepdims=True)
            acc = alpha * acc + jnp.dot(vt_ref[j, e], p.astype(BF16), preferred_element_type=F32)
            out.append((m_new, l_new, acc))
        return tuple(out)

    init = tuple((jnp.full((1, t), -jnp.inf, F32), jnp.zeros((1, t), F32), jnp.zeros((dh, t), F32))
                 for _ in range(hp))

    carry = lax.fori_loop(0, qi, functools.partial(step, masked=False), init)
    fin = step(qi, carry, True)
    for e in range(hp):
        _, l_fin, acc = fin[e]
        cols = slice(e * dh, (e + 1) * dh)
        o = (acc / l_fin).T
        o_ref[:, cols] = (o * jax.nn.sigmoid(g_ref[:, cols].astype(F32))).astype(BF16)


def _fox_flash_call(qa, ka, vt, p, batch, t, hp=8):
    m = qa.shape[0]
    n_heads = vt.shape[2]
    d = n_heads * HEAD_DIM
    seq = m // batch
    nq = seq // t
    hp = math.gcd(hp, n_heads)
    g_blk = 3 * n_heads // hp
    return pl.pallas_call(
        _fox_flash_kernel,
        out_shape=jax.ShapeDtypeStruct((m, d), BF16),
        grid=(batch, n_heads // hp, nq),
        in_specs=[
            pl.BlockSpec((t, 2 * hp * HEAD_DIM), lambda b, h, i: (b * nq + i, h)),
            pl.BlockSpec((seq, 2 * hp * HEAD_DIM), lambda b, h, i: (b, h)),
            pl.BlockSpec((None, nq, hp, HEAD_DIM, t), lambda b, h, i: (b, 0, h, 0, 0)),
            pl.BlockSpec((t, hp * HEAD_DIM), lambda b, h, i: (b * nq + i, g_blk + h)),
        ],
        out_specs=pl.BlockSpec((t, hp * HEAD_DIM), lambda b, h, i: (b * nq + i, h)),
        compiler_params=_cparams(3),
        name="fox_flash",
    )(qa, ka, vt, p)


def _hg_mid_kernel(q_ref, fz_ref, iv_ref, g_ref, lbl_ref, ng_ref, wout_ref, x_ref, mgate_ref, o_ref,
                   qin_scr, kin_scr, kdec_scr, dec_scr, st_scr, y_scr, *, layer):
    ts, d = q_ref.shape
    n_heads = d // HEAD_DIM
    n_chunks = ts // CHUNK

    @pl.when(pl.program_id(1) == 0)
    def _():
        st_scr[...] = jnp.zeros_like(st_scr)

    lg = lbl_ref[...]
    depth = lg.shape[0]
    mx = lg[0:1, :]
    for i in range(1, depth):
        mx = jnp.maximum(mx, lg[i:i + 1, :])
    es = [jnp.exp(lg[i:i + 1, :] - mx) for i in range(depth)]
    tot = es[0]
    for i in range(1, depth):
        tot = tot + es[i]
    lb = jnp.zeros_like(tot)
    for i in range(layer):
        lb = lb + es[i] / tot

    sig = jax.nn.sigmoid(fz_ref[...].astype(F32))
    logf = jnp.log(lb + (1.0 - lb) * sig)
    kk = (1.0 - lb) * (1.0 - sig)

    r = lax.broadcasted_iota(jnp.int32, (ts, ts), 0)
    c = lax.broadcasted_iota(jnp.int32, (ts, ts), 1)
    tril = jnp.where((c <= r) & (jnp.right_shift(c, CHUNK_SHIFT) == jnp.right_shift(r, CHUNK_SHIFT)), 1.0, 0.0).astype(BF16)
    bcum = _tril_dot(tril, logf)

    qin_scr[...] = (q_ref[...].astype(F32) * jnp.exp(bcum)).astype(BF16)
    kin_scr[...] = (kk * jnp.exp(-bcum)).astype(BF16)
    for ci in range(n_chunks):
        rows = slice(ci * CHUNK, (ci + 1) * CHUNK)
        b_last = bcum[(ci + 1) * CHUNK - 1:(ci + 1) * CHUNK, :]
        kdec_scr[rows, :] = (kk[rows, :] * jnp.exp(b_last - bcum[rows, :])).astype(BF16)
        dec_scr[ci] = jnp.broadcast_to(jnp.exp(b_last), (SUBLANES, d))

    rr = lax.broadcasted_iota(jnp.int32, (CHUNK, CHUNK), 0)
    cc = lax.broadcasted_iota(jnp.int32, (CHUNK, CHUNK), 1)
    causal = cc <= rr
    nt = (((1,), (1,)), ((), ()))
    tn = (((0,), (0,)), ((), ()))

    def chunk_body(ci, carry):
        r0 = ci * CHUNK
        rows = pl.ds(r0, CHUNK)
        heads = [slice(h * HEAD_DIM, (h + 1) * HEAD_DIM) for h in range(n_heads)]
        qh = [qin_scr[rows, c] for c in heads]
        vh = [iv_ref[rows, c].astype(BF16) for c in heads]
        st = [st_scr[h] for h in range(n_heads)]
        sc = [lax.dot_general(qh[h], kin_scr[rows, c], nt, preferred_element_type=F32)
              for h, c in enumerate(heads)]
        o_inter = [lax.dot_general(qh[h], st[h].astype(BF16), nt, preferred_element_type=F32)
                   for h in range(n_heads)]
        u_t = [lax.dot_general(vh[h], kdec_scr[rows, c], tn, preferred_element_type=F32)
               for h, c in enumerate(heads)]
        for h, c in enumerate(heads):
            scm = jnp.where(causal, sc[h], 0.0).astype(BF16)
            o = jnp.dot(scm, vh[h], preferred_element_type=F32) + o_inter[h]
            st_scr[h] = st[h] * dec_scr[ci][0:1, c] + u_t[h]
            ms = jnp.mean(o * o, axis=-1, keepdims=True)
            on = (o * lax.rsqrt(ms + EPS)) * ng_ref[...]
            gh = g_ref[rows, c].astype(F32)
            y_scr[rows, c] = (on * (gh * jax.nn.sigmoid(gh))).astype(BF16)
        return carry

    for ci in range(n_chunks):
        chunk_body(ci, 0)
    _project_residual(y_scr[...], wout_ref, x_ref, mgate_ref, o_ref)


def _hg_mid_call(p, lb_logits, norm_gain, w_out, widx, x, mod_r, layer, batch, ts=256):
    m, four_d = p.shape
    d = four_d // 4
    n_heads = d // HEAD_DIM
    seq = m // batch
    ts = min(ts, seq)
    nst = seq // ts
    return pl.pallas_call(
        functools.partial(_hg_mid_kernel, layer=layer),
        out_shape=jax.ShapeDtypeStruct((m, d), F32),
        grid=(batch, nst),
        in_specs=[
            pl.BlockSpec((ts, d), lambda b, s: (b * nst + s, 0)),
            pl.BlockSpec((ts, d), lambda b, s: (b * nst + s, 1)),
            pl.BlockSpec((ts, d), lambda b, s: (b * nst + s, 2)),
            pl.BlockSpec((ts, d), lambda b, s: (b * nst + s, 3)),
            pl.BlockSpec(lb_logits.shape, lambda b, s: (0, 0)),
            pl.BlockSpec((1, HEAD_DIM), lambda b, s: (0, 0)),
        ] + _res_specs(w_out, widx, ts, nst, d, layer * batch * 9 + 5),
        out_specs=pl.BlockSpec((ts, d), lambda b, s: (b * nst + s, 0)),
        scratch_shapes=[
            pltpu.VMEM((ts, d), BF16),
            pltpu.VMEM((ts, d), BF16),
            pltpu.VMEM((ts, d), BF16),
            pltpu.VMEM((ts // CHUNK, SUBLANES, d), F32),
            pltpu.VMEM((n_heads, HEAD_DIM, HEAD_DIM), F32),
            pltpu.VMEM((ts, d), BF16),
        ],
        compiler_params=_cparams(2),
        name="hgrn2_core",
    )(p, p, p, p, lb_logits, norm_gain.reshape(1, HEAD_DIM), w_out, x, mod_r)


def kernel(x, c, ada_w, ada_b, norm_g, ffn_w_gate, ffn_w_up, ffn_w_down, rg_w_in, rg_conv_w, rg_conv_b, rg_w_r, rg_b_r, rg_w_i, rg_b_i, rg_lam, rg_w_out, sc_w_in, sc_conv_w, sc_w_out, fox_w_in, fox_b_f, fox_q_gain, fox_k_gain, fox_w_out, hg_w_in, hg_lb_logits, hg_norm_gain, hg_w_out):
    batch, seq, d = x.shape
    depth = ada_w.shape[0]
    n_heads = d // HEAD_DIM
    n_mixers = 4
    mid_dtype = BF16

    mod = _ada_call(c, ada_w, ada_b)
    mod_r = mod.reshape(depth * batch * 9, 1, d)
    ng_r = norm_g.reshape(depth * 3, 1, d)

    ffn_w = (ffn_w_gate, ffn_w_up, ffn_w_down)
    w_cur = None

    def ffn(xf, w_cur, k, sub):
        nxt = (ffn_w, (k + 1) // 2, (k + 1) % 2) if k + 1 < 2 * depth else None
        if w_cur is None:
            return _ffn_call(xf, mod_r, ng_r, ffn_w, k // 2, sub, batch, nxt, w_index=(0, 0), tf=256)
        return _ffn_call(xf, mod_r, ng_r, w_cur, k // 2, sub, batch, nxt)

    xf = x.reshape(batch * seq, d)
    for i in range(depth):
        xf, w_cur = ffn(xf, w_cur, 2 * i, 0)
        m, j = i % n_mixers, i // n_mixers
        if m == 0:
            p = _mm_mod_call(xf, mod_r, ng_r, rg_w_in, j, rg_w_in.shape[2], i, batch, mid_dtype)
            w_ri = jnp.concatenate([rg_w_r[j], rg_w_i[j]], axis=-1).astype(BF16)
            xf = _rg_mid_call(p, rg_conv_w[j], rg_conv_b[j], w_ri, rg_b_r[j], rg_b_i[j], rg_lam[j],
                              rg_w_out.astype(BF16), j, xf, mod_r, i, batch)
        elif m == 1:
            p = _mm_mod_call(xf, mod_r, ng_r, sc_w_in, j, sc_w_in.shape[2], i, batch, mid_dtype)
            xf = _sc_mid_call(p, sc_conv_w[j], sc_w_out.astype(BF16), j, xf, mod_r, i, batch)
        elif m == 2:
            w_in_t = jnp.swapaxes(fox_w_in, 1, 2)
            w_fl = jnp.pad(w_in_t[j, 4 * d:], ((0, HEAD_DIM - n_heads), (0, 0)))
            p, fl = _mm_mod_call(xf, mod_r, ng_r, w_in_t, j, 4 * d, i, batch, mid_dtype, w_t=True, w_side=w_fl)
            b_f_pad = jnp.pad(fox_b_f[j], (0, HEAD_DIM - n_heads)).reshape(1, HEAD_DIM)
            t_attn = min(512, seq)
            qa, ka, vt = _fox_prep_call(p, fl, b_f_pad, fox_q_gain[j], fox_k_gain[j], batch, t_attn)
            y = _fox_flash_call(qa, ka, vt, p, batch, t_attn)
            xf = _mm_res_call(y, fox_w_out.astype(BF16), j, xf, mod_r, i, batch)
        else:
            p = _mm_mod_call(xf, mod_r, ng_r, hg_w_in, j, hg_w_in.shape[2], i, batch, mid_dtype)
            xf = _hg_mid_call(p, hg_lb_logits, hg_norm_gain[j], hg_w_out.astype(BF16), j, xf, mod_r, i, batch)
        xf, w_cur = ffn(xf, w_cur, 2 * i + 1, 2)
    return xf.reshape(batch, seq, d)
```

```python
import functools
import math

import jax
import jax.numpy as jnp
from jax import lax
from jax.experimental import pallas as pl
from jax.experimental.pallas import tpu as pltpu

F32 = jnp.float32
BF16 = jnp.bfloat16

EPS = 1e-6
HEAD_DIM = 128
CHUNK = 64
CHUNK_SHIFT = 6
RG_C = 8.0
N_MIXERS = 4

LANES = 128
SUBLANES = 8
VMEM_BYTES = 64 * 1024 * 1024
VMEM_LIMIT = VMEM_BYTES - 4 * 1024 * 1024

ADA_COLS = 1024
FFN_ROWS = 1024
FFN_HIDDEN = 512
FFN_HIDDEN_F32 = 256
ROW_CHUNK = 512
PROJ_ROWS = 2048
PROJ_COLS = 512
OUT_ROWS = 512
CORE_ROWS = 256
ATTN_TILE = 512
ATTN_HEADS = 8


def _cparams(n_axes):
    return pltpu.CompilerParams(dimension_semantics=("arbitrary",) * n_axes,
                                vmem_limit_bytes=VMEM_LIMIT)


def _tile(n, preferred):
    if n <= preferred:
        return n
    t = preferred - preferred % LANES
    while n % t:
        t -= LANES
    return t


def _modulate(x, g, shift, scale):
    ms = jnp.mean(x * x, axis=-1, keepdims=True)
    return (x * lax.rsqrt(ms + EPS)) * (g * (1.0 + scale)) + shift


def _log_sigmoid(x):
    return jnp.minimum(x, 0.0) - jnp.log1p(jnp.exp(-jnp.abs(x)))


def _split3(x):
    hi = x.astype(BF16)
    r1 = x - hi.astype(F32)
    mid = r1.astype(BF16)
    lo = (r1 - mid.astype(F32)).astype(BF16)
    return hi, mid, lo


def _tril_dot(tril, x):
    hi, mid, lo = _split3(x)
    acc = jnp.dot(tril, hi, preferred_element_type=F32)
    acc = acc + jnp.dot(tril, mid, preferred_element_type=F32)
    return acc + jnp.dot(tril, lo, preferred_element_type=F32)


def _ada_kernel(c_ref, w_ref, b_ref, o_ref):
    c = c_ref[...]
    sc = (c * jax.nn.sigmoid(c)).astype(BF16)
    o_ref[...] = jnp.dot(sc, w_ref[...].astype(BF16), preferred_element_type=F32) + b_ref[...]


def _ada_call(c, ada_w, ada_b, tn=ADA_COLS):
    depth, d, n = ada_w.shape
    b = c.shape[0]
    rows = -(-b // SUBLANES) * SUBLANES
    c_pad = jnp.pad(c, ((0, rows - b), (0, 0)))
    tn = _tile(n, tn)
    out = pl.pallas_call(
        _ada_kernel,
        out_shape=jax.ShapeDtypeStruct((depth, rows, n), F32),
        grid=(depth, n // tn),
        in_specs=[
            pl.BlockSpec((rows, d), lambda l, j: (0, 0)),
            pl.BlockSpec((None, d, tn), lambda l, j: (l, 0, j)),
            pl.BlockSpec((None, 1, tn), lambda l, j: (l, 0, j)),
        ],
        out_specs=pl.BlockSpec((None, rows, tn), lambda l, j: (l, 0, j)),
        compiler_params=_cparams(2),
        name="ada",
    )(c_pad, ada_w, ada_b.reshape(depth, 1, n))
    return out[:, :b]


def _row_chunks(tm, rc):
    return [slice(r, r + rc) for r in range(0, tm, rc)]


def _ffn_kernel(x_ref, shift_ref, scale_ref, gate_ref, g_ref, wg_ref, wu_ref, wd_ref, *rest, rc, nf, n_ahead):
    ahead_in = rest[:n_ahead]
    o_ref = rest[n_ahead]
    ahead_out = rest[n_ahead + 1:2 * n_ahead + 1]
    h_scr = rest[2 * n_ahead + 1]
    f = pl.program_id(1)
    chunks = _row_chunks(x_ref.shape[0], rc)

    def weights():
        for src, dst in zip(ahead_in, ahead_out):
            dst[...] = src[...].astype(BF16)
        return wg_ref[...].astype(BF16), wu_ref[...].astype(BF16), wd_ref[...].astype(BF16)

    def swiglu(h, wg, wu, wd):
        gg = jnp.dot(h, wg, preferred_element_type=F32)
        uu = jnp.dot(h, wu, preferred_element_type=F32)
        a = ((gg * jax.nn.sigmoid(gg)) * uu).astype(BF16)
        return jnp.dot(a, wd, preferred_element_type=F32)

    def run(first, last):
        w = weights()
        half_gate = 0.5 * gate_ref[...] if last else None
        for rows in chunks:
            if first:
                h = _modulate(x_ref[rows, :], g_ref[...], shift_ref[...], scale_ref[...]).astype(BF16)
                h_scr[rows, :] = h
                acc = swiglu(h, *w)
            else:
                acc = o_ref[rows, :] + swiglu(h_scr[rows, :], *w)
            if last:
                acc = x_ref[rows, :] + half_gate * acc
            o_ref[rows, :] = acc

    if nf == 1:
        run(True, True)
    else:
        pl.when(f == 0)(functools.partial(run, True, False))
        pl.when((f > 0) & (f < nf - 1))(functools.partial(run, False, False))
        pl.when(f == nf - 1)(functools.partial(run, False, True))


def _mod_spec(d, base, tiles_per_batch, col_axis=None, tn=None):
    width = d if tn is None else tn

    def index(*ids):
        col = 0 if col_axis is None else ids[col_axis]
        return (base + 9 * (ids[0] // tiles_per_batch), 0, col)

    return pl.BlockSpec((None, 1, width), index)


def _ffn_call(x, mod_r, ng_r, w, layer, sub, batch, nxt=None, w_index=None, tm=FFN_ROWS, tf=FFN_HIDDEN,
              rc=ROW_CHUNK):
    wg, wu, wd = w
    m, d = x.shape
    f_dim = wg.shape[-1]
    tm = min(tm, m // batch)
    rc = min(rc, tm)
    tf = _tile(f_dim, tf)
    tpb = (m // batch) // tm
    ni, nf = m // tm, f_dim // tf
    base = layer * batch * 9 + sub * 3
    if w_index is None:
        w_specs = [pl.BlockSpec((d, tf), lambda i, f: (0, f)), pl.BlockSpec((d, tf), lambda i, f: (0, f)),
                   pl.BlockSpec((tf, d), lambda i, f: (f, 0))]
    else:
        wl, ww = w_index
        w_specs = [pl.BlockSpec((None, None, d, tf), lambda i, f: (wl, ww, 0, f)),
                   pl.BlockSpec((None, None, d, tf), lambda i, f: (wl, ww, 0, f)),
                   pl.BlockSpec((None, None, tf, d), lambda i, f: (wl, ww, f, 0))]
    in_specs = [
        pl.BlockSpec((tm, d), lambda i, f: (i, 0)),
        _mod_spec(d, base + 0, tpb),
        _mod_spec(d, base + 1, tpb),
        _mod_spec(d, base + 2, tpb),
        pl.BlockSpec((None, 1, d), lambda i, f: (layer * 3 + sub, 0, 0)),
    ] + w_specs
    operands = [x, mod_r, mod_r, mod_r, ng_r, wg, wu, wd]
    out_shape = [jax.ShapeDtypeStruct((m, d), F32)]
    out_specs = [pl.BlockSpec((tm, d), lambda i, f: (i, 0))]
    if nxt is not None:
        (ng, nu, nd), nl, nw = nxt
        dr = d // ni
        in_specs += [
            pl.BlockSpec((None, None, dr, tf), lambda i, f: (nl, nw, i, f)),
            pl.BlockSpec((None, None, dr, tf), lambda i, f: (nl, nw, i, f)),
            pl.BlockSpec((None, None, tf, dr), lambda i, f: (nl, nw, f, i)),
        ]
        operands += [ng, nu, nd]
        out_shape += [jax.ShapeDtypeStruct((d, f_dim), BF16), jax.ShapeDtypeStruct((d, f_dim), BF16),
                      jax.ShapeDtypeStruct((f_dim, d), BF16)]
        out_specs += [pl.BlockSpec((dr, tf), lambda i, f: (i, f)), pl.BlockSpec((dr, tf), lambda i, f: (i, f)),
                      pl.BlockSpec((tf, dr), lambda i, f: (f, i))]
    outs = pl.pallas_call(
        functools.partial(_ffn_kernel, rc=rc, nf=nf, n_ahead=0 if nxt is None else 3),
        out_shape=out_shape,
        grid=(ni, nf),
        in_specs=in_specs,
        out_specs=out_specs,
        scratch_shapes=[pltpu.VMEM((tm, d), BF16)],
        compiler_params=_cparams(2),
        name="ffn",
    )(*operands)
    return outs[0], tuple(outs[1:])


def _mm_mod_kernel(x_ref, shift_ref, scale_ref, g_ref, w_ref, *rest, rc, w_t, side):
    if side:
        ws_ref, o_ref, os_ref, h_scr = rest
    else:
        o_ref, h_scr = rest
    j = pl.program_id(1)
    chunks = _row_chunks(x_ref.shape[0], rc)
    nt = (((1,), (1,)), ((), ()))
    dims = nt if w_t else (((1,), (0,)), ((), ()))

    def project(h, w):
        return lax.dot_general(h, w, dims, preferred_element_type=F32).astype(o_ref.dtype)

    @pl.when(j == 0)
    def _():
        w = w_ref[...].astype(BF16)
        for rows in chunks:
            h = _modulate(x_ref[rows, :], g_ref[...], shift_ref[...], scale_ref[...]).astype(BF16)
            h_scr[rows, :] = h
            o_ref[rows, :] = project(h, w)
            if side:
                os_ref[rows, :] = lax.dot_general(h, ws_ref[...].astype(BF16), nt, preferred_element_type=F32)

    @pl.when(j > 0)
    def _():
        w = w_ref[...].astype(BF16)
        for rows in chunks:
            o_ref[rows, :] = project(h_scr[rows, :], w)


def _mm_mod_call(x, mod_r, ng_r, w, widx, n_out, layer, batch, out_dtype, tm=PROJ_ROWS, tn=PROJ_COLS,
                 rc=ROW_CHUNK, w_t=False,
                 w_side=None):
    m, d = x.shape
    tm = min(tm, m // batch)
    rc = min(rc, tm)
    tn = _tile(n_out, tn)
    tpb = (m // batch) // tm
    base = layer * batch * 9 + 3
    if w_t:
        w_spec = pl.BlockSpec((None, tn, d), lambda i, j: (widx, j, 0))
    else:
        w_spec = pl.BlockSpec((None, d, tn), lambda i, j: (widx, 0, j))
    in_specs = [
        pl.BlockSpec((tm, d), lambda i, j: (i, 0)),
        _mod_spec(d, base + 0, tpb),
        _mod_spec(d, base + 1, tpb),
        pl.BlockSpec((None, 1, d), lambda i, j: (layer * 3 + 1, 0, 0)),
        w_spec,
    ]
    operands = [x, mod_r, mod_r, ng_r, w]
    out_shape = [jax.ShapeDtypeStruct((m, n_out), out_dtype)]
    out_specs = [pl.BlockSpec((tm, tn), lambda i, j: (i, j))]
    if w_side is not None:
        n_side = w_side.shape[0]
        in_specs.append(pl.BlockSpec((n_side, d), lambda i, j: (0, 0)))
        operands.append(w_side)
        out_shape.append(jax.ShapeDtypeStruct((m, n_side), F32))
        out_specs.append(pl.BlockSpec((tm, n_side), lambda i, j: (i, 0)))
    outs = pl.pallas_call(
        functools.partial(_mm_mod_kernel, rc=rc, w_t=w_t, side=w_side is not None),
        out_shape=out_shape,
        grid=(m // tm, n_out // tn),
        in_specs=in_specs,
        out_specs=out_specs,
        scratch_shapes=[pltpu.VMEM((tm, d), BF16)],
        compiler_params=_cparams(2),
        name="in_proj",
    )(*operands)
    return outs[0] if w_side is None else tuple(outs)


def _mm_res_kernel(y_ref, w_ref, x_ref, gate_ref, o_ref):
    acc = jnp.dot(y_ref[...], w_ref[...], preferred_element_type=F32)
    o_ref[...] = x_ref[...] + gate_ref[...] * acc


def _mm_res_call(y, w, widx, x, mod_r, layer, batch, tm=OUT_ROWS):
    m, k = y.shape
    d = x.shape[1]
    tm = min(tm, m // batch)
    tpb = (m // batch) // tm
    base = layer * batch * 9 + 3 + 2
    return pl.pallas_call(
        _mm_res_kernel,
        out_shape=jax.ShapeDtypeStruct((m, d), F32),
        grid=(m // tm,),
        in_specs=[
            pl.BlockSpec((tm, k), lambda i: (i, 0)),
            pl.BlockSpec((None, k, d), lambda i: (widx, 0, 0)),
            pl.BlockSpec((tm, d), lambda i: (i, 0)),
            _mod_spec(d, base, tpb),
        ],
        out_specs=pl.BlockSpec((tm, d), lambda i: (i, 0)),
        compiler_params=_cparams(1),
        name="out_proj",
    )(y, w, x, mod_r)


def _res_specs(w_out, widx, ts, nst, d, gate_base):
    k = w_out.shape[1]
    return [
        pl.BlockSpec((None, k, d), lambda b, s: (widx, 0, 0)),
        pl.BlockSpec((ts, d), lambda b, s: (b * nst + s, 0)),
        pl.BlockSpec((None, 1, d), lambda b, s: (gate_base + 9 * b, 0, 0)),
    ]


def _project_residual(y, wout_ref, x_ref, mgate_ref, o_ref):
    o_ref[...] = x_ref[...] + mgate_ref[...] * jnp.dot(y, wout_ref[...], preferred_element_type=F32)


def _shift_in_tile(ext_scr, cur, ts):
    s = pl.program_id(1)

    @pl.when(s == 0)
    def _():
        ext_scr[0:SUBLANES, :] = jnp.zeros((SUBLANES, ext_scr.shape[1]), F32)

    @pl.when(s > 0)
    def _():
        ext_scr[0:SUBLANES, :] = ext_scr[ts:ts + SUBLANES, :]

    ext_scr[SUBLANES:ts + SUBLANES, :] = cur


def _causal_conv_from_ext(ext_scr, cw_ref, ts):
    kw = cw_ref.shape[0]
    ext = ext_scr[...]
    acc = None
    for k in range(kw):
        shift = kw - 1 - k
        tap = ext if shift == 0 else pltpu.roll(ext, shift, 0)
        term = cw_ref[k:k + 1, :] * tap[SUBLANES:, :]
        acc = term if acc is None else acc + term
    return acc


def _rg_mid_kernel(gate_ref, xb_ref, cw_ref, cb_ref, wri_ref, br_ref, bi_ref, lam_ref,
                   wout_ref, x_ref, mgate_ref, o_ref, ext_scr, a_scr, b_scr, hs_scr, h_scr):
    ts, width = xb_ref.shape
    n_blocks, blk, _ = wri_ref.shape

    @pl.when(pl.program_id(1) == 0)
    def _():
        h_scr[...] = jnp.zeros_like(h_scr)

    _shift_in_tile(ext_scr, xb_ref[...].astype(F32), ts)
    xc = _causal_conv_from_ext(ext_scr, cw_ref, ts) + cb_ref[...]

    for g in range(n_blocks):
        cols = slice(g * blk, (g + 1) * blk)
        xg = xc[:, cols]
        ri = jnp.dot(xg.astype(BF16), wri_ref[g], preferred_element_type=F32)
        r = jax.nn.sigmoid(ri[:, :blk] + br_ref[:, cols])
        ig = jax.nn.sigmoid(ri[:, blk:] + bi_ref[:, cols])
        log_a = (RG_C * r) * _log_sigmoid(lam_ref[:, cols])
        a = jnp.exp(log_a)
        a_scr[:, cols] = a
        z = 1.0 - a * a
        b_scr[:, cols] = jnp.where(z > 0.0, z * lax.rsqrt(z), z) * (ig * xg)

    row = lax.broadcasted_iota(jnp.int32, (SUBLANES, width), 0)

    def body(i, h):
        r0 = i * SUBLANES
        a = a_scr[pl.ds(r0, SUBLANES), :]
        b = b_scr[pl.ds(r0, SUBLANES), :]
        for sh in (1, 2, 4):
            keep = row >= sh
            a_prev = jnp.where(keep, pltpu.roll(a, sh, 0), 1.0)
            b_prev = jnp.where(keep, pltpu.roll(b, sh, 0), 0.0)
            b = a * b_prev + b
            a = a * a_prev
        hs = a * h + b
        hs_scr[pl.ds(r0, SUBLANES), :] = hs
        return jnp.broadcast_to(hs[SUBLANES - 1:SUBLANES, :], (SUBLANES, width))

    h = h_scr[...]
    for i in range(ts // SUBLANES):
        h = body(i, h)
    h_scr[...] = h
    y = (hs_scr[...] * jax.nn.gelu(gate_ref[...].astype(F32))).astype(BF16)
    _project_residual(y, wout_ref, x_ref, mgate_ref, o_ref)


def _rg_mid_call(p, conv_w, conv_b, w_ri, b_r, b_i, lam, w_out, widx, x, mod_r, layer, batch, ts=CORE_ROWS):
    m, two_w = p.shape
    width = two_w // 2
    d = x.shape[1]
    seq = m // batch
    ts = min(ts, seq)
    nst = seq // ts
    n_blocks, blk, _ = w_ri.shape
    full = lambda shape: pl.BlockSpec(shape, lambda b, s: (0,) * len(shape))
    return pl.pallas_call(
        _rg_mid_kernel,
        out_shape=jax.ShapeDtypeStruct((m, d), F32),
        grid=(batch, nst),
        in_specs=[
            pl.BlockSpec((ts, width), lambda b, s: (b * nst + s, 0)),
            pl.BlockSpec((ts, width), lambda b, s: (b * nst + s, 1)),
            full(conv_w.shape),
            full((1, width)),
            full((n_blocks, blk, 2 * blk)),
            full((1, width)),
            full((1, width)),
            full((1, width)),
        ] + _res_specs(w_out, widx, ts, nst, d, layer * batch * 9 + 5),
        out_specs=pl.BlockSpec((ts, d), lambda b, s: (b * nst + s, 0)),
        scratch_shapes=[
            pltpu.VMEM((ts + SUBLANES, width), F32),
            pltpu.VMEM((ts, width), F32),
            pltpu.VMEM((ts, width), F32),
            pltpu.VMEM((ts, width), F32),
            pltpu.VMEM((SUBLANES, width), F32),
        ],
        compiler_params=_cparams(2),
        name="rglru_core",
    )(p, p, conv_w, conv_b.reshape(1, width), w_ri, b_r.reshape(1, width), b_i.reshape(1, width),
      lam.reshape(1, width), w_out, x, mod_r)


def _sc_mid_kernel(bg_ref, cg_ref, xv_ref, cw_ref, wout_ref, x_ref, mgate_ref, o_ref, ext_scr):
    ts = bg_ref.shape[0]
    _shift_in_tile(ext_scr, cg_ref[...].astype(F32) * xv_ref[...].astype(F32), ts)
    conv = _causal_conv_from_ext(ext_scr, cw_ref, ts)
    y = (bg_ref[...].astype(F32) * conv).astype(BF16)
    _project_residual(y, wout_ref, x_ref, mgate_ref, o_ref)


def _sc_mid_call(p, conv_w, w_out, widx, x, mod_r, layer, batch, ts=CORE_ROWS):
    m, three_d = p.shape
    d = three_d // 3
    seq = m // batch
    ts = min(ts, seq)
    nst = seq // ts
    return pl.pallas_call(
        _sc_mid_kernel,
        out_shape=jax.ShapeDtypeStruct((m, d), F32),
        grid=(batch, nst),
        in_specs=[
            pl.BlockSpec((ts, d), lambda b, s: (b * nst + s, 0)),
            pl.BlockSpec((ts, d), lambda b, s: (b * nst + s, 1)),
            pl.BlockSpec((ts, d), lambda b, s: (b * nst + s, 2)),
            pl.BlockSpec(conv_w.shape, lambda b, s: (0, 0)),
        ] + _res_specs(w_out, widx, ts, nst, d, layer * batch * 9 + 5),
        out_specs=pl.BlockSpec((ts, d), lambda b, s: (b * nst + s, 0)),
        scratch_shapes=[pltpu.VMEM((ts + SUBLANES, d), F32)],
        compiler_params=_cparams(2),
        name="shortconv_core",
    )(p, p, p, conv_w, w_out, x, mod_r)


def _fox_prep_kernel(q_ref, k_ref, v_ref, fl_ref, bf_ref, qg_ref, kg_ref, qa_ref, ka_ref, vt_ref, carry_scr):
    ts, d = q_ref.shape

    @pl.when(pl.program_id(1) == 0)
    def _():
        carry_scr[...] = jnp.zeros_like(carry_scr)

    logf = _log_sigmoid(fl_ref[...] + bf_ref[...])
    r = lax.broadcasted_iota(jnp.int32, (ts, ts), 0)
    c = lax.broadcasted_iota(jnp.int32, (ts, ts), 1)
    tril = jnp.where(c <= r, 1.0, 0.0).astype(BF16)
    cum = _tril_dot(tril, logf) + carry_scr[0:1, :]
    carry_scr[...] = jnp.broadcast_to(cum[ts - 1:ts, :], carry_scr.shape)
    f_over_scale = cum * math.sqrt(HEAD_DIM)

    lane = lax.broadcasted_iota(jnp.int32, (ts, HEAD_DIM), 1)
    for h in range(d // HEAD_DIM):
        cols = slice(h * HEAD_DIM, (h + 1) * HEAD_DIM)
        hi, mid, lo = (part.astype(F32) for part in _split3(f_over_scale[:, h:h + 1]))
        parts = jnp.where((lane == 0) | (lane == 3), hi, jnp.where((lane == 1) | (lane == 4), mid, lo))
        q_aug = jnp.where(lane < 3, parts, jnp.where(lane < 6, 1.0, 0.0))
        k_aug = jnp.where(lane < 3, 1.0, jnp.where(lane < 6, -parts, 0.0))
        for src, gain, aug, dst in ((q_ref, qg_ref, q_aug, qa_ref), (k_ref, kg_ref, k_aug, ka_ref)):
            t = src[:, cols].astype(F32)
            ms = jnp.mean(t * t, axis=-1, keepdims=True)
            dst[:, 2 * h * HEAD_DIM:(2 * h + 1) * HEAD_DIM] = ((t * lax.rsqrt(ms + EPS)) * gain[...]).astype(BF16)
            dst[:, (2 * h + 1) * HEAD_DIM:(2 * h + 2) * HEAD_DIM] = aug.astype(BF16)
        vt_ref[h] = v_ref[:, cols].astype(F32).T.astype(BF16)


def _fox_prep_call(p, fl, b_f_pad, q_gain, k_gain, batch, ts):
    m = p.shape[0]
    d = p.shape[1] // 4
    n_heads = d // HEAD_DIM
    seq = m // batch
    nst = seq // ts
    lanes = fl.shape[1]
    return pl.pallas_call(
        _fox_prep_kernel,
        out_shape=(
            jax.ShapeDtypeStruct((m, 2 * d), BF16),
            jax.ShapeDtypeStruct((m, 2 * d), BF16),
            jax.ShapeDtypeStruct((batch, nst, n_heads, HEAD_DIM, ts), BF16),
        ),
        grid=(batch, nst),
        in_specs=[
            pl.BlockSpec((ts, d), lambda b, s: (b * nst + s, 0)),
            pl.BlockSpec((ts, d), lambda b, s: (b * nst + s, 1)),
            pl.BlockSpec((ts, d), lambda b, s: (b * nst + s, 2)),
            pl.BlockSpec((ts, lanes), lambda b, s: (b * nst + s, 0)),
            pl.BlockSpec((1, lanes), lambda b, s: (0, 0)),
            pl.BlockSpec((1, HEAD_DIM), lambda b, s: (0, 0)),
            pl.BlockSpec((1, HEAD_DIM), lambda b, s: (0, 0)),
        ],
        out_specs=(
            pl.BlockSpec((ts, 2 * d), lambda b, s: (b * nst + s, 0)),
            pl.BlockSpec((ts, 2 * d), lambda b, s: (b * nst + s, 0)),
            pl.BlockSpec((None, None, n_heads, HEAD_DIM, ts), lambda b, s: (b, s, 0, 0, 0)),
        ),
        scratch_shapes=[pltpu.VMEM((SUBLANES, lanes), F32)],
        compiler_params=_cparams(2),
        name="fox_prep",
    )(p, p, p, fl, b_f_pad, q_gain.reshape(1, HEAD_DIM), k_gain.reshape(1, HEAD_DIM))


def _fox_flash_kernel(q_ref, k_ref, vt_ref, g_ref, o_ref):
    t = q_ref.shape[0]
    _, hp, dh, _ = vt_ref.shape
    qi = pl.program_id(2)
    to_log2 = (1.0 / math.sqrt(dh)) * math.log2(math.e)
    nt = (((1,), (1,)), ((), ()))

    def scores(j, e):
        cols = slice(2 * e * dh, 2 * (e + 1) * dh)
        kj = k_ref[pl.ds(pl.multiple_of(j * t, t), t), cols]
        return lax.dot_general(kj, q_ref[:, cols], nt, preferred_element_type=F32)

    def step(j, carry, masked):
        raw = [scores(j, e) for e in range(hp)]
        out = []
        for e in range(hp):
            m_prev, l_prev, acc = carry[e]
            s = raw[e] * to_log2
            if masked:
                key = lax.broadcasted_iota(jnp.int32, (t, t), 0)
                qry = lax.broadcasted_iota(jnp.int32, (t, t), 1)
                s = jnp.where(key <= qry, s, -jnp.inf)
            m_new = jnp.maximum(m_prev, jnp.max(s, axis=0, keepdims=True))
            alpha = jnp.exp2(m_prev - m_new)
            p = jnp.exp2(s - m_new)
            l_new = alpha * l_prev + jnp.sum(p, axis=0, keepdims=True)
            acc = alpha * acc + jnp.dot(vt_ref[j, e], p.astype(BF16), preferred_element_type=F32)
            out.append((m_new, l_new, acc))
        return tuple(out)

    init = tuple((jnp.full((1, t), -jnp.inf, F32), jnp.zeros((1, t), F32), jnp.zeros((dh, t), F32))
                 for _ in range(hp))

    carry = lax.fori_loop(0, qi, functools.partial(step, masked=False), init)
    fin = step(qi, carry, True)
    for e in range(hp):
        _, l_fin, acc = fin[e]
        cols = slice(e * dh, (e + 1) * dh)
        o = (acc / l_fin).T
        o_ref[:, cols] = (o * jax.nn.sigmoid(g_ref[:, cols].astype(F32))).astype(BF16)


def _fox_flash_call(qa, ka, vt, p, batch, t, hp=ATTN_HEADS):
    m = qa.shape[0]
    n_heads = vt.shape[2]
    d = n_heads * HEAD_DIM
    seq = m // batch
    nq = seq // t
    hp = math.gcd(hp, n_heads)
    g_blk = 3 * n_heads // hp
    return pl.pallas_call(
        _fox_flash_kernel,
        out_shape=jax.ShapeDtypeStruct((m, d), BF16),
        grid=(batch, n_heads // hp, nq),
        in_specs=[
            pl.BlockSpec((t, 2 * hp * HEAD_DIM), lambda b, h, i: (b * nq + i, h)),
            pl.BlockSpec((seq, 2 * hp * HEAD_DIM), lambda b, h, i: (b, h)),
            pl.BlockSpec((None, nq, hp, HEAD_DIM, t), lambda b, h, i: (b, 0, h, 0, 0)),
            pl.BlockSpec((t, hp * HEAD_DIM), lambda b, h, i: (b * nq + i, g_blk + h)),
        ],
        out_specs=pl.BlockSpec((t, hp * HEAD_DIM), lambda b, h, i: (b * nq + i, h)),
        compiler_params=_cparams(3),
        name="fox_flash",
    )(qa, ka, vt, p)


def _hg_mid_kernel(q_ref, fz_ref, iv_ref, g_ref, lbl_ref, ng_ref, wout_ref, x_ref, mgate_ref, o_ref,
                   qin_scr, kin_scr, kdec_scr, dec_scr, st_scr, y_scr, *, layer):
    ts, d = q_ref.shape
    n_heads = d // HEAD_DIM
    n_chunks = ts // CHUNK

    @pl.when(pl.program_id(1) == 0)
    def _():
        st_scr[...] = jnp.zeros_like(st_scr)

    lg = lbl_ref[...]
    depth = lg.shape[0]
    mx = lg[0:1, :]
    for i in range(1, depth):
        mx = jnp.maximum(mx, lg[i:i + 1, :])
    es = [jnp.exp(lg[i:i + 1, :] - mx) for i in range(depth)]
    tot = es[0]
    for i in range(1, depth):
        tot = tot + es[i]
    lb = jnp.zeros_like(tot)
    for i in range(layer):
        lb = lb + es[i] / tot

    sig = jax.nn.sigmoid(fz_ref[...].astype(F32))
    logf = jnp.log(lb + (1.0 - lb) * sig)
    kk = (1.0 - lb) * (1.0 - sig)

    r = lax.broadcasted_iota(jnp.int32, (ts, ts), 0)
    c = lax.broadcasted_iota(jnp.int32, (ts, ts), 1)
    tril = jnp.where((c <= r) & (jnp.right_shift(c, CHUNK_SHIFT) == jnp.right_shift(r, CHUNK_SHIFT)), 1.0, 0.0).astype(BF16)
    bcum = _tril_dot(tril, logf)

    qin_scr[...] = (q_ref[...].astype(F32) * jnp.exp(bcum)).astype(BF16)
    kin_scr[...] = (kk * jnp.exp(-bcum)).astype(BF16)
    for ci in range(n_chunks):
        rows = slice(ci * CHUNK, (ci + 1) * CHUNK)
        b_last = bcum[(ci + 1) * CHUNK - 1:(ci + 1) * CHUNK, :]
        kdec_scr[rows, :] = (kk[rows, :] * jnp.exp(b_last - bcum[rows, :])).astype(BF16)
        dec_scr[ci] = jnp.broadcast_to(jnp.exp(b_last), (SUBLANES, d))

    rr = lax.broadcasted_iota(jnp.int32, (CHUNK, CHUNK), 0)
    cc = lax.broadcasted_iota(jnp.int32, (CHUNK, CHUNK), 1)
    causal = cc <= rr
    nt = (((1,), (1,)), ((), ()))
    tn = (((0,), (0,)), ((), ()))

    def chunk_body(ci, carry):
        r0 = ci * CHUNK
        rows = pl.ds(r0, CHUNK)
        heads = [slice(h * HEAD_DIM, (h + 1) * HEAD_DIM) for h in range(n_heads)]
        qh = [qin_scr[rows, c] for c in heads]
        vh = [iv_ref[rows, c].astype(BF16) for c in heads]
        st = [st_scr[h] for h in range(n_heads)]
        sc = [lax.dot_general(qh[h], kin_scr[rows, c], nt, preferred_element_type=F32)
              for h, c in enumerate(heads)]
        o_inter = [lax.dot_general(qh[h], st[h].astype(BF16), nt, preferred_element_type=F32)
                   for h in range(n_heads)]
        u_t = [lax.dot_general(vh[h], kdec_scr[rows, c], tn, preferred_element_type=F32)
               for h, c in enumerate(heads)]
        for h, c in enumerate(heads):
            scm = jnp.where(causal, sc[h], 0.0).astype(BF16)
            o = jnp.dot(scm, vh[h], preferred_element_type=F32) + o_inter[h]
            st_scr[h] = st[h] * dec_scr[ci][0:1, c] + u_t[h]
            ms = jnp.mean(o * o, axis=-1, keepdims=True)
            on = (o * lax.rsqrt(ms + EPS)) * ng_ref[...]
            gh = g_ref[rows, c].astype(F32)
            y_scr[rows, c] = (on * (gh * jax.nn.sigmoid(gh))).astype(BF16)
        return carry

    for ci in range(n_chunks):
        chunk_body(ci, 0)
    _project_residual(y_scr[...], wout_ref, x_ref, mgate_ref, o_ref)


def _hg_mid_call(p, lb_logits, norm_gain, w_out, widx, x, mod_r, layer, batch, ts=CORE_ROWS):
    m, four_d = p.shape
    d = four_d // 4
    n_heads = d // HEAD_DIM
    seq = m // batch
    ts = min(ts, seq)
    nst = seq // ts
    return pl.pallas_call(
        functools.partial(_hg_mid_kernel, layer=layer),
        out_shape=jax.ShapeDtypeStruct((m, d), F32),
        grid=(batch, nst),
        in_specs=[
            pl.BlockSpec((ts, d), lambda b, s: (b * nst + s, 0)),
            pl.BlockSpec((ts, d), lambda b, s: (b * nst + s, 1)),
            pl.BlockSpec((ts, d), lambda b, s: (b * nst + s, 2)),
            pl.BlockSpec((ts, d), lambda b, s: (b * nst + s, 3)),
            pl.BlockSpec(lb_logits.shape, lambda b, s: (0, 0)),
            pl.BlockSpec((1, HEAD_DIM), lambda b, s: (0, 0)),
        ] + _res_specs(w_out, widx, ts, nst, d, layer * batch * 9 + 5),
        out_specs=pl.BlockSpec((ts, d), lambda b, s: (b * nst + s, 0)),
        scratch_shapes=[
            pltpu.VMEM((ts, d), BF16),
            pltpu.VMEM((ts, d), BF16),
            pltpu.VMEM((ts, d), BF16),
            pltpu.VMEM((ts // CHUNK, SUBLANES, d), F32),
            pltpu.VMEM((n_heads, HEAD_DIM, HEAD_DIM), F32),
            pltpu.VMEM((ts, d), BF16),
        ],
        compiler_params=_cparams(2),
        name="hgrn2_core",
    )(p, p, p, p, lb_logits, norm_gain.reshape(1, HEAD_DIM), w_out, x, mod_r)


def kernel(x, c, ada_w, ada_b, norm_g, ffn_w_gate, ffn_w_up, ffn_w_down, rg_w_in, rg_conv_w, rg_conv_b, rg_w_r, rg_b_r, rg_w_i, rg_b_i, rg_lam, rg_w_out, sc_w_in, sc_conv_w, sc_w_out, fox_w_in, fox_b_f, fox_q_gain, fox_k_gain, fox_w_out, hg_w_in, hg_lb_logits, hg_norm_gain, hg_w_out):
    batch, seq, d = x.shape
    depth = ada_w.shape[0]
    n_heads = d // HEAD_DIM
    assert d % HEAD_DIM == 0 and seq % CHUNK == 0 and seq % SUBLANES == 0
    mid_dtype = BF16

    mod = _ada_call(c, ada_w, ada_b)
    mod_r = mod.reshape(depth * batch * 9, 1, d)
    ng_r = norm_g.reshape(depth * 3, 1, d)

    ffn_w = (ffn_w_gate, ffn_w_up, ffn_w_down)
    w_cur = None

    def ffn(xf, w_cur, k, sub):
        nxt = (ffn_w, (k + 1) // 2, (k + 1) % 2) if k + 1 < 2 * depth else None
        if w_cur is None:
            return _ffn_call(xf, mod_r, ng_r, ffn_w, k // 2, sub, batch, nxt, w_index=(0, 0),
                             tf=FFN_HIDDEN_F32)
        return _ffn_call(xf, mod_r, ng_r, w_cur, k // 2, sub, batch, nxt)

    xf = x.reshape(batch * seq, d)
    for i in range(depth):
        xf, w_cur = ffn(xf, w_cur, 2 * i, 0)
        m, j = i % N_MIXERS, i // N_MIXERS
        if m == 0:
            p = _mm_mod_call(xf, mod_r, ng_r, rg_w_in, j, rg_w_in.shape[2], i, batch, mid_dtype)
            w_ri = jnp.concatenate([rg_w_r[j], rg_w_i[j]], axis=-1).astype(BF16)
            xf = _rg_mid_call(p, rg_conv_w[j], rg_conv_b[j], w_ri, rg_b_r[j], rg_b_i[j], rg_lam[j],
                              rg_w_out.astype(BF16), j, xf, mod_r, i, batch)
        elif m == 1:
            p = _mm_mod_call(xf, mod_r, ng_r, sc_w_in, j, sc_w_in.shape[2], i, batch, mid_dtype)
            xf = _sc_mid_call(p, sc_conv_w[j], sc_w_out.astype(BF16), j, xf, mod_r, i, batch)
        elif m == 2:
            w_in_t = jnp.swapaxes(fox_w_in, 1, 2)
            w_fl = jnp.pad(w_in_t[j, 4 * d:], ((0, LANES - n_heads), (0, 0)))
            p, fl = _mm_mod_call(xf, mod_r, ng_r, w_in_t, j, 4 * d, i, batch, mid_dtype, w_t=True, w_side=w_fl)
            b_f_pad = jnp.pad(fox_b_f[j], (0, LANES - n_heads)).reshape(1, LANES)
            t_attn = min(ATTN_TILE, seq)
            qa, ka, vt = _fox_prep_call(p, fl, b_f_pad, fox_q_gain[j], fox_k_gain[j], batch, t_attn)
            y = _fox_flash_call(qa, ka, vt, p, batch, t_attn)
            xf = _mm_res_call(y, fox_w_out.astype(BF16), j, xf, mod_r, i, batch)
        else:
            p = _mm_mod_call(xf, mod_r, ng_r, hg_w_in, j, hg_w_in.shape[2], i, batch, mid_dtype)
            xf = _hg_mid_call(p, hg_lb_logits, hg_norm_gain[j], hg_w_out.astype(BF16), j, xf, mod_r, i, batch)
        xf, w_cur = ffn(xf, w_cur, 2 * i + 1, 2)
    return xf.reshape(batch, seq, d)
```

```python
import functools
import math

import jax
import jax.numpy as jnp
from jax import lax
from jax.experimental import pallas as pl
from jax.experimental.pallas import tpu as pltpu

F32 = jnp.float32
BF16 = jnp.bfloat16

EPS = 1e-6
HEAD_DIM = 128
CHUNK = 64
CHUNK_SHIFT = 6
RG_C = 8.0
N_MIXERS = 4

LANES = 128
SUBLANES = 8
VMEM_BYTES = 64 * 1024 * 1024
VMEM_LIMIT = VMEM_BYTES - 4 * 1024 * 1024

ADA_COLS = 1024
FFN_ROWS = 1024
FFN_HIDDEN = 512
FFN_HIDDEN_F32 = 256
ROW_CHUNK = 512
PROJ_ROWS = 2048
PROJ_COLS = 512
OUT_ROWS = 512
CORE_ROWS = 256
ATTN_TILE = 512
ATTN_HEADS = 8


def _cparams(n_axes):
    return pltpu.CompilerParams(dimension_semantics=("arbitrary",) * n_axes,
                                vmem_limit_bytes=VMEM_LIMIT)


def _tile(n, preferred):
    if n <= preferred:
        return n
    t = preferred - preferred % LANES
    while n % t:
        t -= LANES
    return t


def _modulate(x, g, shift, scale):
    ms = jnp.mean(x * x, axis=-1, keepdims=True)
    return (x * lax.rsqrt(ms + EPS)) * (g * (1.0 + scale)) + shift


def _log_sigmoid(x):
    return jnp.minimum(x, 0.0) - jnp.log1p(jnp.exp(-jnp.abs(x)))


def _split3(x):
    hi = x.astype(BF16)
    r1 = x - hi.astype(F32)
    mid = r1.astype(BF16)
    lo = (r1 - mid.astype(F32)).astype(BF16)
    return hi, mid, lo


def _tril_dot(tril, x):
    hi, mid, lo = _split3(x)
    acc = jnp.dot(tril, hi, preferred_element_type=F32)
    acc = acc + jnp.dot(tril, mid, preferred_element_type=F32)
    return acc + jnp.dot(tril, lo, preferred_element_type=F32)


def _ada_kernel(c_ref, w_ref, b_ref, o_ref):
    c = c_ref[...]
    sc = (c * jax.nn.sigmoid(c)).astype(BF16)
    o_ref[...] = jnp.dot(sc, w_ref[...].astype(BF16), preferred_element_type=F32) + b_ref[...]


def _ada_call(c, ada_w, ada_b, tn=ADA_COLS):
    depth, d, n = ada_w.shape
    b = c.shape[0]
    rows = -(-b // SUBLANES) * SUBLANES
    c_pad = jnp.pad(c, ((0, rows - b), (0, 0)))
    tn = _tile(n, tn)
    out = pl.pallas_call(
        _ada_kernel,
        out_shape=jax.ShapeDtypeStruct((depth, rows, n), F32),
        grid=(depth, n // tn),
        in_specs=[
            pl.BlockSpec((rows, d), lambda l, j: (0, 0)),
            pl.BlockSpec((None, d, tn), lambda l, j: (l, 0, j)),
            pl.BlockSpec((None, 1, tn), lambda l, j: (l, 0, j)),
        ],
        out_specs=pl.BlockSpec((None, rows, tn), lambda l, j: (l, 0, j)),
        compiler_params=_cparams(2),
        name="ada",
    )(c_pad, ada_w, ada_b.reshape(depth, 1, n))
    return out[:, :b]


def _row_chunks(tm, rc):
    return [slice(r, r + rc) for r in range(0, tm, rc)]


def _ffn_kernel(x_ref, shift_ref, scale_ref, gate_ref, g_ref, wg_ref, wu_ref, wd_ref, *rest, rc, nf, n_ahead):
    ahead_in = rest[:n_ahead]
    o_ref = rest[n_ahead]
    ahead_out = rest[n_ahead + 1:2 * n_ahead + 1]
    h_scr = rest[2 * n_ahead + 1]
    f = pl.program_id(1)
    chunks = _row_chunks(x_ref.shape[0], rc)

    def weights():
        for src, dst in zip(ahead_in, ahead_out):
            dst[...] = src[...].astype(BF16)
        return wg_ref[...].astype(BF16), wu_ref[...].astype(BF16), wd_ref[...].astype(BF16)

    def swiglu(h, wg, wu, wd):
        gg = jnp.dot(h, wg, preferred_element_type=F32)
        uu = jnp.dot(h, wu, preferred_element_type=F32)
        a = ((gg * jax.nn.sigmoid(gg)) * uu).astype(BF16)
        return jnp.dot(a, wd, preferred_element_type=F32)

    def run(first, last):
        w = weights()
        half_gate = 0.5 * gate_ref[...] if last else None
        for rows in chunks:
            if first:
                h = _modulate(x_ref[rows, :], g_ref[...], shift_ref[...], scale_ref[...]).astype(BF16)
                h_scr[rows, :] = h
                acc = swiglu(h, *w)
            else:
                acc = o_ref[rows, :] + swiglu(h_scr[rows, :], *w)
            if last:
                acc = x_ref[rows, :] + half_gate * acc
            o_ref[rows, :] = acc

    if nf == 1:
        run(True, True)
    else:
        pl.when(f == 0)(functools.partial(run, True, False))
        pl.when((f > 0) & (f < nf - 1))(functools.partial(run, False, False))
        pl.when(f == nf - 1)(functools.partial(run, False, True))


def _mod_spec(d, base, tiles_per_batch, col_axis=None, tn=None):
    width = d if tn is None else tn

    def index(*ids):
        col = 0 if col_axis is None else ids[col_axis]
        return (base + 9 * (ids[0] // tiles_per_batch), 0, col)

    return pl.BlockSpec((None, 1, width), index)


def _ffn_call(x, mod_r, ng_r, w, layer, sub, batch, nxt=None, w_index=None, tm=FFN_ROWS, tf=FFN_HIDDEN,
              rc=ROW_CHUNK):
    wg, wu, wd = w
    m, d = x.shape
    f_dim = wg.shape[-1]
    tm = min(tm, m // batch)
    rc = min(rc, tm)
    tf = _tile(f_dim, tf)
    tpb = (m // batch) // tm
    ni, nf = m // tm, f_dim // tf
    base = layer * batch * 9 + sub * 3
    if w_index is None:
        w_specs = [pl.BlockSpec((d, tf), lambda i, f: (0, f)), pl.BlockSpec((d, tf), lambda i, f: (0, f)),
                   pl.BlockSpec((tf, d), lambda i, f: (f, 0))]
    else:
        wl, ww = w_index
        w_specs = [pl.BlockSpec((None, None, d, tf), lambda i, f: (wl, ww, 0, f)),
                   pl.BlockSpec((None, None, d, tf), lambda i, f: (wl, ww, 0, f)),
                   pl.BlockSpec((None, None, tf, d), lambda i, f: (wl, ww, f, 0))]
    in_specs = [
        pl.BlockSpec((tm, d), lambda i, f: (i, 0)),
        _mod_spec(d, base + 0, tpb),
        _mod_spec(d, base + 1, tpb),
        _mod_spec(d, base + 2, tpb),
        pl.BlockSpec((None, 1, d), lambda i, f: (layer * 3 + sub, 0, 0)),
    ] + w_specs
    operands = [x, mod_r, mod_r, mod_r, ng_r, wg, wu, wd]
    out_shape = [jax.ShapeDtypeStruct((m, d), F32)]
    out_specs = [pl.BlockSpec((tm, d), lambda i, f: (i, 0))]
    if nxt is not None:
        (ng, nu, nd), nl, nw = nxt
        dr = d // ni
        in_specs += [
            pl.BlockSpec((None, None, dr, tf), lambda i, f: (nl, nw, i, f)),
            pl.BlockSpec((None, None, dr, tf), lambda i, f: (nl, nw, i, f)),
            pl.BlockSpec((None, None, tf, dr), lambda i, f: (nl, nw, f, i)),
        ]
        operands += [ng, nu, nd]
        out_shape += [jax.ShapeDtypeStruct((d, f_dim), BF16), jax.ShapeDtypeStruct((d, f_dim), BF16),
                      jax.ShapeDtypeStruct((f_dim, d), BF16)]
        out_specs += [pl.BlockSpec((dr, tf), lambda i, f: (i, f)), pl.BlockSpec((dr, tf), lambda i, f: (i, f)),
                      pl.BlockSpec((tf, dr), lambda i, f: (f, i))]
    outs = pl.pallas_call(
        functools.partial(_ffn_kernel, rc=rc, nf=nf, n_ahead=0 if nxt is None else 3),
        out_shape=out_shape,
        grid=(ni, nf),
        in_specs=in_specs,
        out_specs=out_specs,
        scratch_shapes=[pltpu.VMEM((tm, d), BF16)],
        compiler_params=_cparams(2),
        name="ffn",
    )(*operands)
    return outs[0], tuple(outs[1:])


def _mm_mod_kernel(x_ref, shift_ref, scale_ref, g_ref, w_ref, *rest, rc, w_t, side):
    if side:
        ws_ref, o_ref, os_ref, h_scr = rest
    else:
        o_ref, h_scr = rest
    j = pl.program_id(1)
    chunks = _row_chunks(x_ref.shape[0], rc)
    nt = (((1,), (1,)), ((), ()))
    dims = nt if w_t else (((1,), (0,)), ((), ()))

    def project(h, w):
        return lax.dot_general(h, w, dims, preferred_element_type=F32).astype(o_ref.dtype)

    @pl.when(j == 0)
    def _():
        w = w_ref[...].astype(BF16)
        for rows in chunks:
            h = _modulate(x_ref[rows, :], g_ref[...], shift_ref[...], scale_ref[...]).astype(BF16)
            h_scr[rows, :] = h
            o_ref[rows, :] = project(h, w)
            if side:
                os_ref[rows, :] = lax.dot_general(h, ws_ref[...].astype(BF16), nt, preferred_element_type=F32)

    @pl.when(j > 0)
    def _():
        w = w_ref[...].astype(BF16)
        for rows in chunks:
            o_ref[rows, :] = project(h_scr[rows, :], w)


def _mm_mod_call(x, mod_r, ng_r, w, widx, n_out, layer, batch, out_dtype, tm=PROJ_ROWS, tn=PROJ_COLS,
                 rc=ROW_CHUNK, w_t=False,
                 w_side=None):
    m, d = x.shape
    tm = min(tm, m // batch)
    rc = min(rc, tm)
    tn = _tile(n_out, tn)
    tpb = (m // batch) // tm
    base = layer * batch * 9 + 3
    if w_t:
        w_spec = pl.BlockSpec((None, tn, d), lambda i, j: (widx, j, 0))
    else:
        w_spec = pl.BlockSpec((None, d, tn), lambda i, j: (widx, 0, j))
    in_specs = [
        pl.BlockSpec((tm, d), lambda i, j: (i, 0)),
        _mod_spec(d, base + 0, tpb),
        _mod_spec(d, base + 1, tpb),
        pl.BlockSpec((None, 1, d), lambda i, j: (layer * 3 + 1, 0, 0)),
        w_spec,
    ]
    operands = [x, mod_r, mod_r, ng_r, w]
    out_shape = [jax.ShapeDtypeStruct((m, n_out), out_dtype)]
    out_specs = [pl.BlockSpec((tm, tn), lambda i, j: (i, j))]
    if w_side is not None:
        n_side = w_side.shape[0]
        in_specs.append(pl.BlockSpec((n_side, d), lambda i, j: (0, 0)))
        operands.append(w_side)
        out_shape.append(jax.ShapeDtypeStruct((m, n_side), F32))
        out_specs.append(pl.BlockSpec((tm, n_side), lambda i, j: (i, 0)))
    outs = pl.pallas_call(
        functools.partial(_mm_mod_kernel, rc=rc, w_t=w_t, side=w_side is not None),
        out_shape=out_shape,
        grid=(m // tm, n_out // tn),
        in_specs=in_specs,
        out_specs=out_specs,
        scratch_shapes=[pltpu.VMEM((tm, d), BF16)],
        compiler_params=_cparams(2),
        name="in_proj",
    )(*operands)
    return outs[0] if w_side is None else tuple(outs)


def _mm_res_kernel(y_ref, w_ref, x_ref, gate_ref, o_ref):
    acc = jnp.dot(y_ref[...], w_ref[...], preferred_element_type=F32)
    o_ref[...] = x_ref[...] + gate_ref[...] * acc


def _mm_res_call(y, w, widx, x, mod_r, layer, batch, tm=OUT_ROWS):
    m, k = y.shape
    d = x.shape[1]
    tm = min(tm, m // batch)
    tpb = (m // batch) // tm
    base = layer * batch * 9 + 3 + 2
    return pl.pallas_call(
        _mm_res_kernel,
        out_shape=jax.ShapeDtypeStruct((m, d), F32),
        grid=(m // tm,),
        in_specs=[
            pl.BlockSpec((tm, k), lambda i: (i, 0)),
            pl.BlockSpec((None, k, d), lambda i: (widx, 0, 0)),
            pl.BlockSpec((tm, d), lambda i: (i, 0)),
            _mod_spec(d, base, tpb),
        ],
        out_specs=pl.BlockSpec((tm, d), lambda i: (i, 0)),
        compiler_params=_cparams(1),
        name="out_proj",
    )(y, w, x, mod_r)


def _res_specs(w_out, widx, ts, nst, d, gate_base):
    k = w_out.shape[1]
    return [
        pl.BlockSpec((None, k, d), lambda b, s: (widx, 0, 0), pipeline_mode=pl.Buffered(1)),
        pl.BlockSpec((ts, d), lambda b, s: (b * nst + s, 0)),
        pl.BlockSpec((None, 1, d), lambda b, s: (gate_base + 9 * b, 0, 0)),
    ]


def _w_out_scratch(w_out, d):
    return pltpu.VMEM((w_out.shape[1], d), BF16)


def _cast_w_out_once(wout_ref, wbf_scr):
    @pl.when((pl.program_id(0) == 0) & (pl.program_id(1) == 0))
    def _():
        wbf_scr[...] = wout_ref[...].astype(BF16)


def _project_residual(y, wbf_scr, x_ref, mgate_ref, o_ref):
    o_ref[...] = x_ref[...] + mgate_ref[...] * jnp.dot(y, wbf_scr[...], preferred_element_type=F32)


def _shift_in_tile(ext_scr, cur, ts):
    s = pl.program_id(1)

    @pl.when(s == 0)
    def _():
        ext_scr[0:SUBLANES, :] = jnp.zeros((SUBLANES, ext_scr.shape[1]), F32)

    @pl.when(s > 0)
    def _():
        ext_scr[0:SUBLANES, :] = ext_scr[ts:ts + SUBLANES, :]

    ext_scr[SUBLANES:ts + SUBLANES, :] = cur


def _causal_conv_from_ext(ext_scr, cw_ref, ts):
    kw = cw_ref.shape[0]
    ext = ext_scr[...]
    acc = None
    for k in range(kw):
        shift = kw - 1 - k
        tap = ext if shift == 0 else pltpu.roll(ext, shift, 0)
        term = cw_ref[k:k + 1, :] * tap[SUBLANES:, :]
        acc = term if acc is None else acc + term
    return acc


def _rg_mid_kernel(gate_ref, xb_ref, cw_ref, cb_ref, wri_ref, br_ref, bi_ref, lam_ref,
                   wout_ref, x_ref, mgate_ref, o_ref, ext_scr, a_scr, b_scr, hs_scr, h_scr, wbf_scr):
    ts, width = xb_ref.shape
    n_blocks, blk, _ = wri_ref.shape
    _cast_w_out_once(wout_ref, wbf_scr)

    @pl.when(pl.program_id(1) == 0)
    def _():
        h_scr[...] = jnp.zeros_like(h_scr)

    _shift_in_tile(ext_scr, xb_ref[...].astype(F32), ts)
    xc = _causal_conv_from_ext(ext_scr, cw_ref, ts) + cb_ref[...]

    for g in range(n_blocks):
        cols = slice(g * blk, (g + 1) * blk)
        xg = xc[:, cols]
        ri = jnp.dot(xg.astype(BF16), wri_ref[g], preferred_element_type=F32)
        r = jax.nn.sigmoid(ri[:, :blk] + br_ref[:, cols])
        ig = jax.nn.sigmoid(ri[:, blk:] + bi_ref[:, cols])
        log_a = (RG_C * r) * _log_sigmoid(lam_ref[:, cols])
        a = jnp.exp(log_a)
        a_scr[:, cols] = a
        z = 1.0 - a * a
        b_scr[:, cols] = jnp.where(z > 0.0, z * lax.rsqrt(z), z) * (ig * xg)

    row = lax.broadcasted_iota(jnp.int32, (SUBLANES, width), 0)

    def body(i, h):
        r0 = i * SUBLANES
        a = a_scr[pl.ds(r0, SUBLANES), :]
        b = b_scr[pl.ds(r0, SUBLANES), :]
        for sh in (1, 2, 4):
            keep = row >= sh
            a_prev = jnp.where(keep, pltpu.roll(a, sh, 0), 1.0)
            b_prev = jnp.where(keep, pltpu.roll(b, sh, 0), 0.0)
            b = a * b_prev + b
            a = a * a_prev
        hs = a * h + b
        hs_scr[pl.ds(r0, SUBLANES), :] = hs
        return jnp.broadcast_to(hs[SUBLANES - 1:SUBLANES, :], (SUBLANES, width))

    h = h_scr[...]
    for i in range(ts // SUBLANES):
        h = body(i, h)
    h_scr[...] = h
    y = (hs_scr[...] * jax.nn.gelu(gate_ref[...].astype(F32))).astype(BF16)
    _project_residual(y, wbf_scr, x_ref, mgate_ref, o_ref)


def _rg_mid_call(p, conv_w, conv_b, w_ri, b_r, b_i, lam, w_out, widx, x, mod_r, layer, batch, ts=CORE_ROWS):
    m, two_w = p.shape
    width = two_w // 2
    d = x.shape[1]
    seq = m // batch
    ts = min(ts, seq)
    nst = seq // ts
    n_blocks, blk, _ = w_ri.shape
    full = lambda shape: pl.BlockSpec(shape, lambda b, s: (0,) * len(shape))
    return pl.pallas_call(
        _rg_mid_kernel,
        out_shape=jax.ShapeDtypeStruct((m, d), F32),
        grid=(batch, nst),
        in_specs=[
            pl.BlockSpec((ts, width), lambda b, s: (b * nst + s, 0)),
            pl.BlockSpec((ts, width), lambda b, s: (b * nst + s, 1)),
            full(conv_w.shape),
            full((1, width)),
            full((n_blocks, blk, 2 * blk)),
            full((1, width)),
            full((1, width)),
            full((1, width)),
        ] + _res_specs(w_out, widx, ts, nst, d, layer * batch * 9 + 5),
        out_specs=pl.BlockSpec((ts, d), lambda b, s: (b * nst + s, 0)),
        scratch_shapes=[
            pltpu.VMEM((ts + SUBLANES, width), F32),
            pltpu.VMEM((ts, width), F32),
            pltpu.VMEM((ts, width), F32),
            pltpu.VMEM((ts, width), F32),
            pltpu.VMEM((SUBLANES, width), F32),
            _w_out_scratch(w_out, d),
        ],
        compiler_params=_cparams(2),
        name="rglru_core",
    )(p, p, conv_w, conv_b.reshape(1, width), w_ri, b_r.reshape(1, width), b_i.reshape(1, width),
      lam.reshape(1, width), w_out, x, mod_r)


def _sc_mid_kernel(bg_ref, cg_ref, xv_ref, cw_ref, wout_ref, x_ref, mgate_ref, o_ref, ext_scr, wbf_scr):
    ts = bg_ref.shape[0]
    _cast_w_out_once(wout_ref, wbf_scr)
    _shift_in_tile(ext_scr, cg_ref[...].astype(F32) * xv_ref[...].astype(F32), ts)
    conv = _causal_conv_from_ext(ext_scr, cw_ref, ts)
    y = (bg_ref[...].astype(F32) * conv).astype(BF16)
    _project_residual(y, wbf_scr, x_ref, mgate_ref, o_ref)


def _sc_mid_call(p, conv_w, w_out, widx, x, mod_r, layer, batch, ts=CORE_ROWS):
    m, three_d = p.shape
    d = three_d // 3
    seq = m // batch
    ts = min(ts, seq)
    nst = seq // ts
    return pl.pallas_call(
        _sc_mid_kernel,
        out_shape=jax.ShapeDtypeStruct((m, d), F32),
        grid=(batch, nst),
        in_specs=[
            pl.BlockSpec((ts, d), lambda b, s: (b * nst + s, 0)),
            pl.BlockSpec((ts, d), lambda b, s: (b * nst + s, 1)),
            pl.BlockSpec((ts, d), lambda b, s: (b * nst + s, 2)),
            pl.BlockSpec(conv_w.shape, lambda b, s: (0, 0)),
        ] + _res_specs(w_out, widx, ts, nst, d, layer * batch * 9 + 5),
        out_specs=pl.BlockSpec((ts, d), lambda b, s: (b * nst + s, 0)),
        scratch_shapes=[pltpu.VMEM((ts + SUBLANES, d), F32), _w_out_scratch(w_out, d)],
        compiler_params=_cparams(2),
        name="shortconv_core",
    )(p, p, p, conv_w, w_out, x, mod_r)


def _fox_prep_kernel(q_ref, k_ref, v_ref, fl_ref, bf_ref, qg_ref, kg_ref, qa_ref, ka_ref, vt_ref, carry_scr):
    ts, d = q_ref.shape

    @pl.when(pl.program_id(1) == 0)
    def _():
        carry_scr[...] = jnp.zeros_like(carry_scr)

    logf = _log_sigmoid(fl_ref[...] + bf_ref[...])
    r = lax.broadcasted_iota(jnp.int32, (ts, ts), 0)
    c = lax.broadcasted_iota(jnp.int32, (ts, ts), 1)
    tril = jnp.where(c <= r, 1.0, 0.0).astype(BF16)
    cum = _tril_dot(tril, logf) + carry_scr[0:1, :]
    carry_scr[...] = jnp.broadcast_to(cum[ts - 1:ts, :], carry_scr.shape)
    f_over_scale = cum * math.sqrt(HEAD_DIM)

    lane = lax.broadcasted_iota(jnp.int32, (ts, HEAD_DIM), 1)
    for h in range(d // HEAD_DIM):
        cols = slice(h * HEAD_DIM, (h + 1) * HEAD_DIM)
        hi, mid, lo = (part.astype(F32) for part in _split3(f_over_scale[:, h:h + 1]))
        parts = jnp.where((lane == 0) | (lane == 3), hi, jnp.where((lane == 1) | (lane == 4), mid, lo))
        q_aug = jnp.where(lane < 3, parts, jnp.where(lane < 6, 1.0, 0.0))
        k_aug = jnp.where(lane < 3, 1.0, jnp.where(lane < 6, -parts, 0.0))
        for src, gain, aug, dst in ((q_ref, qg_ref, q_aug, qa_ref), (k_ref, kg_ref, k_aug, ka_ref)):
            t = src[:, cols].astype(F32)
            ms = jnp.mean(t * t, axis=-1, keepdims=True)
            dst[:, 2 * h * HEAD_DIM:(2 * h + 1) * HEAD_DIM] = ((t * lax.rsqrt(ms + EPS)) * gain[...]).astype(BF16)
            dst[:, (2 * h + 1) * HEAD_DIM:(2 * h + 2) * HEAD_DIM] = aug.astype(BF16)
        vt_ref[h] = v_ref[:, cols].astype(F32).T.astype(BF16)


def _fox_prep_call(p, fl, b_f_pad, q_gain, k_gain, batch, ts):
    m = p.shape[0]
    d = p.shape[1] // 4
    n_heads = d // HEAD_DIM
    seq = m // batch
    nst = seq // ts
    lanes = fl.shape[1]
    return pl.pallas_call(
        _fox_prep_kernel,
        out_shape=(
            jax.ShapeDtypeStruct((m, 2 * d), BF16),
            jax.ShapeDtypeStruct((m, 2 * d), BF16),
            jax.ShapeDtypeStruct((batch, nst, n_heads, HEAD_DIM, ts), BF16),
        ),
        grid=(batch, nst),
        in_specs=[
            pl.BlockSpec((ts, d), lambda b, s: (b * nst + s, 0)),
            pl.BlockSpec((ts, d), lambda b, s: (b * nst + s, 1)),
            pl.BlockSpec((ts, d), lambda b, s: (b * nst + s, 2)),
            pl.BlockSpec((ts, lanes), lambda b, s: (b * nst + s, 0)),
            pl.BlockSpec((1, lanes), lambda b, s: (0, 0)),
            pl.BlockSpec((1, HEAD_DIM), lambda b, s: (0, 0)),
            pl.BlockSpec((1, HEAD_DIM), lambda b, s: (0, 0)),
        ],
        out_specs=(
            pl.BlockSpec((ts, 2 * d), lambda b, s: (b * nst + s, 0)),
            pl.BlockSpec((ts, 2 * d), lambda b, s: (b * nst + s, 0)),
            pl.BlockSpec((None, None, n_heads, HEAD_DIM, ts), lambda b, s: (b, s, 0, 0, 0)),
        ),
        scratch_shapes=[pltpu.VMEM((SUBLANES, lanes), F32)],
        compiler_params=_cparams(2),
        name="fox_prep",
    )(p, p, p, fl, b_f_pad, q_gain.reshape(1, HEAD_DIM), k_gain.reshape(1, HEAD_DIM))


def _fox_flash_kernel(q_ref, k_ref, vt_ref, g_ref, o_ref):
    t = q_ref.shape[0]
    _, hp, dh, _ = vt_ref.shape
    qi = pl.program_id(2)
    to_log2 = (1.0 / math.sqrt(dh)) * math.log2(math.e)
    nt = (((1,), (1,)), ((), ()))

    def scores(j, e):
        cols = slice(2 * e * dh, 2 * (e + 1) * dh)
        kj = k_ref[pl.ds(pl.multiple_of(j * t, t), t), cols]
        return lax.dot_general(kj, q_ref[:, cols], nt, preferred_element_type=F32)

    def step(j, carry, masked):
        raw = [scores(j, e) for e in range(hp)]
        out = []
        for e in range(hp):
            m_prev, l_prev, acc = carry[e]
            s = raw[e] * to_log2
            if masked:
                key = lax.broadcasted_iota(jnp.int32, (t, t), 0)
                qry = lax.broadcasted_iota(jnp.int32, (t, t), 1)
                s = jnp.where(key <= qry, s, -jnp.inf)
            m_new = jnp.maximum(m_prev, jnp.max(s, axis=0, keepdims=True))
            alpha = jnp.exp2(m_prev - m_new)
            p = jnp.exp2(s - m_new)
            l_new = alpha * l_prev + jnp.sum(p, axis=0, keepdims=True)
            acc = alpha * acc + jnp.dot(vt_ref[j, e], p.astype(BF16), preferred_element_type=F32)
            out.append((m_new, l_new, acc))
        return tuple(out)

    init = tuple((jnp.full((1, t), -jnp.inf, F32), jnp.zeros((1, t), F32), jnp.zeros((dh, t), F32))
                 for _ in range(hp))

    carry = lax.fori_loop(0, qi, functools.partial(step, masked=False), init)
    fin = step(qi, carry, True)
    for e in range(hp):
        _, l_fin, acc = fin[e]
        cols = slice(e * dh, (e + 1) * dh)
        o = (acc / l_fin).T
        o_ref[:, cols] = (o * jax.nn.sigmoid(g_ref[:, cols].astype(F32))).astype(BF16)


def _fox_flash_call(qa, ka, vt, p, batch, t, hp=ATTN_HEADS):
    m = qa.shape[0]
    n_heads = vt.shape[2]
    d = n_heads * HEAD_DIM
    seq = m // batch
    nq = seq // t
    hp = math.gcd(hp, n_heads)
    g_blk = 3 * n_heads // hp
    return pl.pallas_call(
        _fox_flash_kernel,
        out_shape=jax.ShapeDtypeStruct((m, d), BF16),
        grid=(batch, n_heads // hp, nq),
        in_specs=[
            pl.BlockSpec((t, 2 * hp * HEAD_DIM), lambda b, h, i: (b * nq + i, h)),
            pl.BlockSpec((seq, 2 * hp * HEAD_DIM), lambda b, h, i: (b, h)),
            pl.BlockSpec((None, nq, hp, HEAD_DIM, t), lambda b, h, i: (b, 0, h, 0, 0)),
            pl.BlockSpec((t, hp * HEAD_DIM), lambda b, h, i: (b * nq + i, g_blk + h)),
        ],
        out_specs=pl.BlockSpec((t, hp * HEAD_DIM), lambda b, h, i: (b * nq + i, h)),
        compiler_params=_cparams(3),
        name="fox_flash",
    )(qa, ka, vt, p)


def _hg_mid_kernel(q_ref, fz_ref, iv_ref, g_ref, lbl_ref, ng_ref, wout_ref, x_ref, mgate_ref, o_ref,
                   qin_scr, kin_scr, kdec_scr, dec_scr, st_scr, y_scr, wbf_scr, *, layer):
    ts, d = q_ref.shape
    n_heads = d // HEAD_DIM
    n_chunks = ts // CHUNK
    _cast_w_out_once(wout_ref, wbf_scr)

    @pl.when(pl.program_id(1) == 0)
    def _():
        st_scr[...] = jnp.zeros_like(st_scr)

    lg = lbl_ref[...]
    depth = lg.shape[0]
    mx = lg[0:1, :]
    for i in range(1, depth):
        mx = jnp.maximum(mx, lg[i:i + 1, :])
    es = [jnp.exp(lg[i:i + 1, :] - mx) for i in range(depth)]
    tot = es[0]
    for i in range(1, depth):
        tot = tot + es[i]
    lb = jnp.zeros_like(tot)
    for i in range(layer):
        lb = lb + es[i] / tot

    sig = jax.nn.sigmoid(fz_ref[...].astype(F32))
    logf = jnp.log(lb + (1.0 - lb) * sig)
    kk = (1.0 - lb) * (1.0 - sig)

    r = lax.broadcasted_iota(jnp.int32, (ts, ts), 0)
    c = lax.broadcasted_iota(jnp.int32, (ts, ts), 1)
    tril = jnp.where((c <= r) & (jnp.right_shift(c, CHUNK_SHIFT) == jnp.right_shift(r, CHUNK_SHIFT)), 1.0, 0.0).astype(BF16)
    bcum = _tril_dot(tril, logf)

    qin_scr[...] = (q_ref[...].astype(F32) * jnp.exp(bcum)).astype(BF16)
    kin_scr[...] = (kk * jnp.exp(-bcum)).astype(BF16)
    for ci in range(n_chunks):
        rows = slice(ci * CHUNK, (ci + 1) * CHUNK)
        b_last = bcum[(ci + 1) * CHUNK - 1:(ci + 1) * CHUNK, :]
        kdec_scr[rows, :] = (kk[rows, :] * jnp.exp(b_last - bcum[rows, :])).astype(BF16)
        dec_scr[ci] = jnp.broadcast_to(jnp.exp(b_last), (SUBLANES, d))

    rr = lax.broadcasted_iota(jnp.int32, (CHUNK, CHUNK), 0)
    cc = lax.broadcasted_iota(jnp.int32, (CHUNK, CHUNK), 1)
    causal = cc <= rr
    nt = (((1,), (1,)), ((), ()))
    tn = (((0,), (0,)), ((), ()))

    def chunk_body(ci, carry):
        r0 = ci * CHUNK
        rows = pl.ds(r0, CHUNK)
        heads = [slice(h * HEAD_DIM, (h + 1) * HEAD_DIM) for h in range(n_heads)]
        qh = [qin_scr[rows, c] for c in heads]
        vh = [iv_ref[rows, c].astype(BF16) for c in heads]
        st = [st_scr[h] for h in range(n_heads)]
        sc = [lax.dot_general(qh[h], kin_scr[rows, c], nt, preferred_element_type=F32)
              for h, c in enumerate(heads)]
        o_inter = [lax.dot_general(qh[h], st[h].astype(BF16), nt, preferred_element_type=F32)
                   for h in range(n_heads)]
        u_t = [lax.dot_general(vh[h], kdec_scr[rows, c], tn, preferred_element_type=F32)
               for h, c in enumerate(heads)]
        for h, c in enumerate(heads):
            scm = jnp.where(causal, sc[h], 0.0).astype(BF16)
            o = jnp.dot(scm, vh[h], preferred_element_type=F32) + o_inter[h]
            st_scr[h] = st[h] * dec_scr[ci][0:1, c] + u_t[h]
            ms = jnp.mean(o * o, axis=-1, keepdims=True)
            on = (o * lax.rsqrt(ms + EPS)) * ng_ref[...]
            gh = g_ref[rows, c].astype(F32)
            y_scr[rows, c] = (on * (gh * jax.nn.sigmoid(gh))).astype(BF16)
        return carry

    for ci in range(n_chunks):
        chunk_body(ci, 0)
    _project_residual(y_scr[...], wbf_scr, x_ref, mgate_ref, o_ref)


def _hg_mid_call(p, lb_logits, norm_gain, w_out, widx, x, mod_r, layer, batch, ts=CORE_ROWS):
    m, four_d = p.shape
    d = four_d // 4
    n_heads = d // HEAD_DIM
    seq = m // batch
    ts = min(ts, seq)
    nst = seq // ts
    return pl.pallas_call(
        functools.partial(_hg_mid_kernel, layer=layer),
        out_shape=jax.ShapeDtypeStruct((m, d), F32),
        grid=(batch, nst),
        in_specs=[
            pl.BlockSpec((ts, d), lambda b, s: (b * nst + s, 0)),
            pl.BlockSpec((ts, d), lambda b, s: (b * nst + s, 1)),
            pl.BlockSpec((ts, d), lambda b, s: (b * nst + s, 2)),
            pl.BlockSpec((ts, d), lambda b, s: (b * nst + s, 3)),
            pl.BlockSpec(lb_logits.shape, lambda b, s: (0, 0)),
            pl.BlockSpec((1, HEAD_DIM), lambda b, s: (0, 0)),
        ] + _res_specs(w_out, widx, ts, nst, d, layer * batch * 9 + 5),
        out_specs=pl.BlockSpec((ts, d), lambda b, s: (b * nst + s, 0)),
        scratch_shapes=[
            pltpu.VMEM((ts, d), BF16),
            pltpu.VMEM((ts, d), BF16),
            pltpu.VMEM((ts, d), BF16),
            pltpu.VMEM((ts // CHUNK, SUBLANES, d), F32),
            pltpu.VMEM((n_heads, HEAD_DIM, HEAD_DIM), F32),
            pltpu.VMEM((ts, d), BF16),
            _w_out_scratch(w_out, d),
        ],
        compiler_params=_cparams(2),
        name="hgrn2_core",
    )(p, p, p, p, lb_logits, norm_gain.reshape(1, HEAD_DIM), w_out, x, mod_r)


def kernel(x, c, ada_w, ada_b, norm_g, ffn_w_gate, ffn_w_up, ffn_w_down, rg_w_in, rg_conv_w, rg_conv_b, rg_w_r, rg_b_r, rg_w_i, rg_b_i, rg_lam, rg_w_out, sc_w_in, sc_conv_w, sc_w_out, fox_w_in, fox_b_f, fox_q_gain, fox_k_gain, fox_w_out, hg_w_in, hg_lb_logits, hg_norm_gain, hg_w_out):
    batch, seq, d = x.shape
    depth = ada_w.shape[0]
    n_heads = d // HEAD_DIM
    assert d % HEAD_DIM == 0 and seq % CHUNK == 0 and seq % SUBLANES == 0
    mid_dtype = BF16

    mod = _ada_call(c, ada_w, ada_b)
    mod_r = mod.reshape(depth * batch * 9, 1, d)
    ng_r = norm_g.reshape(depth * 3, 1, d)

    ffn_w = (ffn_w_gate, ffn_w_up, ffn_w_down)
    w_cur = None

    def ffn(xf, w_cur, k, sub):
        nxt = (ffn_w, (k + 1) // 2, (k + 1) % 2) if k + 1 < 2 * depth else None
        if w_cur is None:
            return _ffn_call(xf, mod_r, ng_r, ffn_w, k // 2, sub, batch, nxt, w_index=(0, 0),
                             tf=FFN_HIDDEN_F32)
        return _ffn_call(xf, mod_r, ng_r, w_cur, k // 2, sub, batch, nxt)

    xf = x.reshape(batch * seq, d)
    for i in range(depth):
        xf, w_cur = ffn(xf, w_cur, 2 * i, 0)
        m, j = i % N_MIXERS, i // N_MIXERS
        if m == 0:
            p = _mm_mod_call(xf, mod_r, ng_r, rg_w_in, j, rg_w_in.shape[2], i, batch, mid_dtype)
            w_ri = jnp.concatenate([rg_w_r[j], rg_w_i[j]], axis=-1).astype(BF16)
            xf = _rg_mid_call(p, rg_conv_w[j], rg_conv_b[j], w_ri, rg_b_r[j], rg_b_i[j], rg_lam[j],
                              rg_w_out, j, xf, mod_r, i, batch)
        elif m == 1:
            p = _mm_mod_call(xf, mod_r, ng_r, sc_w_in, j, sc_w_in.shape[2], i, batch, mid_dtype)
            xf = _sc_mid_call(p, sc_conv_w[j], sc_w_out, j, xf, mod_r, i, batch)
        elif m == 2:
            w_in_t = jnp.swapaxes(fox_w_in, 1, 2)
            w_fl = jnp.pad(w_in_t[j, 4 * d:], ((0, LANES - n_heads), (0, 0)))
            p, fl = _mm_mod_call(xf, mod_r, ng_r, w_in_t, j, 4 * d, i, batch, mid_dtype, w_t=True, w_side=w_fl)
            b_f_pad = jnp.pad(fox_b_f[j], (0, LANES - n_heads)).reshape(1, LANES)
            t_attn = min(ATTN_TILE, seq)
            qa, ka, vt = _fox_prep_call(p, fl, b_f_pad, fox_q_gain[j], fox_k_gain[j], batch, t_attn)
            y = _fox_flash_call(qa, ka, vt, p, batch, t_attn)
            xf = _mm_res_call(y, fox_w_out.astype(BF16), j, xf, mod_r, i, batch)
        else:
            p = _mm_mod_call(xf, mod_r, ng_r, hg_w_in, j, hg_w_in.shape[2], i, batch, mid_dtype)
            xf = _hg_mid_call(p, hg_lb_logits, hg_norm_gain[j], hg_w_out, j, xf, mod_r, i, batch)
        xf, w_cur = ffn(xf, w_cur, 2 * i + 1, 2)
    return xf.reshape(batch, seq, d)
```

```python
import functools
import math

import jax
import jax.numpy as jnp
from jax import lax
from jax.experimental import pallas as pl
from jax.experimental.pallas import tpu as pltpu

F32 = jnp.float32
BF16 = jnp.bfloat16

EPS = 1e-6
HEAD_DIM = 128
CHUNK = 64
CHUNK_SHIFT = 6
RG_C = 8.0
N_MIXERS = 4

LANES = 128
SUBLANES = 8
VMEM_BYTES = 64 * 1024 * 1024
VMEM_LIMIT = VMEM_BYTES - 4 * 1024 * 1024

ADA_COLS = 2048
FFN_ROWS = 1024
FFN_HIDDEN = 512
FFN_HIDDEN_F32 = 256
ROW_CHUNK = 512
PROJ_ROWS = 2048
PROJ_COLS = 512
OUT_ROWS = 512
CORE_ROWS = 256
ATTN_TILE = 512
ATTN_HEADS = 8


def _cparams(n_axes):
    return pltpu.CompilerParams(dimension_semantics=("arbitrary",) * n_axes,
                                vmem_limit_bytes=VMEM_LIMIT)


def _tile(n, preferred):
    if n <= preferred:
        return n
    t = preferred - preferred % LANES
    while n % t:
        t -= LANES
    return t


def _modulate(x, g, shift, scale):
    ms = jnp.mean(x * x, axis=-1, keepdims=True)
    return (x * lax.rsqrt(ms + EPS)) * (g * (1.0 + scale)) + shift


def _log_sigmoid(x):
    return jnp.minimum(x, 0.0) - jnp.log1p(jnp.exp(-jnp.abs(x)))


def _split3(x):
    hi = x.astype(BF16)
    r1 = x - hi.astype(F32)
    mid = r1.astype(BF16)
    lo = (r1 - mid.astype(F32)).astype(BF16)
    return hi, mid, lo


def _tril_dot(tril, x):
    hi, mid, lo = _split3(x)
    acc = jnp.dot(tril, hi, preferred_element_type=F32)
    acc = acc + jnp.dot(tril, mid, preferred_element_type=F32)
    return acc + jnp.dot(tril, lo, preferred_element_type=F32)


def _ada_kernel(c_ref, w_ref, b_ref, o_ref):
    c = c_ref[...]
    sc = (c * jax.nn.sigmoid(c)).astype(BF16)
    o_ref[...] = jnp.dot(sc, w_ref[...].astype(BF16), preferred_element_type=F32) + b_ref[...]


def _ada_call(c, ada_w, ada_b, tn=ADA_COLS):
    depth, d, n = ada_w.shape
    b = c.shape[0]
    rows = -(-b // SUBLANES) * SUBLANES
    c_pad = jnp.pad(c, ((0, rows - b), (0, 0)))
    tn = _tile(n, tn)
    out = pl.pallas_call(
        _ada_kernel,
        out_shape=jax.ShapeDtypeStruct((depth, rows, n), F32),
        grid=(depth, n // tn),
        in_specs=[
            pl.BlockSpec((rows, d), lambda l, j: (0, 0)),
            pl.BlockSpec((None, d, tn), lambda l, j: (l, 0, j)),
            pl.BlockSpec((None, 1, tn), lambda l, j: (l, 0, j)),
        ],
        out_specs=pl.BlockSpec((None, rows, tn), lambda l, j: (l, 0, j)),
        compiler_params=_cparams(2),
        name="ada",
    )(c_pad, ada_w, ada_b.reshape(depth, 1, n))
    return out[:, :b]


def _row_chunks(tm, rc):
    return [slice(r, r + rc) for r in range(0, tm, rc)]


def _ffn_kernel(x_ref, shift_ref, scale_ref, gate_ref, g_ref, wg_ref, wu_ref, wd_ref, *rest, rc, nf, n_ahead):
    ahead_in = rest[:n_ahead]
    o_ref = rest[n_ahead]
    ahead_out = rest[n_ahead + 1:2 * n_ahead + 1]
    h_scr = rest[2 * n_ahead + 1]
    f = pl.program_id(1)
    chunks = _row_chunks(x_ref.shape[0], rc)

    def weights():
        for src, dst in zip(ahead_in, ahead_out):
            dst[...] = src[...].astype(BF16)
        return wg_ref[...].astype(BF16), wu_ref[...].astype(BF16), wd_ref[...].astype(BF16)

    def swiglu(h, wg, wu, wd):
        gg = jnp.dot(h, wg, preferred_element_type=F32)
        uu = jnp.dot(h, wu, preferred_element_type=F32)
        a = ((gg * jax.nn.sigmoid(gg)) * uu).astype(BF16)
        return jnp.dot(a, wd, preferred_element_type=F32)

    def run(first, last):
        w = weights()
        half_gate = 0.5 * gate_ref[...] if last else None
        for rows in chunks:
            if first:
                h = _modulate(x_ref[rows, :], g_ref[...], shift_ref[...], scale_ref[...]).astype(BF16)
                h_scr[rows, :] = h
                acc = swiglu(h, *w)
            else:
                acc = o_ref[rows, :] + swiglu(h_scr[rows, :], *w)
            if last:
                acc = x_ref[rows, :] + half_gate * acc
            o_ref[rows, :] = acc

    if nf == 1:
        run(True, True)
    else:
        pl.when(f == 0)(functools.partial(run, True, False))
        pl.when((f > 0) & (f < nf - 1))(functools.partial(run, False, False))
        pl.when(f == nf - 1)(functools.partial(run, False, True))


def _mod_spec(d, base, tiles_per_batch, col_axis=None, tn=None):
    width = d if tn is None else tn

    def index(*ids):
        col = 0 if col_axis is None else ids[col_axis]
        return (base + 9 * (ids[0] // tiles_per_batch), 0, col)

    return pl.BlockSpec((None, 1, width), index)


def _ffn_call(x, mod_r, ng_r, w, layer, sub, batch, nxt=None, w_index=None, tm=FFN_ROWS, tf=FFN_HIDDEN,
              rc=ROW_CHUNK):
    wg, wu, wd = w
    m, d = x.shape
    f_dim = wg.shape[-1]
    tm = min(tm, m // batch)
    rc = min(rc, tm)
    tf = _tile(f_dim, tf)
    tpb = (m // batch) // tm
    ni, nf = m // tm, f_dim // tf
    base = layer * batch * 9 + sub * 3
    if w_index is None:
        w_specs = [pl.BlockSpec((d, tf), lambda i, f: (0, f)), pl.BlockSpec((d, tf), lambda i, f: (0, f)),
                   pl.BlockSpec((tf, d), lambda i, f: (f, 0))]
    else:
        wl, ww = w_index
        w_specs = [pl.BlockSpec((None, None, d, tf), lambda i, f: (wl, ww, 0, f)),
                   pl.BlockSpec((None, None, d, tf), lambda i, f: (wl, ww, 0, f)),
                   pl.BlockSpec((None, None, tf, d), lambda i, f: (wl, ww, f, 0))]
    in_specs = [
        pl.BlockSpec((tm, d), lambda i, f: (i, 0)),
        _mod_spec(d, base + 0, tpb),
        _mod_spec(d, base + 1, tpb),
        _mod_spec(d, base + 2, tpb),
        pl.BlockSpec((None, 1, d), lambda i, f: (layer * 3 + sub, 0, 0)),
    ] + w_specs
    operands = [x, mod_r, mod_r, mod_r, ng_r, wg, wu, wd]
    out_shape = [jax.ShapeDtypeStruct((m, d), F32)]
    out_specs = [pl.BlockSpec((tm, d), lambda i, f: (i, 0))]
    if nxt is not None:
        (ng, nu, nd), nl, nw = nxt
        dr = d // ni
        in_specs += [
            pl.BlockSpec((None, None, dr, tf), lambda i, f: (nl, nw, i, f)),
            pl.BlockSpec((None, None, dr, tf), lambda i, f: (nl, nw, i, f)),
            pl.BlockSpec((None, None, tf, dr), lambda i, f: (nl, nw, f, i)),
        ]
        operands += [ng, nu, nd]
        out_shape += [jax.ShapeDtypeStruct((d, f_dim), BF16), jax.ShapeDtypeStruct((d, f_dim), BF16),
                      jax.ShapeDtypeStruct((f_dim, d), BF16)]
        out_specs += [pl.BlockSpec((dr, tf), lambda i, f: (i, f)), pl.BlockSpec((dr, tf), lambda i, f: (i, f)),
                      pl.BlockSpec((tf, dr), lambda i, f: (f, i))]
    outs = pl.pallas_call(
        functools.partial(_ffn_kernel, rc=rc, nf=nf, n_ahead=0 if nxt is None else 3),
        out_shape=out_shape,
        grid=(ni, nf),
        in_specs=in_specs,
        out_specs=out_specs,
        scratch_shapes=[pltpu.VMEM((tm, d), BF16)],
        compiler_params=_cparams(2),
        name="ffn",
    )(*operands)
    return outs[0], tuple(outs[1:])


def _mm_mod_kernel(x_ref, shift_ref, scale_ref, g_ref, w_ref, *rest, rc, w_t, side):
    if side:
        ws_ref, o_ref, os_ref, h_scr = rest
    else:
        o_ref, h_scr = rest
    j = pl.program_id(1)
    chunks = _row_chunks(x_ref.shape[0], rc)
    nt = (((1,), (1,)), ((), ()))
    dims = nt if w_t else (((1,), (0,)), ((), ()))

    def project(h, w):
        return lax.dot_general(h, w, dims, preferred_element_type=F32).astype(o_ref.dtype)

    @pl.when(j == 0)
    def _():
        w = w_ref[...].astype(BF16)
        for rows in chunks:
            h = _modulate(x_ref[rows, :], g_ref[...], shift_ref[...], scale_ref[...]).astype(BF16)
            h_scr[rows, :] = h
            o_ref[rows, :] = project(h, w)
            if side:
                os_ref[rows, :] = lax.dot_general(h, ws_ref[...].astype(BF16), nt, preferred_element_type=F32)

    @pl.when(j > 0)
    def _():
        w = w_ref[...].astype(BF16)
        for rows in chunks:
            o_ref[rows, :] = project(h_scr[rows, :], w)


def _mm_mod_call(x, mod_r, ng_r, w, widx, n_out, layer, batch, out_dtype, tm=PROJ_ROWS, tn=PROJ_COLS,
                 rc=ROW_CHUNK, w_t=False,
                 w_side=None):
    m, d = x.shape
    tm = min(tm, m // batch)
    rc = min(rc, tm)
    tn = _tile(n_out, tn)
    tpb = (m // batch) // tm
    base = layer * batch * 9 + 3
    if w_t:
        w_spec = pl.BlockSpec((None, tn, d), lambda i, j: (widx, j, 0))
    else:
        w_spec = pl.BlockSpec((None, d, tn), lambda i, j: (widx, 0, j))
    in_specs = [
        pl.BlockSpec((tm, d), lambda i, j: (i, 0)),
        _mod_spec(d, base + 0, tpb),
        _mod_spec(d, base + 1, tpb),
        pl.BlockSpec((None, 1, d), lambda i, j: (layer * 3 + 1, 0, 0)),
        w_spec,
    ]
    operands = [x, mod_r, mod_r, ng_r, w]
    out_shape = [jax.ShapeDtypeStruct((m, n_out), out_dtype)]
    out_specs = [pl.BlockSpec((tm, tn), lambda i, j: (i, j))]
    if w_side is not None:
        n_side = w_side.shape[0]
        in_specs.append(pl.BlockSpec((n_side, d), lambda i, j: (0, 0)))
        operands.append(w_side)
        out_shape.append(jax.ShapeDtypeStruct((m, n_side), F32))
        out_specs.append(pl.BlockSpec((tm, n_side), lambda i, j: (i, 0)))
    outs = pl.pallas_call(
        functools.partial(_mm_mod_kernel, rc=rc, w_t=w_t, side=w_side is not None),
        out_shape=out_shape,
        grid=(m // tm, n_out // tn),
        in_specs=in_specs,
        out_specs=out_specs,
        scratch_shapes=[pltpu.VMEM((tm, d), BF16)],
        compiler_params=_cparams(2),
        name="in_proj",
    )(*operands)
    return outs[0] if w_side is None else tuple(outs)


def _mm_res_kernel(y_ref, w_ref, x_ref, gate_ref, o_ref, wbf_scr):
    @pl.when(pl.program_id(0) == 0)
    def _():
        wbf_scr[...] = w_ref[...].astype(BF16)

    acc = jnp.dot(y_ref[...], wbf_scr[...], preferred_element_type=F32)
    o_ref[...] = x_ref[...] + gate_ref[...] * acc


def _mm_res_call(y, w, widx, x, mod_r, layer, batch, tm=OUT_ROWS):
    m, k = y.shape
    d = x.shape[1]
    tm = min(tm, m // batch)
    tpb = (m // batch) // tm
    base = layer * batch * 9 + 3 + 2
    return pl.pallas_call(
        _mm_res_kernel,
        out_shape=jax.ShapeDtypeStruct((m, d), F32),
        grid=(m // tm,),
        in_specs=[
            pl.BlockSpec((tm, k), lambda i: (i, 0)),
            pl.BlockSpec((None, k, d), lambda i: (widx, 0, 0), pipeline_mode=pl.Buffered(1)),
            pl.BlockSpec((tm, d), lambda i: (i, 0)),
            _mod_spec(d, base, tpb),
        ],
        out_specs=pl.BlockSpec((tm, d), lambda i: (i, 0)),
        scratch_shapes=[pltpu.VMEM((k, d), BF16)],
        compiler_params=_cparams(1),
        name="out_proj",
    )(y, w, x, mod_r)


def _res_specs(w_out, widx, ts, nst, d, gate_base):
    k = w_out.shape[1]
    return [
        pl.BlockSpec((None, k, d), lambda b, s: (widx, 0, 0), pipeline_mode=pl.Buffered(1)),
        pl.BlockSpec((ts, d), lambda b, s: (b * nst + s, 0)),
        pl.BlockSpec((None, 1, d), lambda b, s: (gate_base + 9 * b, 0, 0)),
    ]


def _w_out_scratch(w_out, d):
    return pltpu.VMEM((w_out.shape[1], d), BF16)


def _cast_w_out_once(wout_ref, wbf_scr):
    @pl.when((pl.program_id(0) == 0) & (pl.program_id(1) == 0))
    def _():
        wbf_scr[...] = wout_ref[...].astype(BF16)


def _project_residual(y, wbf_scr, x_ref, mgate_ref, o_ref):
    o_ref[...] = x_ref[...] + mgate_ref[...] * jnp.dot(y, wbf_scr[...], preferred_element_type=F32)


def _shift_in_tile(ext_scr, cur, ts):
    s = pl.program_id(1)

    @pl.when(s == 0)
    def _():
        ext_scr[0:SUBLANES, :] = jnp.zeros((SUBLANES, ext_scr.shape[1]), F32)

    @pl.when(s > 0)
    def _():
        ext_scr[0:SUBLANES, :] = ext_scr[ts:ts + SUBLANES, :]

    ext_scr[SUBLANES:ts + SUBLANES, :] = cur


def _causal_conv_from_ext(ext_scr, cw_ref, ts):
    kw = cw_ref.shape[0]
    ext = ext_scr[...]
    acc = None
    for k in range(kw):
        shift = kw - 1 - k
        tap = ext if shift == 0 else pltpu.roll(ext, shift, 0)
        term = cw_ref[k:k + 1, :] * tap[SUBLANES:, :]
        acc = term if acc is None else acc + term
    return acc


def _rg_mid_kernel(gate_ref, xb_ref, cw_ref, cb_ref, wri_ref, br_ref, bi_ref, lam_ref,
                   wout_ref, x_ref, mgate_ref, o_ref, ext_scr, a_scr, b_scr, hs_scr, h_scr, wbf_scr):
    ts, width = xb_ref.shape
    n_blocks, blk, _ = wri_ref.shape
    _cast_w_out_once(wout_ref, wbf_scr)

    @pl.when(pl.program_id(1) == 0)
    def _():
        h_scr[...] = jnp.zeros_like(h_scr)

    _shift_in_tile(ext_scr, xb_ref[...].astype(F32), ts)
    xc = _causal_conv_from_ext(ext_scr, cw_ref, ts) + cb_ref[...]

    for g in range(n_blocks):
        cols = slice(g * blk, (g + 1) * blk)
        xg = xc[:, cols]
        ri = jnp.dot(xg.astype(BF16), wri_ref[g], preferred_element_type=F32)
        r = jax.nn.sigmoid(ri[:, :blk] + br_ref[:, cols])
        ig = jax.nn.sigmoid(ri[:, blk:] + bi_ref[:, cols])
        log_a = (RG_C * r) * _log_sigmoid(lam_ref[:, cols])
        a = jnp.exp(log_a)
        a_scr[:, cols] = a
        z = 1.0 - a * a
        b_scr[:, cols] = jnp.where(z > 0.0, z * lax.rsqrt(z), z) * (ig * xg)

    row = lax.broadcasted_iota(jnp.int32, (SUBLANES, width), 0)

    def body(i, h):
        r0 = i * SUBLANES
        a = a_scr[pl.ds(r0, SUBLANES), :]
        b = b_scr[pl.ds(r0, SUBLANES), :]
        for sh in (1, 2, 4):
            keep = row >= sh
            a_prev = jnp.where(keep, pltpu.roll(a, sh, 0), 1.0)
            b_prev = jnp.where(keep, pltpu.roll(b, sh, 0), 0.0)
            b = a * b_prev + b
            a = a * a_prev
        hs = a * h + b
        hs_scr[pl.ds(r0, SUBLANES), :] = hs
        return jnp.broadcast_to(hs[SUBLANES - 1:SUBLANES, :], (SUBLANES, width))

    h = h_scr[...]
    for i in range(ts // SUBLANES):
        h = body(i, h)
    h_scr[...] = h
    y = (hs_scr[...] * jax.nn.gelu(gate_ref[...].astype(F32))).astype(BF16)
    _project_residual(y, wbf_scr, x_ref, mgate_ref, o_ref)


def _rg_mid_call(p, conv_w, conv_b, w_ri, b_r, b_i, lam, w_out, widx, x, mod_r, layer, batch, ts=CORE_ROWS):
    m, two_w = p.shape
    width = two_w // 2
    d = x.shape[1]
    seq = m // batch
    ts = min(ts, seq)
    nst = seq // ts
    n_blocks, blk, _ = w_ri.shape
    full = lambda shape: pl.BlockSpec(shape, lambda b, s: (0,) * len(shape))
    return pl.pallas_call(
        _rg_mid_kernel,
        out_shape=jax.ShapeDtypeStruct((m, d), F32),
        grid=(batch, nst),
        in_specs=[
            pl.BlockSpec((ts, width), lambda b, s: (b * nst + s, 0)),
            pl.BlockSpec((ts, width), lambda b, s: (b * nst + s, 1)),
            full(conv_w.shape),
            full((1, width)),
            full((n_blocks, blk, 2 * blk)),
            full((1, width)),
            full((1, width)),
            full((1, width)),
        ] + _res_specs(w_out, widx, ts, nst, d, layer * batch * 9 + 5),
        out_specs=pl.BlockSpec((ts, d), lambda b, s: (b * nst + s, 0)),
        scratch_shapes=[
            pltpu.VMEM((ts + SUBLANES, width), F32),
            pltpu.VMEM((ts, width), F32),
            pltpu.VMEM((ts, width), F32),
            pltpu.VMEM((ts, width), F32),
            pltpu.VMEM((SUBLANES, width), F32),
            _w_out_scratch(w_out, d),
        ],
        compiler_params=_cparams(2),
        name="rglru_core",
    )(p, p, conv_w, conv_b.reshape(1, width), w_ri, b_r.reshape(1, width), b_i.reshape(1, width),
      lam.reshape(1, width), w_out, x, mod_r)


def _sc_mid_kernel(bg_ref, cg_ref, xv_ref, cw_ref, wout_ref, x_ref, mgate_ref, o_ref, ext_scr, wbf_scr):
    ts = bg_ref.shape[0]
    _cast_w_out_once(wout_ref, wbf_scr)
    _shift_in_tile(ext_scr, cg_ref[...].astype(F32) * xv_ref[...].astype(F32), ts)
    conv = _causal_conv_from_ext(ext_scr, cw_ref, ts)
    y = (bg_ref[...].astype(F32) * conv).astype(BF16)
    _project_residual(y, wbf_scr, x_ref, mgate_ref, o_ref)


def _sc_mid_call(p, conv_w, w_out, widx, x, mod_r, layer, batch, ts=CORE_ROWS):
    m, three_d = p.shape
    d = three_d // 3
    seq = m // batch
    ts = min(ts, seq)
    nst = seq // ts
    return pl.pallas_call(
        _sc_mid_kernel,
        out_shape=jax.ShapeDtypeStruct((m, d), F32),
        grid=(batch, nst),
        in_specs=[
            pl.BlockSpec((ts, d), lambda b, s: (b * nst + s, 0)),
            pl.BlockSpec((ts, d), lambda b, s: (b * nst + s, 1)),
            pl.BlockSpec((ts, d), lambda b, s: (b * nst + s, 2)),
            pl.BlockSpec(conv_w.shape, lambda b, s: (0, 0)),
        ] + _res_specs(w_out, widx, ts, nst, d, layer * batch * 9 + 5),
        out_specs=pl.BlockSpec((ts, d), lambda b, s: (b * nst + s, 0)),
        scratch_shapes=[pltpu.VMEM((ts + SUBLANES, d), F32), _w_out_scratch(w_out, d)],
        compiler_params=_cparams(2),
        name="shortconv_core",
    )(p, p, p, conv_w, w_out, x, mod_r)


def _fox_prep_kernel(q_ref, k_ref, v_ref, fl_ref, bf_ref, qg_ref, kg_ref, qa_ref, ka_ref, vt_ref, carry_scr):
    ts, d = q_ref.shape

    @pl.when(pl.program_id(1) == 0)
    def _():
        carry_scr[...] = jnp.zeros_like(carry_scr)

    logf = _log_sigmoid(fl_ref[...] + bf_ref[...])
    r = lax.broadcasted_iota(jnp.int32, (ts, ts), 0)
    c = lax.broadcasted_iota(jnp.int32, (ts, ts), 1)
    tril = jnp.where(c <= r, 1.0, 0.0).astype(BF16)
    cum = _tril_dot(tril, logf) + carry_scr[0:1, :]
    carry_scr[...] = jnp.broadcast_to(cum[ts - 1:ts, :], carry_scr.shape)
    f_over_scale = cum * math.sqrt(HEAD_DIM)

    lane = lax.broadcasted_iota(jnp.int32, (ts, HEAD_DIM), 1)
    for h in range(d // HEAD_DIM):
        cols = slice(h * HEAD_DIM, (h + 1) * HEAD_DIM)
        hi, mid, lo = (part.astype(F32) for part in _split3(f_over_scale[:, h:h + 1]))
        parts = jnp.where((lane == 0) | (lane == 3), hi, jnp.where((lane == 1) | (lane == 4), mid, lo))
        q_aug = jnp.where(lane < 3, parts, jnp.where(lane < 6, 1.0, 0.0))
        k_aug = jnp.where(lane < 3, 1.0, jnp.where(lane < 6, -parts, 0.0))
        for src, gain, aug, dst in ((q_ref, qg_ref, q_aug, qa_ref), (k_ref, kg_ref, k_aug, ka_ref)):
            t = src[:, cols].astype(F32)
            ms = jnp.mean(t * t, axis=-1, keepdims=True)
            dst[:, 2 * h * HEAD_DIM:(2 * h + 1) * HEAD_DIM] = ((t * lax.rsqrt(ms + EPS)) * gain[...]).astype(BF16)
            dst[:, (2 * h + 1) * HEAD_DIM:(2 * h + 2) * HEAD_DIM] = aug.astype(BF16)
        vt_ref[h] = v_ref[:, cols].astype(F32).T.astype(BF16)


def _fox_prep_call(p, fl, b_f_pad, q_gain, k_gain, batch, ts):
    m = p.shape[0]
    d = p.shape[1] // 4
    n_heads = d // HEAD_DIM
    seq = m // batch
    nst = seq // ts
    lanes = fl.shape[1]
    return pl.pallas_call(
        _fox_prep_kernel,
        out_shape=(
            jax.ShapeDtypeStruct((m, 2 * d), BF16),
            jax.ShapeDtypeStruct((m, 2 * d), BF16),
            jax.ShapeDtypeStruct((batch, nst, n_heads, HEAD_DIM, ts), BF16),
        ),
        grid=(batch, nst),
        in_specs=[
            pl.BlockSpec((ts, d), lambda b, s: (b * nst + s, 0)),
            pl.BlockSpec((ts, d), lambda b, s: (b * nst + s, 1)),
            pl.BlockSpec((ts, d), lambda b, s: (b * nst + s, 2)),
            pl.BlockSpec((ts, lanes), lambda b, s: (b * nst + s, 0)),
            pl.BlockSpec((1, lanes), lambda b, s: (0, 0)),
            pl.BlockSpec((1, HEAD_DIM), lambda b, s: (0, 0)),
            pl.BlockSpec((1, HEAD_DIM), lambda b, s: (0, 0)),
        ],
        out_specs=(
            pl.BlockSpec((ts, 2 * d), lambda b, s: (b * nst + s, 0)),
            pl.BlockSpec((ts, 2 * d), lambda b, s: (b * nst + s, 0)),
            pl.BlockSpec((None, None, n_heads, HEAD_DIM, ts), lambda b, s: (b, s, 0, 0, 0)),
        ),
        scratch_shapes=[pltpu.VMEM((SUBLANES, lanes), F32)],
        compiler_params=_cparams(2),
        name="fox_prep",
    )(p, p, p, fl, b_f_pad, q_gain.reshape(1, HEAD_DIM), k_gain.reshape(1, HEAD_DIM))


def _fox_flash_kernel(q_ref, k_ref, vt_ref, g_ref, o_ref):
    t = q_ref.shape[0]
    _, hp, dh, _ = vt_ref.shape
    qi = pl.program_id(2)
    to_log2 = (1.0 / math.sqrt(dh)) * math.log2(math.e)
    nt = (((1,), (1,)), ((), ()))

    def scores(j, e):
        cols = slice(2 * e * dh, 2 * (e + 1) * dh)
        kj = k_ref[pl.ds(pl.multiple_of(j * t, t), t), cols]
        return lax.dot_general(kj, q_ref[:, cols], nt, preferred_element_type=F32)

    def step(j, carry, masked):
        raw = [scores(j, e) for e in range(hp)]
        out = []
        for e in range(hp):
            m_prev, l_prev, acc = carry[e]
            s = raw[e] * to_log2
            if masked:
                key = lax.broadcasted_iota(jnp.int32, (t, t), 0)
                qry = lax.broadcasted_iota(jnp.int32, (t, t), 1)
                s = jnp.where(key <= qry, s, -jnp.inf)
            m_new = jnp.maximum(m_prev, jnp.max(s, axis=0, keepdims=True))
            alpha = jnp.exp2(m_prev - m_new)
            p = jnp.exp2(s - m_new)
            l_new = alpha * l_prev + jnp.sum(p, axis=0, keepdims=True)
            acc = alpha * acc + jnp.dot(vt_ref[j, e], p.astype(BF16), preferred_element_type=F32)
            out.append((m_new, l_new, acc))
        return tuple(out)

    init = tuple((jnp.full((1, t), -jnp.inf, F32), jnp.zeros((1, t), F32), jnp.zeros((dh, t), F32))
                 for _ in range(hp))

    carry = lax.fori_loop(0, qi, functools.partial(step, masked=False), init)
    fin = step(qi, carry, True)
    for e in range(hp):
        _, l_fin, acc = fin[e]
        cols = slice(e * dh, (e + 1) * dh)
        o = (acc / l_fin).T
        o_ref[:, cols] = (o * jax.nn.sigmoid(g_ref[:, cols].astype(F32))).astype(BF16)


def _fox_flash_call(qa, ka, vt, p, batch, t, hp=ATTN_HEADS):
    m = qa.shape[0]
    n_heads = vt.shape[2]
    d = n_heads * HEAD_DIM
    seq = m // batch
    nq = seq // t
    hp = math.gcd(hp, n_heads)
    g_blk = 3 * n_heads // hp
    return pl.pallas_call(
        _fox_flash_kernel,
        out_shape=jax.ShapeDtypeStruct((m, d), BF16),
        grid=(batch, n_heads // hp, nq),
        in_specs=[
            pl.BlockSpec((t, 2 * hp * HEAD_DIM), lambda b, h, i: (b * nq + i, h)),
            pl.BlockSpec((seq, 2 * hp * HEAD_DIM), lambda b, h, i: (b, h)),
            pl.BlockSpec((None, nq, hp, HEAD_DIM, t), lambda b, h, i: (b, 0, h, 0, 0)),
            pl.BlockSpec((t, hp * HEAD_DIM), lambda b, h, i: (b * nq + i, g_blk + h)),
        ],
        out_specs=pl.BlockSpec((t, hp * HEAD_DIM), lambda b, h, i: (b * nq + i, h)),
        compiler_params=_cparams(3),
        name="fox_flash",
    )(qa, ka, vt, p)


def _hg_mid_kernel(q_ref, fz_ref, iv_ref, g_ref, lbl_ref, ng_ref, wout_ref, x_ref, mgate_ref, o_ref,
                   qin_scr, kin_scr, kdec_scr, dec_scr, st_scr, y_scr, wbf_scr, *, layer):
    ts, d = q_ref.shape
    n_heads = d // HEAD_DIM
    n_chunks = ts // CHUNK
    _cast_w_out_once(wout_ref, wbf_scr)

    @pl.when(pl.program_id(1) == 0)
    def _():
        st_scr[...] = jnp.zeros_like(st_scr)

    lg = lbl_ref[...]
    depth = lg.shape[0]
    mx = lg[0:1, :]
    for i in range(1, depth):
        mx = jnp.maximum(mx, lg[i:i + 1, :])
    es = [jnp.exp(lg[i:i + 1, :] - mx) for i in range(depth)]
    tot = es[0]
    for i in range(1, depth):
        tot = tot + es[i]
    lb = jnp.zeros_like(tot)
    for i in range(layer):
        lb = lb + es[i] / tot

    sig = jax.nn.sigmoid(fz_ref[...].astype(F32))
    logf = jnp.log(lb + (1.0 - lb) * sig)
    kk = (1.0 - lb) * (1.0 - sig)

    r = lax.broadcasted_iota(jnp.int32, (ts, ts), 0)
    c = lax.broadcasted_iota(jnp.int32, (ts, ts), 1)
    tril = jnp.where((c <= r) & (jnp.right_shift(c, CHUNK_SHIFT) == jnp.right_shift(r, CHUNK_SHIFT)), 1.0, 0.0).astype(BF16)
    bcum = _tril_dot(tril, logf)

    qin_scr[...] = (q_ref[...].astype(F32) * jnp.exp(bcum)).astype(BF16)
    kin_scr[...] = (kk * jnp.exp(-bcum)).astype(BF16)
    for ci in range(n_chunks):
        rows = slice(ci * CHUNK, (ci + 1) * CHUNK)
        b_last = bcum[(ci + 1) * CHUNK - 1:(ci + 1) * CHUNK, :]
        kdec_scr[rows, :] = (kk[rows, :] * jnp.exp(b_last - bcum[rows, :])).astype(BF16)
        dec_scr[ci] = jnp.broadcast_to(jnp.exp(b_last), (SUBLANES, d))

    rr = lax.broadcasted_iota(jnp.int32, (CHUNK, CHUNK), 0)
    cc = lax.broadcasted_iota(jnp.int32, (CHUNK, CHUNK), 1)
    causal = cc <= rr
    nt = (((1,), (1,)), ((), ()))
    tn = (((0,), (0,)), ((), ()))

    def chunk_body(ci, carry):
        r0 = ci * CHUNK
        rows = pl.ds(r0, CHUNK)
        heads = [slice(h * HEAD_DIM, (h + 1) * HEAD_DIM) for h in range(n_heads)]
        qh = [qin_scr[rows, c] for c in heads]
        vh = [iv_ref[rows, c].astype(BF16) for c in heads]
        st = [st_scr[h] for h in range(n_heads)]
        sc = [lax.dot_general(qh[h], kin_scr[rows, c], nt, preferred_element_type=F32)
              for h, c in enumerate(heads)]
        o_inter = [lax.dot_general(qh[h], st[h].astype(BF16), nt, preferred_element_type=F32)
                   for h in range(n_heads)]
        u_t = [lax.dot_general(vh[h], kdec_scr[rows, c], tn, preferred_element_type=F32)
               for h, c in enumerate(heads)]
        for h, c in enumerate(heads):
            scm = jnp.where(causal, sc[h], 0.0).astype(BF16)
            o = jnp.dot(scm, vh[h], preferred_element_type=F32) + o_inter[h]
            st_scr[h] = st[h] * dec_scr[ci][0:1, c] + u_t[h]
            ms = jnp.mean(o * o, axis=-1, keepdims=True)
            on = (o * lax.rsqrt(ms + EPS)) * ng_ref[...]
            gh = g_ref[rows, c].astype(F32)
            y_scr[rows, c] = (on * (gh * jax.nn.sigmoid(gh))).astype(BF16)
        return carry

    for ci in range(n_chunks):
        chunk_body(ci, 0)
    _project_residual(y_scr[...], wbf_scr, x_ref, mgate_ref, o_ref)


def _hg_mid_call(p, lb_logits, norm_gain, w_out, widx, x, mod_r, layer, batch, ts=CORE_ROWS):
    m, four_d = p.shape
    d = four_d // 4
    n_heads = d // HEAD_DIM
    seq = m // batch
    ts = min(ts, seq)
    nst = seq // ts
    return pl.pallas_call(
        functools.partial(_hg_mid_kernel, layer=layer),
        out_shape=jax.ShapeDtypeStruct((m, d), F32),
        grid=(batch, nst),
        in_specs=[
            pl.BlockSpec((ts, d), lambda b, s: (b * nst + s, 0)),
            pl.BlockSpec((ts, d), lambda b, s: (b * nst + s, 1)),
            pl.BlockSpec((ts, d), lambda b, s: (b * nst + s, 2)),
            pl.BlockSpec((ts, d), lambda b, s: (b * nst + s, 3)),
            pl.BlockSpec(lb_logits.shape, lambda b, s: (0, 0)),
            pl.BlockSpec((1, HEAD_DIM), lambda b, s: (0, 0)),
        ] + _res_specs(w_out, widx, ts, nst, d, layer * batch * 9 + 5),
        out_specs=pl.BlockSpec((ts, d), lambda b, s: (b * nst + s, 0)),
        scratch_shapes=[
            pltpu.VMEM((ts, d), BF16),
            pltpu.VMEM((ts, d), BF16),
            pltpu.VMEM((ts, d), BF16),
            pltpu.VMEM((ts // CHUNK, SUBLANES, d), F32),
            pltpu.VMEM((n_heads, HEAD_DIM, HEAD_DIM), F32),
            pltpu.VMEM((ts, d), BF16),
            _w_out_scratch(w_out, d),
        ],
        compiler_params=_cparams(2),
        name="hgrn2_core",
    )(p, p, p, p, lb_logits, norm_gain.reshape(1, HEAD_DIM), w_out, x, mod_r)


def kernel(x, c, ada_w, ada_b, norm_g, ffn_w_gate, ffn_w_up, ffn_w_down, rg_w_in, rg_conv_w, rg_conv_b, rg_w_r, rg_b_r, rg_w_i, rg_b_i, rg_lam, rg_w_out, sc_w_in, sc_conv_w, sc_w_out, fox_w_in, fox_b_f, fox_q_gain, fox_k_gain, fox_w_out, hg_w_in, hg_lb_logits, hg_norm_gain, hg_w_out):
    batch, seq, d = x.shape
    depth = ada_w.shape[0]
    n_heads = d // HEAD_DIM
    assert d % HEAD_DIM == 0 and seq % CHUNK == 0 and seq % SUBLANES == 0
    mid_dtype = BF16

    mod = _ada_call(c, ada_w, ada_b)
    mod_r = mod.reshape(depth * batch * 9, 1, d)
    ng_r = norm_g.reshape(depth * 3, 1, d)

    ffn_w = (ffn_w_gate, ffn_w_up, ffn_w_down)
    w_cur = None

    def ffn(xf, w_cur, k, sub):
        nxt = (ffn_w, (k + 1) // 2, (k + 1) % 2) if k + 1 < 2 * depth else None
        if w_cur is None:
            return _ffn_call(xf, mod_r, ng_r, ffn_w, k // 2, sub, batch, nxt, w_index=(0, 0),
                             tf=FFN_HIDDEN_F32)
        return _ffn_call(xf, mod_r, ng_r, w_cur, k // 2, sub, batch, nxt)

    xf = x.reshape(batch * seq, d)
    for i in range(depth):
        xf, w_cur = ffn(xf, w_cur, 2 * i, 0)
        m, j = i % N_MIXERS, i // N_MIXERS
        if m == 0:
            p = _mm_mod_call(xf, mod_r, ng_r, rg_w_in, j, rg_w_in.shape[2], i, batch, mid_dtype)
            w_ri = jnp.concatenate([rg_w_r[j], rg_w_i[j]], axis=-1).astype(BF16)
            xf = _rg_mid_call(p, rg_conv_w[j], rg_conv_b[j], w_ri, rg_b_r[j], rg_b_i[j], rg_lam[j],
                              rg_w_out, j, xf, mod_r, i, batch)
        elif m == 1:
            p = _mm_mod_call(xf, mod_r, ng_r, sc_w_in, j, sc_w_in.shape[2], i, batch, mid_dtype)
            xf = _sc_mid_call(p, sc_conv_w[j], sc_w_out, j, xf, mod_r, i, batch)
        elif m == 2:
            w_in_t = jnp.swapaxes(fox_w_in, 1, 2)
            w_fl = jnp.pad(w_in_t[j, 4 * d:], ((0, LANES - n_heads), (0, 0)))
            p, fl = _mm_mod_call(xf, mod_r, ng_r, w_in_t, j, 4 * d, i, batch, mid_dtype, w_t=True, w_side=w_fl)
            b_f_pad = jnp.pad(fox_b_f[j], (0, LANES - n_heads)).reshape(1, LANES)
            t_attn = min(ATTN_TILE, seq)
            qa, ka, vt = _fox_prep_call(p, fl, b_f_pad, fox_q_gain[j], fox_k_gain[j], batch, t_attn)
            y = _fox_flash_call(qa, ka, vt, p, batch, t_attn)
            xf = _mm_res_call(y, fox_w_out, j, xf, mod_r, i, batch)
        else:
            p = _mm_mod_call(xf, mod_r, ng_r, hg_w_in, j, hg_w_in.shape[2], i, batch, mid_dtype)
            xf = _hg_mid_call(p, hg_lb_logits, hg_norm_gain[j], hg_w_out, j, xf, mod_r, i, batch)
        xf, w_cur = ffn(xf, w_cur, 2 * i + 1, 2)
    return xf.reshape(batch, seq, d)
```

```python
import functools
import math

import jax
import jax.numpy as jnp
from jax import lax
from jax.experimental import pallas as pl
from jax.experimental.pallas import tpu as pltpu

F32 = jnp.float32
BF16 = jnp.bfloat16

EPS = 1e-6
HEAD_DIM = 128
CHUNK = 64
CHUNK_SHIFT = 6
RG_C = 8.0
N_MIXERS = 4

LANES = 128
SUBLANES = 8
VMEM_BYTES = 64 * 1024 * 1024
VMEM_LIMIT = VMEM_BYTES - 4 * 1024 * 1024

ADA_COLS = 2048
FFN_ROWS = 1024
FFN_HIDDEN = 512
FFN_HIDDEN_F32 = 256
ROW_CHUNK = 512
PROJ_ROWS = 2048
PROJ_COLS = 512
OUT_ROWS = 512
CORE_ROWS = 256
ATTN_TILE = 512
ATTN_HEADS = 8


def _cparams(n_axes):
    return pltpu.CompilerParams(dimension_semantics=("arbitrary",) * n_axes,
                                vmem_limit_bytes=VMEM_LIMIT)


def _tile(n, preferred):
    if n <= preferred:
        return n
    t = preferred - preferred % LANES
    while n % t:
        t -= LANES
    return t


def _modulate(x, g, shift, scale):
    ms = jnp.mean(x * x, axis=-1, keepdims=True)
    return (x * lax.rsqrt(ms + EPS)) * (g * (1.0 + scale)) + shift


def _log_sigmoid(x):
    return jnp.minimum(x, 0.0) - jnp.log1p(jnp.exp(-jnp.abs(x)))


def _split3(x):
    hi = x.astype(BF16)
    r1 = x - hi.astype(F32)
    mid = r1.astype(BF16)
    lo = (r1 - mid.astype(F32)).astype(BF16)
    return hi, mid, lo


def _tril_dot(tril, x):
    hi, mid, lo = _split3(x)
    acc = jnp.dot(tril, hi, preferred_element_type=F32)
    acc = acc + jnp.dot(tril, mid, preferred_element_type=F32)
    return acc + jnp.dot(tril, lo, preferred_element_type=F32)


def _ada_kernel(c_ref, w_ref, b_ref, o_ref):
    c = c_ref[...]
    sc = (c * jax.nn.sigmoid(c)).astype(BF16)
    o_ref[...] = jnp.dot(sc, w_ref[...].astype(BF16), preferred_element_type=F32) + b_ref[...]


def _ada_call(c, ada_w, ada_b, tn=ADA_COLS):
    depth, d, n = ada_w.shape
    b = c.shape[0]
    rows = -(-b // SUBLANES) * SUBLANES
    c_pad = jnp.pad(c, ((0, rows - b), (0, 0)))
    tn = _tile(n, tn)
    out = pl.pallas_call(
        _ada_kernel,
        out_shape=jax.ShapeDtypeStruct((depth, rows, n), F32),
        grid=(depth, n // tn),
        in_specs=[
            pl.BlockSpec((rows, d), lambda l, j: (0, 0)),
            pl.BlockSpec((None, d, tn), lambda l, j: (l, 0, j)),
            pl.BlockSpec((None, 1, tn), lambda l, j: (l, 0, j)),
        ],
        out_specs=pl.BlockSpec((None, rows, tn), lambda l, j: (l, 0, j)),
        compiler_params=_cparams(2),
        name="ada",
    )(c_pad, ada_w, ada_b.reshape(depth, 1, n))
    return out[:, :b]


def _row_chunks(tm, rc):
    return [slice(r, r + rc) for r in range(0, tm, rc)]


def _ffn_kernel(x_ref, shift_ref, scale_ref, gate_ref, g_ref, wg_ref, wu_ref, wd_ref, *rest, rc, nf, n_ahead):
    ahead_in = rest[:n_ahead]
    o_ref = rest[n_ahead]
    ahead_out = rest[n_ahead + 1:2 * n_ahead + 1]
    h_scr = rest[2 * n_ahead + 1]
    f = pl.program_id(1)
    chunks = _row_chunks(x_ref.shape[0], rc)

    def weights():
        for src, dst in zip(ahead_in, ahead_out):
            dst[...] = src[...].astype(BF16)
        return wg_ref[...].astype(BF16), wu_ref[...].astype(BF16), wd_ref[...].astype(BF16)

    def swiglu(h, wg, wu, wd):
        gg = jnp.dot(h, wg, preferred_element_type=F32)
        uu = jnp.dot(h, wu, preferred_element_type=F32)
        a = ((gg * jax.nn.sigmoid(gg)) * uu).astype(BF16)
        return jnp.dot(a, wd, preferred_element_type=F32)

    def run(first, last):
        w = weights()
        half_gate = 0.5 * gate_ref[...] if last else None
        for rows in chunks:
            if first:
                h = _modulate(x_ref[rows, :], g_ref[...], shift_ref[...], scale_ref[...]).astype(BF16)
                h_scr[rows, :] = h
                acc = swiglu(h, *w)
            else:
                acc = o_ref[rows, :] + swiglu(h_scr[rows, :], *w)
            if last:
                acc = x_ref[rows, :] + half_gate * acc
            o_ref[rows, :] = acc

    if nf == 1:
        run(True, True)
    else:
        pl.when(f == 0)(functools.partial(run, True, False))
        pl.when((f > 0) & (f < nf - 1))(functools.partial(run, False, False))
        pl.when(f == nf - 1)(functools.partial(run, False, True))


def _mod_spec(d, base, tiles_per_batch, col_axis=None, tn=None):
    width = d if tn is None else tn

    def index(*ids):
        col = 0 if col_axis is None else ids[col_axis]
        return (base + 9 * (ids[0] // tiles_per_batch), 0, col)

    return pl.BlockSpec((None, 1, width), index)


def _ffn_call(x, mod_r, ng_r, w, layer, sub, batch, nxt=None, w_index=None, tm=FFN_ROWS, tf=FFN_HIDDEN,
              rc=ROW_CHUNK):
    wg, wu, wd = w
    m, d = x.shape
    f_dim = wg.shape[-1]
    tm = min(tm, m // batch)
    rc = min(rc, tm)
    tf = _tile(f_dim, tf)
    tpb = (m // batch) // tm
    ni, nf = m // tm, f_dim // tf
    base = layer * batch * 9 + sub * 3
    if w_index is None:
        w_specs = [pl.BlockSpec((d, tf), lambda i, f: (0, f)), pl.BlockSpec((d, tf), lambda i, f: (0, f)),
                   pl.BlockSpec((tf, d), lambda i, f: (f, 0))]
    else:
        wl, ww = w_index
        w_specs = [pl.BlockSpec((None, None, d, tf), lambda i, f: (wl, ww, 0, f)),
                   pl.BlockSpec((None, None, d, tf), lambda i, f: (wl, ww, 0, f)),
                   pl.BlockSpec((None, None, tf, d), lambda i, f: (wl, ww, f, 0))]
    in_specs = [
        pl.BlockSpec((tm, d), lambda i, f: (i, 0)),
        _mod_spec(d, base + 0, tpb),
        _mod_spec(d, base + 1, tpb),
        _mod_spec(d, base + 2, tpb),
        pl.BlockSpec((None, 1, d), lambda i, f: (layer * 3 + sub, 0, 0)),
    ] + w_specs
    operands = [x, mod_r, mod_r, mod_r, ng_r, wg, wu, wd]
    out_shape = [jax.ShapeDtypeStruct((m, d), F32)]
    out_specs = [pl.BlockSpec((tm, d), lambda i, f: (i, 0))]
    if nxt is not None:
        (ng, nu, nd), nl, nw = nxt
        dr = d // ni
        in_specs += [
            pl.BlockSpec((None, None, dr, tf), lambda i, f: (nl, nw, i, f)),
            pl.BlockSpec((None, None, dr, tf), lambda i, f: (nl, nw, i, f)),
            pl.BlockSpec((None, None, tf, dr), lambda i, f: (nl, nw, f, i)),
        ]
        operands += [ng, nu, nd]
        out_shape += [jax.ShapeDtypeStruct((d, f_dim), BF16), jax.ShapeDtypeStruct((d, f_dim), BF16),
                      jax.ShapeDtypeStruct((f_dim, d), BF16)]
        out_specs += [pl.BlockSpec((dr, tf), lambda i, f: (i, f)), pl.BlockSpec((dr, tf), lambda i, f: (i, f)),
                      pl.BlockSpec((tf, dr), lambda i, f: (f, i))]
    outs = pl.pallas_call(
        functools.partial(_ffn_kernel, rc=rc, nf=nf, n_ahead=0 if nxt is None else 3),
        out_shape=out_shape,
        grid=(ni, nf),
        in_specs=in_specs,
        out_specs=out_specs,
        scratch_shapes=[pltpu.VMEM((tm, d), BF16)],
        compiler_params=_cparams(2),
        name="ffn",
    )(*operands)
    return outs[0], tuple(outs[1:])


def _mm_mod_kernel(x_ref, shift_ref, scale_ref, g_ref, w_ref, *rest, rc, w_t, side):
    if side:
        ws_ref, o_ref, os_ref, h_scr = rest
    else:
        o_ref, h_scr = rest
    j = pl.program_id(1)
    chunks = _row_chunks(x_ref.shape[0], rc)
    nt = (((1,), (1,)), ((), ()))
    dims = nt if w_t else (((1,), (0,)), ((), ()))

    def project(h, w):
        return lax.dot_general(h, w, dims, preferred_element_type=F32).astype(o_ref.dtype)

    @pl.when(j == 0)
    def _():
        w = w_ref[...].astype(BF16)
        for rows in chunks:
            h = _modulate(x_ref[rows, :], g_ref[...], shift_ref[...], scale_ref[...]).astype(BF16)
            h_scr[rows, :] = h
            o_ref[rows, :] = project(h, w)
            if side:
                os_ref[rows, :] = lax.dot_general(h, ws_ref[...].astype(BF16), nt, preferred_element_type=F32)

    @pl.when(j > 0)
    def _():
        w = w_ref[...].astype(BF16)
        for rows in chunks:
            o_ref[rows, :] = project(h_scr[rows, :], w)


def _mm_mod_call(x, mod_r, ng_r, w, widx, n_out, layer, batch, out_dtype, tm=PROJ_ROWS, tn=PROJ_COLS,
                 rc=ROW_CHUNK, w_t=False,
                 w_side=None):
    m, d = x.shape
    tm = min(tm, m // batch)
    rc = min(rc, tm)
    tn = _tile(n_out, tn)
    tpb = (m // batch) // tm
    base = layer * batch * 9 + 3
    if w_t:
        w_spec = pl.BlockSpec((None, tn, d), lambda i, j: (widx, j, 0))
    else:
        w_spec = pl.BlockSpec((None, d, tn), lambda i, j: (widx, 0, j))
    in_specs = [
        pl.BlockSpec((tm, d), lambda i, j: (i, 0)),
        _mod_spec(d, base + 0, tpb),
        _mod_spec(d, base + 1, tpb),
        pl.BlockSpec((None, 1, d), lambda i, j: (layer * 3 + 1, 0, 0)),
        w_spec,
    ]
    operands = [x, mod_r, mod_r, ng_r, w]
    out_shape = [jax.ShapeDtypeStruct((m, n_out), out_dtype)]
    out_specs = [pl.BlockSpec((tm, tn), lambda i, j: (i, j))]
    if w_side is not None:
        n_side = w_side.shape[0]
        in_specs.append(pl.BlockSpec((n_side, d), lambda i, j: (0, 0)))
        operands.append(w_side)
        out_shape.append(jax.ShapeDtypeStruct((m, n_side), F32))
        out_specs.append(pl.BlockSpec((tm, n_side), lambda i, j: (i, 0)))
    outs = pl.pallas_call(
        functools.partial(_mm_mod_kernel, rc=rc, w_t=w_t, side=w_side is not None),
        out_shape=out_shape,
        grid=(m // tm, n_out // tn),
        in_specs=in_specs,
        out_specs=out_specs,
        scratch_shapes=[pltpu.VMEM((tm, d), BF16)],
        compiler_params=_cparams(2),
        name="in_proj",
    )(*operands)
    return outs[0] if w_side is None else tuple(outs)


def _mm_res_kernel(y_ref, w_ref, x_ref, gate_ref, o_ref, wbf_scr):
    @pl.when(pl.program_id(0) == 0)
    def _():
        wbf_scr[...] = w_ref[...].astype(BF16)

    acc = jnp.dot(y_ref[...], wbf_scr[...], preferred_element_type=F32)
    o_ref[...] = x_ref[...] + gate_ref[...] * acc


def _mm_res_call(y, w, widx, x, mod_r, layer, batch, tm=OUT_ROWS):
    m, k = y.shape
    d = x.shape[1]
    tm = min(tm, m // batch)
    tpb = (m // batch) // tm
    base = layer * batch * 9 + 3 + 2
    return pl.pallas_call(
        _mm_res_kernel,
        out_shape=jax.ShapeDtypeStruct((m, d), F32),
        grid=(m // tm,),
        in_specs=[
            pl.BlockSpec((tm, k), lambda i: (i, 0)),
            pl.BlockSpec((None, k, d), lambda i: (widx, 0, 0), pipeline_mode=pl.Buffered(1)),
            pl.BlockSpec((tm, d), lambda i: (i, 0)),
            _mod_spec(d, base, tpb),
        ],
        out_specs=pl.BlockSpec((tm, d), lambda i: (i, 0)),
        scratch_shapes=[pltpu.VMEM((k, d), BF16)],
        compiler_params=_cparams(1),
        name="out_proj",
    )(y, w, x, mod_r)


def _res_specs(w_out, widx, ts, nst, d, gate_base):
    k = w_out.shape[1]
    return [
        pl.BlockSpec((None, k, d), lambda b, s: (widx, 0, 0), pipeline_mode=pl.Buffered(1)),
        pl.BlockSpec((ts, d), lambda b, s: (b * nst + s, 0)),
        pl.BlockSpec((None, 1, d), lambda b, s: (gate_base + 9 * b, 0, 0)),
    ]


def _w_out_scratch(w_out, d):
    return pltpu.VMEM((w_out.shape[1], d), BF16)


def _cast_w_out_once(wout_ref, wbf_scr):
    @pl.when((pl.program_id(0) == 0) & (pl.program_id(1) == 0))
    def _():
        wbf_scr[...] = wout_ref[...].astype(BF16)


def _project_residual(y, wbf_scr, x_ref, mgate_ref, o_ref):
    o_ref[...] = x_ref[...] + mgate_ref[...] * jnp.dot(y, wbf_scr[...], preferred_element_type=F32)


def _shift_in_tile(ext_scr, cur, ts):
    s = pl.program_id(1)

    @pl.when(s == 0)
    def _():
        ext_scr[0:SUBLANES, :] = jnp.zeros((SUBLANES, ext_scr.shape[1]), F32)

    @pl.when(s > 0)
    def _():
        ext_scr[0:SUBLANES, :] = ext_scr[ts:ts + SUBLANES, :]

    ext_scr[SUBLANES:ts + SUBLANES, :] = cur


def _causal_conv_from_ext(ext_scr, cw_ref, ts):
    kw = cw_ref.shape[0]
    ext = ext_scr[...]
    acc = None
    for k in range(kw):
        shift = kw - 1 - k
        tap = ext if shift == 0 else pltpu.roll(ext, shift, 0)
        term = cw_ref[k:k + 1, :] * tap[SUBLANES:, :]
        acc = term if acc is None else acc + term
    return acc


def _rg_mid_kernel(gate_ref, xb_ref, cw_ref, cb_ref, wri_ref, br_ref, bi_ref, lam_ref,
                   wout_ref, x_ref, mgate_ref, o_ref, ext_scr, a_scr, b_scr, hs_scr, h_scr, wbf_scr):
    ts, width = xb_ref.shape
    n_blocks, blk, _ = wri_ref.shape
    _cast_w_out_once(wout_ref, wbf_scr)

    @pl.when(pl.program_id(1) == 0)
    def _():
        h_scr[...] = jnp.zeros_like(h_scr)

    _shift_in_tile(ext_scr, xb_ref[...].astype(F32), ts)
    xc = _causal_conv_from_ext(ext_scr, cw_ref, ts) + cb_ref[...]

    for g in range(n_blocks):
        cols = slice(g * blk, (g + 1) * blk)
        xg = xc[:, cols]
        ri = jnp.dot(xg.astype(BF16), wri_ref[g], preferred_element_type=F32)
        r = jax.nn.sigmoid(ri[:, :blk] + br_ref[:, cols])
        ig = jax.nn.sigmoid(ri[:, blk:] + bi_ref[:, cols])
        log_a = (RG_C * r) * _log_sigmoid(lam_ref[:, cols])
        a = jnp.exp(log_a)
        a_scr[:, cols] = a
        z = 1.0 - a * a
        b_scr[:, cols] = jnp.where(z > 0.0, z * lax.rsqrt(z), z) * (ig * xg)

    row = lax.broadcasted_iota(jnp.int32, (SUBLANES, width), 0)

    def body(i, h):
        r0 = i * SUBLANES
        a = a_scr[pl.ds(r0, SUBLANES), :]
        b = b_scr[pl.ds(r0, SUBLANES), :]
        for sh in (1, 2, 4):
            keep = row >= sh
            a_prev = jnp.where(keep, pltpu.roll(a, sh, 0), 1.0)
            b_prev = jnp.where(keep, pltpu.roll(b, sh, 0), 0.0)
            b = a * b_prev + b
            a = a * a_prev
        hs = a * h + b
        hs_scr[pl.ds(r0, SUBLANES), :] = hs
        return jnp.broadcast_to(hs[SUBLANES - 1:SUBLANES, :], (SUBLANES, width))

    h = h_scr[...]
    for i in range(ts // SUBLANES):
        h = body(i, h)
    h_scr[...] = h
    y = (hs_scr[...] * jax.nn.gelu(gate_ref[...].astype(F32))).astype(BF16)
    _project_residual(y, wbf_scr, x_ref, mgate_ref, o_ref)


def _rg_mid_call(p, conv_w, conv_b, w_ri, b_r, b_i, lam, w_out, widx, x, mod_r, layer, batch, ts=CORE_ROWS):
    m, two_w = p.shape
    width = two_w // 2
    d = x.shape[1]
    seq = m // batch
    ts = min(ts, seq)
    nst = seq // ts
    n_blocks, blk, _ = w_ri.shape
    full = lambda shape: pl.BlockSpec(shape, lambda b, s: (0,) * len(shape))
    return pl.pallas_call(
        _rg_mid_kernel,
        out_shape=jax.ShapeDtypeStruct((m, d), F32),
        grid=(batch, nst),
        in_specs=[
            pl.BlockSpec((ts, width), lambda b, s: (b * nst + s, 0)),
            pl.BlockSpec((ts, width), lambda b, s: (b * nst + s, 1)),
            full(conv_w.shape),
            full((1, width)),
            full((n_blocks, blk, 2 * blk)),
            full((1, width)),
            full((1, width)),
            full((1, width)),
        ] + _res_specs(w_out, widx, ts, nst, d, layer * batch * 9 + 5),
        out_specs=pl.BlockSpec((ts, d), lambda b, s: (b * nst + s, 0)),
        scratch_shapes=[
            pltpu.VMEM((ts + SUBLANES, width), F32),
            pltpu.VMEM((ts, width), F32),
            pltpu.VMEM((ts, width), F32),
            pltpu.VMEM((ts, width), F32),
            pltpu.VMEM((SUBLANES, width), F32),
            _w_out_scratch(w_out, d),
        ],
        compiler_params=_cparams(2),
        name="rglru_core",
    )(p, p, conv_w, conv_b.reshape(1, width), w_ri, b_r.reshape(1, width), b_i.reshape(1, width),
      lam.reshape(1, width), w_out, x, mod_r)


def _sc_mid_kernel(bg_ref, cg_ref, xv_ref, cw_ref, wout_ref, x_ref, mgate_ref, o_ref, ext_scr, wbf_scr):
    ts = bg_ref.shape[0]
    _cast_w_out_once(wout_ref, wbf_scr)
    _shift_in_tile(ext_scr, cg_ref[...].astype(F32) * xv_ref[...].astype(F32), ts)
    conv = _causal_conv_from_ext(ext_scr, cw_ref, ts)
    y = (bg_ref[...].astype(F32) * conv).astype(BF16)
    _project_residual(y, wbf_scr, x_ref, mgate_ref, o_ref)


def _sc_mid_call(p, conv_w, w_out, widx, x, mod_r, layer, batch, ts=CORE_ROWS):
    m, three_d = p.shape
    d = three_d // 3
    seq = m // batch
    ts = min(ts, seq)
    nst = seq // ts
    return pl.pallas_call(
        _sc_mid_kernel,
        out_shape=jax.ShapeDtypeStruct((m, d), F32),
        grid=(batch, nst),
        in_specs=[
            pl.BlockSpec((ts, d), lambda b, s: (b * nst + s, 0)),
            pl.BlockSpec((ts, d), lambda b, s: (b * nst + s, 1)),
            pl.BlockSpec((ts, d), lambda b, s: (b * nst + s, 2)),
            pl.BlockSpec(conv_w.shape, lambda b, s: (0, 0)),
        ] + _res_specs(w_out, widx, ts, nst, d, layer * batch * 9 + 5),
        out_specs=pl.BlockSpec((ts, d), lambda b, s: (b * nst + s, 0)),
        scratch_shapes=[pltpu.VMEM((ts + SUBLANES, d), F32), _w_out_scratch(w_out, d)],
        compiler_params=_cparams(2),
        name="shortconv_core",
    )(p, p, p, conv_w, w_out, x, mod_r)


def _fox_prep_kernel(q_ref, k_ref, v_ref, fl_ref, bf_ref, qg_ref, kg_ref, qa_ref, ka_ref, vt_ref, carry_scr):
    ts, d = q_ref.shape

    @pl.when(pl.program_id(1) == 0)
    def _():
        carry_scr[...] = jnp.zeros_like(carry_scr)

    logf = _log_sigmoid(fl_ref[...] + bf_ref[...])
    r = lax.broadcasted_iota(jnp.int32, (ts, ts), 0)
    c = lax.broadcasted_iota(jnp.int32, (ts, ts), 1)
    tril = jnp.where(c <= r, 1.0, 0.0).astype(BF16)
    cum = _tril_dot(tril, logf) + carry_scr[0:1, :]
    carry_scr[...] = jnp.broadcast_to(cum[ts - 1:ts, :], carry_scr.shape)
    f_over_scale = cum * math.sqrt(HEAD_DIM)

    lane = lax.broadcasted_iota(jnp.int32, (ts, HEAD_DIM), 1)
    for h in range(d // HEAD_DIM):
        cols = slice(h * HEAD_DIM, (h + 1) * HEAD_DIM)
        hi, mid, lo = (part.astype(F32) for part in _split3(f_over_scale[:, h:h + 1]))
        parts = jnp.where((lane == 0) | (lane == 3), hi, jnp.where((lane == 1) | (lane == 4), mid, lo))
        q_aug = jnp.where(lane < 3, parts, jnp.where(lane < 6, 1.0, 0.0))
        k_aug = jnp.where(lane < 3, 1.0, jnp.where(lane < 6, -parts, 0.0))
        for src, gain, aug, dst in ((q_ref, qg_ref, q_aug, qa_ref), (k_ref, kg_ref, k_aug, ka_ref)):
            t = src[:, cols].astype(F32)
            ms = jnp.mean(t * t, axis=-1, keepdims=True)
            dst[:, 2 * h * HEAD_DIM:(2 * h + 1) * HEAD_DIM] = ((t * lax.rsqrt(ms + EPS)) * gain[...]).astype(BF16)
            dst[:, (2 * h + 1) * HEAD_DIM:(2 * h + 2) * HEAD_DIM] = aug.astype(BF16)
        vt_ref[h] = v_ref[:, cols].astype(F32).T.astype(BF16)


def _fox_prep_call(p, fl, b_f_pad, q_gain, k_gain, batch, ts):
    m = p.shape[0]
    d = p.shape[1] // 4
    n_heads = d // HEAD_DIM
    seq = m // batch
    nst = seq // ts
    lanes = fl.shape[1]
    return pl.pallas_call(
        _fox_prep_kernel,
        out_shape=(
            jax.ShapeDtypeStruct((m, 2 * d), BF16),
            jax.ShapeDtypeStruct((m, 2 * d), BF16),
            jax.ShapeDtypeStruct((batch, nst, n_heads, HEAD_DIM, ts), BF16),
        ),
        grid=(batch, nst),
        in_specs=[
            pl.BlockSpec((ts, d), lambda b, s: (b * nst + s, 0)),
            pl.BlockSpec((ts, d), lambda b, s: (b * nst + s, 1)),
            pl.BlockSpec((ts, d), lambda b, s: (b * nst + s, 2)),
            pl.BlockSpec((ts, lanes), lambda b, s: (b * nst + s, 0)),
            pl.BlockSpec((1, lanes), lambda b, s: (0, 0)),
            pl.BlockSpec((1, HEAD_DIM), lambda b, s: (0, 0)),
            pl.BlockSpec((1, HEAD_DIM), lambda b, s: (0, 0)),
        ],
        out_specs=(
            pl.BlockSpec((ts, 2 * d), lambda b, s: (b * nst + s, 0)),
            pl.BlockSpec((ts, 2 * d), lambda b, s: (b * nst + s, 0)),
            pl.BlockSpec((None, None, n_heads, HEAD_DIM, ts), lambda b, s: (b, s, 0, 0, 0)),
        ),
        scratch_shapes=[pltpu.VMEM((SUBLANES, lanes), F32)],
        compiler_params=_cparams(2),
        name="fox_prep",
    )(p, p, p, fl, b_f_pad, q_gain.reshape(1, HEAD_DIM), k_gain.reshape(1, HEAD_DIM))


def _fox_flash_kernel(q_ref, k_ref, vt_ref, g_ref, o_ref):
    t = q_ref.shape[0]
    _, hp, dh, _ = vt_ref.shape
    qi = pl.program_id(2)
    to_log2 = (1.0 / math.sqrt(dh)) * math.log2(math.e)
    nt = (((1,), (1,)), ((), ()))

    def scores(j, e):
        cols = slice(2 * e * dh, 2 * (e + 1) * dh)
        kj = k_ref[pl.ds(pl.multiple_of(j * t, t), t), cols]
        return lax.dot_general(kj, q_ref[:, cols], nt, preferred_element_type=F32)

    def step(j, carry, masked):
        raw = [scores(j, e) for e in range(hp)]
        out = []
        for e in range(hp):
            m_prev, l_prev, acc = carry[e]
            s = raw[e] * to_log2
            if masked:
                key = lax.broadcasted_iota(jnp.int32, (t, t), 0)
                qry = lax.broadcasted_iota(jnp.int32, (t, t), 1)
                s = jnp.where(key <= qry, s, -jnp.inf)
            m_new = jnp.maximum(m_prev, jnp.max(s, axis=0, keepdims=True))
            alpha = jnp.exp2(m_prev - m_new)
            p = jnp.exp2(s - m_new)
            l_new = alpha * l_prev + jnp.sum(p, axis=0, keepdims=True)
            acc = alpha * acc + jnp.dot(vt_ref[j, e], p.astype(BF16), preferred_element_type=F32)
            out.append((m_new, l_new, acc))
        return tuple(out)

    init = tuple((jnp.full((1, t), -jnp.inf, F32), jnp.zeros((1, t), F32), jnp.zeros((dh, t), F32))
                 for _ in range(hp))

    carry = lax.fori_loop(0, qi, functools.partial(step, masked=False), init)
    fin = step(qi, carry, True)
    for e in range(hp):
        _, l_fin, acc = fin[e]
        cols = slice(e * dh, (e + 1) * dh)
        o = (acc / l_fin).T
        o_ref[:, cols] = (o * jax.nn.sigmoid(g_ref[:, cols].astype(F32))).astype(BF16)


def _fox_flash_call(qa, ka, vt, p, batch, t, hp=ATTN_HEADS):
    m = qa.shape[0]
    n_heads = vt.shape[2]
    d = n_heads * HEAD_DIM
    seq = m // batch
    nq = seq // t
    hp = math.gcd(hp, n_heads)
    g_blk = 3 * n_heads // hp
    return pl.pallas_call(
        _fox_flash_kernel,
        out_shape=jax.ShapeDtypeStruct((m, d), BF16),
        grid=(batch, n_heads // hp, nq),
        in_specs=[
            pl.BlockSpec((t, 2 * hp * HEAD_DIM), lambda b, h, i: (b * nq + i, h)),
            pl.BlockSpec((seq, 2 * hp * HEAD_DIM), lambda b, h, i: (b, h)),
            pl.BlockSpec((None, nq, hp, HEAD_DIM, t), lambda b, h, i: (b, 0, h, 0, 0)),
            pl.BlockSpec((t, hp * HEAD_DIM), lambda b, h, i: (b * nq + i, g_blk + h)),
        ],
        out_specs=pl.BlockSpec((t, hp * HEAD_DIM), lambda b, h, i: (b * nq + i, h)),
        compiler_params=_cparams(3),
        name="fox_flash",
    )(qa, ka, vt, p)


def _hg_mid_kernel(q_ref, fz_ref, iv_ref, g_ref, lbl_ref, ng_ref, wout_ref, x_ref, mgate_ref, o_ref,
                   qin_scr, kin_scr, kdec_scr, dec_scr, st_scr, y_scr, wbf_scr, *, layer):
    ts, d = q_ref.shape
    n_heads = d // HEAD_DIM
    n_chunks = ts // CHUNK
    _cast_w_out_once(wout_ref, wbf_scr)

    @pl.when(pl.program_id(1) == 0)
    def _():
        st_scr[...] = jnp.zeros_like(st_scr)

    lg = lbl_ref[...]
    depth = lg.shape[0]
    mx = lg[0:1, :]
    for i in range(1, depth):
        mx = jnp.maximum(mx, lg[i:i + 1, :])
    es = [jnp.exp(lg[i:i + 1, :] - mx) for i in range(depth)]
    tot = es[0]
    for i in range(1, depth):
        tot = tot + es[i]
    lb = jnp.zeros_like(tot)
    for i in range(layer):
        lb = lb + es[i] / tot

    r = lax.broadcasted_iota(jnp.int32, (ts, ts), 0)
    c = lax.broadcasted_iota(jnp.int32, (ts, ts), 1)
    tril = jnp.where((c <= r) & (jnp.right_shift(c, CHUNK_SHIFT) == jnp.right_shift(r, CHUNK_SHIFT)), 1.0, 0.0).astype(BF16)

    slab = min(d, 4 * LANES)
    for c0 in range(0, d, slab):
        cols = slice(c0, c0 + slab)
        lbc = lb[:, cols]
        sig = jax.nn.sigmoid(fz_ref[:, cols].astype(F32))
        logf = jnp.log(lbc + (1.0 - lbc) * sig)
        kk = (1.0 - lbc) * (1.0 - sig)
        bcum = _tril_dot(tril, logf)
        qin_scr[:, cols] = (q_ref[:, cols].astype(F32) * jnp.exp(bcum)).astype(BF16)
        kin_scr[:, cols] = (kk * jnp.exp(-bcum)).astype(BF16)
        for ci in range(n_chunks):
            rows = slice(ci * CHUNK, (ci + 1) * CHUNK)
            b_last = bcum[(ci + 1) * CHUNK - 1:(ci + 1) * CHUNK, :]
            kdec_scr[rows, cols] = (kk[rows, :] * jnp.exp(b_last - bcum[rows, :])).astype(BF16)
            dec_scr[ci, :, cols] = jnp.broadcast_to(jnp.exp(b_last), (SUBLANES, slab))

    rr = lax.broadcasted_iota(jnp.int32, (CHUNK, CHUNK), 0)
    cc = lax.broadcasted_iota(jnp.int32, (CHUNK, CHUNK), 1)
    causal = cc <= rr
    nt = (((1,), (1,)), ((), ()))
    tn = (((0,), (0,)), ((), ()))

    def chunk_body(ci, carry):
        r0 = ci * CHUNK
        rows = pl.ds(r0, CHUNK)
        heads = [slice(h * HEAD_DIM, (h + 1) * HEAD_DIM) for h in range(n_heads)]
        qh = [qin_scr[rows, c] for c in heads]
        vh = [iv_ref[rows, c].astype(BF16) for c in heads]
        st = [st_scr[h] for h in range(n_heads)]
        sc = [lax.dot_general(qh[h], kin_scr[rows, c], nt, preferred_element_type=F32)
              for h, c in enumerate(heads)]
        o_inter = [lax.dot_general(qh[h], st[h].astype(BF16), nt, preferred_element_type=F32)
                   for h in range(n_heads)]
        u_t = [lax.dot_general(vh[h], kdec_scr[rows, c], tn, preferred_element_type=F32)
               for h, c in enumerate(heads)]
        for h, c in enumerate(heads):
            scm = jnp.where(causal, sc[h], 0.0).astype(BF16)
            o = jnp.dot(scm, vh[h], preferred_element_type=F32) + o_inter[h]
            st_scr[h] = st[h] * dec_scr[ci][0:1, c] + u_t[h]
            ms = jnp.mean(o * o, axis=-1, keepdims=True)
            on = (o * lax.rsqrt(ms + EPS)) * ng_ref[...]
            gh = g_ref[rows, c].astype(F32)
            y_scr[rows, c] = (on * (gh * jax.nn.sigmoid(gh))).astype(BF16)
        return carry

    for ci in range(n_chunks):
        chunk_body(ci, 0)
    _project_residual(y_scr[...], wbf_scr, x_ref, mgate_ref, o_ref)


def _hg_mid_call(p, lb_logits, norm_gain, w_out, widx, x, mod_r, layer, batch, ts=CORE_ROWS):
    m, four_d = p.shape
    d = four_d // 4
    n_heads = d // HEAD_DIM
    seq = m // batch
    ts = min(ts, seq)
    nst = seq // ts
    return pl.pallas_call(
        functools.partial(_hg_mid_kernel, layer=layer),
        out_shape=jax.ShapeDtypeStruct((m, d), F32),
        grid=(batch, nst),
        in_specs=[
            pl.BlockSpec((ts, d), lambda b, s: (b * nst + s, 0)),
            pl.BlockSpec((ts, d), lambda b, s: (b * nst + s, 1)),
            pl.BlockSpec((ts, d), lambda b, s: (b * nst + s, 2)),
            pl.BlockSpec((ts, d), lambda b, s: (b * nst + s, 3)),
            pl.BlockSpec(lb_logits.shape, lambda b, s: (0, 0)),
            pl.BlockSpec((1, HEAD_DIM), lambda b, s: (0, 0)),
        ] + _res_specs(w_out, widx, ts, nst, d, layer * batch * 9 + 5),
        out_specs=pl.BlockSpec((ts, d), lambda b, s: (b * nst + s, 0)),
        scratch_shapes=[
            pltpu.VMEM((ts, d), BF16),
            pltpu.VMEM((ts, d), BF16),
            pltpu.VMEM((ts, d), BF16),
            pltpu.VMEM((ts // CHUNK, SUBLANES, d), F32),
            pltpu.VMEM((n_heads, HEAD_DIM, HEAD_DIM), F32),
            pltpu.VMEM((ts, d), BF16),
            _w_out_scratch(w_out, d),
        ],
        compiler_params=_cparams(2),
        name="hgrn2_core",
    )(p, p, p, p, lb_logits, norm_gain.reshape(1, HEAD_DIM), w_out, x, mod_r)


def kernel(x, c, ada_w, ada_b, norm_g, ffn_w_gate, ffn_w_up, ffn_w_down, rg_w_in, rg_conv_w, rg_conv_b, rg_w_r, rg_b_r, rg_w_i, rg_b_i, rg_lam, rg_w_out, sc_w_in, sc_conv_w, sc_w_out, fox_w_in, fox_b_f, fox_q_gain, fox_k_gain, fox_w_out, hg_w_in, hg_lb_logits, hg_norm_gain, hg_w_out):
    batch, seq, d = x.shape
    depth = ada_w.shape[0]
    n_heads = d // HEAD_DIM
    assert d % HEAD_DIM == 0 and seq % CHUNK == 0 and seq % SUBLANES == 0
    mid_dtype = BF16

    mod = _ada_call(c, ada_w, ada_b)
    mod_r = mod.reshape(depth * batch * 9, 1, d)
    ng_r = norm_g.reshape(depth * 3, 1, d)

    ffn_w = (ffn_w_gate, ffn_w_up, ffn_w_down)
    w_cur = None

    def ffn(xf, w_cur, k, sub):
        nxt = (ffn_w, (k + 1) // 2, (k + 1) % 2) if k + 1 < 2 * depth else None
        if w_cur is None:
            return _ffn_call(xf, mod_r, ng_r, ffn_w, k // 2, sub, batch, nxt, w_index=(0, 0),
                             tf=FFN_HIDDEN_F32)
        return _ffn_call(xf, mod_r, ng_r, w_cur, k // 2, sub, batch, nxt)

    xf = x.reshape(batch * seq, d)
    for i in range(depth):
        xf, w_cur = ffn(xf, w_cur, 2 * i, 0)
        m, j = i % N_MIXERS, i // N_MIXERS
        if m == 0:
            p = _mm_mod_call(xf, mod_r, ng_r, rg_w_in, j, rg_w_in.shape[2], i, batch, mid_dtype)
            w_ri = jnp.concatenate([rg_w_r[j], rg_w_i[j]], axis=-1).astype(BF16)
            xf = _rg_mid_call(p, rg_conv_w[j], rg_conv_b[j], w_ri, rg_b_r[j], rg_b_i[j], rg_lam[j],
                              rg_w_out, j, xf, mod_r, i, batch)
        elif m == 1:
            p = _mm_mod_call(xf, mod_r, ng_r, sc_w_in, j, sc_w_in.shape[2], i, batch, mid_dtype)
            xf = _sc_mid_call(p, sc_conv_w[j], sc_w_out, j, xf, mod_r, i, batch)
        elif m == 2:
            w_in_t = jnp.swapaxes(fox_w_in, 1, 2)
            w_fl = jnp.pad(w_in_t[j, 4 * d:], ((0, LANES - n_heads), (0, 0)))
            p, fl = _mm_mod_call(xf, mod_r, ng_r, w_in_t, j, 4 * d, i, batch, mid_dtype, w_t=True, w_side=w_fl)
            b_f_pad = jnp.pad(fox_b_f[j], (0, LANES - n_heads)).reshape(1, LANES)
            t_attn = min(ATTN_TILE, seq)
            qa, ka, vt = _fox_prep_call(p, fl, b_f_pad, fox_q_gain[j], fox_k_gain[j], batch, t_attn)
            y = _fox_flash_call(qa, ka, vt, p, batch, t_attn)
            xf = _mm_res_call(y, fox_w_out, j, xf, mod_r, i, batch)
        else:
            p = _mm_mod_call(xf, mod_r, ng_r, hg_w_in, j, hg_w_in.shape[2], i, batch, mid_dtype)
            xf = _hg_mid_call(p, hg_lb_logits, hg_norm_gain[j], hg_w_out, j, xf, mod_r, i, batch)
        xf, w_cur = ffn(xf, w_cur, 2 * i + 1, 2)
    return xf.reshape(batch, seq, d)
```
